```python
import jax
import jax.numpy as jnp
from jax import lax
import numpy as np

D_MODEL = 1024
BATCH = 2
SEQ = 16384
DEPTH = 2
DEC_BATCH = 16
DEC_SEQ = 64
PAST_LEN = 2048

CHUNK = 64
Q_BLOCK = 128
HEAD_DIM = 64
N_FOX_HEADS = 8
N_SB_HEADS = 8
N_SWA_HEADS = 16
N_SWA_KV_HEADS = 4
SWA_GROUP = N_SWA_HEADS // N_SWA_KV_HEADS
WINDOW = 128
WINDOW_CHUNKS = WINDOW // CHUNK
BAND = (WINDOW_CHUNKS + 1) * CHUNK
ROPE_THETA = 500000.0
ROPE_DIM = HEAD_DIM // 4
D_FF = ((8 * D_MODEL + 3 * 256 - 1) // (3 * 256)) * 256
PE_DIM = 256
RMS_EPS = 1e-6
FORGET_BIAS_INIT = 3.0
SCALE = HEAD_DIM ** -0.5
N_AB_LAYERS = (DEPTH + 1) // 2
N_C_LAYERS = DEPTH // 2
FOX_W = N_FOX_HEADS * HEAD_DIM
SB_W = N_SB_HEADS * HEAD_DIM
AB_IN_SPLITS = (FOX_W, 2 * FOX_W, 3 * FOX_W, 3 * FOX_W + N_FOX_HEADS,
                3 * FOX_W + N_FOX_HEADS + SB_W, 3 * FOX_W + N_FOX_HEADS + 2 * SB_W)
AB_IN_WIDTH = 3 * FOX_W + N_FOX_HEADS + 3 * SB_W
C_Q_W = N_SWA_HEADS * HEAD_DIM
C_KV_W = N_SWA_KV_HEADS * HEAD_DIM
C_IN_SPLITS = (C_Q_W, C_Q_W + C_KV_W)
C_IN_WIDTH = C_Q_W + 2 * C_KV_W

kernel_name = "fox_stickbreak_swa_sink_streaming_trunk"


def _rmsnorm(x, g):
    xf = x.astype(jnp.float32)
    y = xf * lax.rsqrt(jnp.mean(xf * xf, axis=-1, keepdims=True) + RMS_EPS)
    return (y * g.astype(jnp.float32)).astype(x.dtype)


def _rope_partial(x, pos):
    half = ROPE_DIM // 2
    inv_freq = ROPE_THETA ** (-2.0 * jnp.arange(half, dtype=jnp.float32) / ROPE_DIM)
    ang = pos.astype(jnp.float32)[:, None] * inv_freq[None, :]
    cos = jnp.cos(ang)[None, :, None, :]
    sin = jnp.sin(ang)[None, :, None, :]
    xf = x.astype(jnp.float32)
    x1 = xf[..., :half]
    x2 = xf[..., half:ROPE_DIM]
    out = jnp.concatenate([x1 * cos - x2 * sin, x2 * cos + x1 * sin, xf[..., ROPE_DIM:]], axis=-1)
    return out.astype(x.dtype)


def _fox_attend(q, k, v, cq, ck, qpos, kpos):
    s = jnp.einsum("bqhd,bkhd->bhqk", q, k).astype(jnp.float32) * SCALE
    s = s + jnp.swapaxes(cq, 1, 2)[:, :, :, None] - jnp.swapaxes(ck, 1, 2)[:, :, None, :]
    s = jnp.where(kpos[None, :] <= qpos[:, None], s, -jnp.inf)
    p = jax.nn.softmax(s, axis=-1)
    return jnp.einsum("bhqk,bkhd->bqhd", p.astype(v.dtype), v)


def _sb_attend(q, k, v, qpos, kpos):
    z = jnp.einsum("bqhd,bkhd->bhqk", q, k).astype(jnp.float32) * SCALE
    mask = kpos[None, :] < qpos[:, None]
    log_1m = jnp.where(mask, jax.nn.log_sigmoid(-z), 0.0)
    between = lax.cumsum(log_1m, axis=3, reverse=True) - log_1m
    a = jnp.where(mask, jnp.exp(jax.nn.log_sigmoid(z) + between), 0.0)
    return jnp.einsum("bhqk,bkhd->bqhd", a.astype(v.dtype), v)


def _ab_project(xn, w_in, b_f):
    B, T, _ = xn.shape
    fq, fk, fv, flog, sq, sk, sv = jnp.split(xn @ w_in, AB_IN_SPLITS, axis=-1)
    fox = [t.reshape(B, T, N_FOX_HEADS, HEAD_DIM) for t in (fq, fk, fv)]
    sb = [t.reshape(B, T, N_SB_HEADS, HEAD_DIM) for t in (sq, sk, sv)]
    logf = jax.nn.log_sigmoid((flog + b_f).astype(jnp.float32))
    return fox[0], fox[1], fox[2], logf, sb[0], sb[1], sb[2]


def _ab_merge(fo, so, w_out):
    B, T = fo.shape[:2]
    return jnp.concatenate([fo.reshape(B, T, FOX_W), so.reshape(B, T, SB_W)], axis=-1) @ w_out


def _ab_mixer_prompt(xn, w_in, b_f, w_out):
    fq, fk, fv, logf, sq, sk, sv = _ab_project(xn, w_in, b_f)
    B, S = xn.shape[:2]
    c = jnp.cumsum(logf, axis=1)
    kpos = jnp.arange(S)

    def block(i):
        start = i * Q_BLOCK
        qpos = start + jnp.arange(Q_BLOCK)
        sl = lambda t: lax.dynamic_slice_in_dim(t, start, Q_BLOCK, axis=1)
        return (_fox_attend(sl(fq), fk, fv, sl(c), c, qpos, kpos),
                _sb_attend(sl(sq), sk, sv, qpos, kpos))

    fo, so = lax.map(block, jnp.arange(S // Q_BLOCK))
    fo = jnp.swapaxes(fo, 0, 1).reshape(B, S, N_FOX_HEADS, HEAD_DIM)
    so = jnp.swapaxes(so, 0, 1).reshape(B, S, N_SB_HEADS, HEAD_DIM)
    return _ab_merge(fo, so, w_out), fk, fv, logf, sk, sv


def _ab_mixer_sample(xn, w_in, b_f, w_out, c_fk, c_fv, c_fl, c_sk, c_sv):
    fq, fk, fv, logf, sq, sk, sv = _ab_project(xn, w_in, b_f)
    P, T = c_fk.shape[1], xn.shape[1]
    c = jnp.cumsum(jnp.concatenate([c_fl.astype(jnp.float32), logf], axis=1), axis=1)
    qpos = P + jnp.arange(T)
    kpos = jnp.arange(P + T)
    fo = _fox_attend(fq, jnp.concatenate([c_fk, fk], axis=1), jnp.concatenate([c_fv, fv], axis=1),
                     c[:, P:], c, qpos, kpos)
    so = _sb_attend(sq, jnp.concatenate([c_sk, sk], axis=1), jnp.concatenate([c_sv, sv], axis=1),
                    qpos, kpos)
    return _ab_merge(fo, so, w_out), fk, fv, logf, sk, sv


def _c_project(xn, w_in, pos):
    B, T, _ = xn.shape
    q, k, v = jnp.split(xn @ w_in, C_IN_SPLITS, axis=-1)
    q = _rope_partial(q.reshape(B, T, N_SWA_HEADS, HEAD_DIM), pos)
    k = _rope_partial(k.reshape(B, T, N_SWA_KV_HEADS, HEAD_DIM), pos)
    return q, k, v.reshape(B, T, N_SWA_KV_HEADS, HEAD_DIM)


def _swa_attend(q, k, v, sinks, mask):
    B, N, Q, H, dh = q.shape
    qg = q.reshape(B, N, Q, N_SWA_KV_HEADS, SWA_GROUP, dh)
    s = jnp.einsum("bnqhgd,bnkhd->bnhgqk", qg, k).astype(jnp.float32) * SCALE
    s = jnp.where(mask[None, :, None, None], s, -jnp.inf)
    sink = sinks.astype(jnp.float32).reshape(1, 1, N_SWA_KV_HEADS, SWA_GROUP, 1, 1)
    m = jnp.maximum(jnp.max(s, axis=-1, keepdims=True), sink)
    e = jnp.exp(s - m)
    p = e / (jnp.sum(e, axis=-1, keepdims=True) + jnp.exp(sink - m))
    o = jnp.einsum("bnhgqk,bnkhd->bnqhgd", p.astype(v.dtype), v)
    return o.reshape(B, N, Q, H, dh)


def _band(t, n_chunks):
    B = t.shape[0]
    tp = jnp.pad(t, ((0, 0), (WINDOW, 0), (0, 0), (0, 0)))
    tp = tp.reshape(B, n_chunks + WINDOW_CHUNKS, CHUNK, N_SWA_KV_HEADS, HEAD_DIM)
    return jnp.concatenate([tp[:, j:j + n_chunks] for j in range(WINDOW_CHUNKS + 1)], axis=2)


def _c_mixer_prompt(xn, w_in, sinks, w_out):
    B, S = xn.shape[:2]
    q, k, v = _c_project(xn, w_in, jnp.arange(S))
    nc = S // CHUNK
    kpos = (jnp.arange(nc)[:, None] - WINDOW_CHUNKS) * CHUNK + jnp.arange(BAND)[None, :]
    o = _swa_attend(q.reshape(B, nc, CHUNK, N_SWA_HEADS, HEAD_DIM), _band(k, nc), _band(v, nc),
                    sinks, (kpos >= 0)[:, None, :])
    return o.reshape(B, S, C_Q_W) @ w_out, k[:, S - WINDOW:], v[:, S - WINDOW:]


def _c_mixer_sample(xn, w_in, sinks, w_out, c_k, c_v):
    B, T = xn.shape[:2]
    q, k, v = _c_project(xn, w_in, PAST_LEN + jnp.arange(T))
    k_all = jnp.concatenate([c_k, k], axis=1)
    v_all = jnp.concatenate([c_v, v], axis=1)
    mask = jnp.ones((1, 1, k_all.shape[1]), dtype=bool)
    o = _swa_attend(q[:, None], k_all[:, None], v_all[:, None], sinks, mask)
    return o.reshape(B, T, C_Q_W) @ w_out, k_all[:, -WINDOW:], v_all[:, -WINDOW:]


def _swiglu(h, g, w_gate, w_up, w_down):
    hn = _rmsnorm(h, g)
    return h + (jax.nn.silu(hn @ w_gate) * (hn @ w_up)) @ w_down


def _per_layer_embed(h, p_i, g, w_gate, w_proj):
    gate = jax.nn.sigmoid(_rmsnorm(h, g) @ w_gate)
    return h + (p_i @ w_proj) * gate


def _trunk(x, p, W, caches):
    states = tuple([] for _ in range(7))
    h = x
    for i in range(DEPTH):
        j = i // 2
        hn = _rmsnorm(h, W["norm_mix"][i])
        if i % 2 == 0:
            args = (hn, W["w_ab_in"][j], W["b_fox_f"][j], W["w_ab_out"][j])
            if caches is None:
                out, *new = _ab_mixer_prompt(*args)
            else:
                out, *new = _ab_mixer_sample(*args, *(c[j] for c in caches[:5]))
            for lst, t in zip(states[:5], new):
                lst.append(t)
        else:
            args = (hn, W["w_c_in"][j], W["c_sinks"][j], W["w_c_out"][j])
            if caches is None:
                out, *new = _c_mixer_prompt(*args)
            else:
                out, *new = _c_mixer_sample(*args, caches[5][j], caches[6][j])
            for lst, t in zip(states[5:], new):
                lst.append(t)
        h = h + out
        h = _swiglu(h, W["norm_ffn"][i], W["w_ffn_gate"][i], W["w_ffn_up"][i], W["w_ffn_down"][i])
        h = _per_layer_embed(h, p[i], W["norm_pe"][i], W["w_pe_gate"][i], W["w_pe_proj"][i])
    y = _rmsnorm(h, W["norm_final"])
    return y, tuple(jnp.stack(lst, axis=0) for lst in states)


def setup_inputs(seed: int = 0) -> dict:
    key = jax.random.key(seed)
    ks = jax.random.split(key, 26)
    f32 = jnp.float32

    def nrm(k, shape, scale=1.0):
        return jax.random.normal(k, shape, f32) * scale

    fox_cache = (N_AB_LAYERS, DEC_BATCH, PAST_LEN, N_FOX_HEADS, HEAD_DIM)
    sb_cache = (N_AB_LAYERS, DEC_BATCH, PAST_LEN, N_SB_HEADS, HEAD_DIM)
    swa_cache = (N_C_LAYERS, DEC_BATCH, WINDOW, N_SWA_KV_HEADS, HEAD_DIM)
    return {
        "x_prompt": nrm(ks[0], (BATCH, SEQ, D_MODEL)),
        "x_sample": nrm(ks[1], (DEC_BATCH, DEC_SEQ, D_MODEL)),
        "cache_fox_k": nrm(ks[2], fox_cache),
        "cache_fox_v": nrm(ks[3], fox_cache),
        "cache_fox_logf": jax.nn.log_sigmoid(FORGET_BIAS_INIT + nrm(ks[4], fox_cache[:4])),
        "cache_sb_k": nrm(ks[5], sb_cache),
        "cache_sb_v": nrm(ks[6], sb_cache),
        "cache_swa_k": nrm(ks[7], swa_cache),
        "cache_swa_v": nrm(ks[8], swa_cache),
        "p_prompt": nrm(ks[9], (DEPTH, BATCH, SEQ, PE_DIM)),
        "p_sample": nrm(ks[10], (DEPTH, DEC_BATCH, DEC_SEQ, PE_DIM)),
        "norm_mix": 1.0 + nrm(ks[11], (DEPTH, D_MODEL), 0.05),
        "w_ab_in": nrm(ks[12], (N_AB_LAYERS, D_MODEL, AB_IN_WIDTH), D_MODEL ** -0.5),
        "b_fox_f": FORGET_BIAS_INIT + nrm(ks[13], (N_AB_LAYERS, N_FOX_HEADS), 0.1),
        "w_ab_out": nrm(ks[14], (N_AB_LAYERS, FOX_W + SB_W, D_MODEL), (FOX_W + SB_W) ** -0.5),
        "w_c_in": nrm(ks[15], (N_C_LAYERS, D_MODEL, C_IN_WIDTH), D_MODEL ** -0.5),
        "c_sinks": nrm(ks[16], (N_C_LAYERS, N_SWA_HEADS), 0.5),
        "w_c_out": nrm(ks[17], (N_C_LAYERS, C_Q_W, D_MODEL), C_Q_W ** -0.5),
        "norm_ffn": 1.0 + nrm(ks[18], (DEPTH, D_MODEL), 0.05),
        "w_ffn_gate": nrm(ks[19], (DEPTH, D_MODEL, D_FF), D_MODEL ** -0.5),
        "w_ffn_up": nrm(ks[20], (DEPTH, D_MODEL, D_FF), D_MODEL ** -0.5),
        "w_ffn_down": nrm(ks[21], (DEPTH, D_FF, D_MODEL), D_FF ** -0.5),
        "norm_pe": 1.0 + nrm(ks[22], (DEPTH, D_MODEL), 0.05),
        "w_pe_gate": nrm(ks[23], (DEPTH, D_MODEL, D_MODEL), D_MODEL ** -0.5),
        "w_pe_proj": nrm(ks[24], (DEPTH, PE_DIM, D_MODEL), PE_DIM ** -0.5),
        "norm_final": 1.0 + nrm(ks[25], (D_MODEL,), 0.05),
    }


def reference(x_prompt, x_sample, cache_fox_k, cache_fox_v, cache_fox_logf, cache_sb_k, cache_sb_v,
              cache_swa_k, cache_swa_v, p_prompt, p_sample, norm_mix, w_ab_in, b_fox_f, w_ab_out,
              w_c_in, c_sinks, w_c_out, norm_ffn, w_ffn_gate, w_ffn_up, w_ffn_down, norm_pe,
              w_pe_gate, w_pe_proj, norm_final):
    W = {
        "norm_mix": norm_mix, "w_ab_in": w_ab_in, "b_fox_f": b_fox_f, "w_ab_out": w_ab_out,
        "w_c_in": w_c_in, "c_sinks": c_sinks, "w_c_out": w_c_out,
        "norm_ffn": norm_ffn, "w_ffn_gate": w_ffn_gate, "w_ffn_up": w_ffn_up, "w_ffn_down": w_ffn_down,
        "norm_pe": norm_pe, "w_pe_gate": w_pe_gate, "w_pe_proj": w_pe_proj, "norm_final": norm_final,
    }
    caches = (cache_fox_k, cache_fox_v, cache_fox_logf, cache_sb_k, cache_sb_v, cache_swa_k, cache_swa_v)
    y_prompt, (pf_k, pf_v, pf_lf, ps_k, ps_v, pw_k, pw_v) = _trunk(x_prompt, p_prompt, W, None)
    y_sample, (sf_k, sf_v, sf_lf, ss_k, ss_v, sw_k, sw_v) = _trunk(x_sample, p_sample, W, caches)
    return (y_prompt, y_sample, pf_k, pf_v, pf_lf, ps_k, ps_v, pw_k, pw_v,
            sf_k, sf_v, sf_lf, ss_k, ss_v, sw_k, sw_v)
```

```python
import functools

import jax
import jax.numpy as jnp
from jax import lax
from jax.experimental import pallas as pl
from jax.experimental.pallas import tpu as pltpu

F32 = jnp.float32
BF16 = jnp.bfloat16

HEAD_DIM = 64
CHUNK = 64
WINDOW = 128
ROPE_DIM = 16
ROPE_THETA = 500000.0
RMS_EPS = 1e-6
SCALE = HEAD_DIM ** -0.5

LANES = 128
V7X_VMEM_BYTES = 64 * 1024 * 1024
VMEM_LIMIT = V7X_VMEM_BYTES - 8 * 1024 * 1024


def _params(*sem):
    return pltpu.CompilerParams(dimension_semantics=sem, vmem_limit_bytes=VMEM_LIMIT)


def _row_tile(n, pref):
    t = pref
    while n % t:
        t //= 2
    return t


def _rms(x, g):
    ms = jnp.mean(x * x, axis=-1, keepdims=True)
    return x * lax.rsqrt(ms + RMS_EPS) * g


def _dot(a, b):
    return jnp.dot(a, b, preferred_element_type=F32)


def _dot_t(a, b):
    return lax.dot_general(a, b, (((1,), (1,)), ((), ())), preferred_element_type=F32)


def _softplus(z):
    return jnp.maximum(z, 0.0) + jnp.log(1.0 + jnp.exp(-jnp.abs(z)))


def _split_heads(q):
    first = lax.broadcasted_iota(jnp.int32, q.shape, 1) < HEAD_DIM
    zero = jnp.zeros_like(q)
    return jnp.where(first, q, zero), jnp.where(first, zero, q)


def _merge_heads(a, b):
    first = lax.broadcasted_iota(jnp.int32, a.shape, 1) < HEAD_DIM
    return jnp.where(first, a, b)


def _proj_ab_kernel(x_ref, g_ref, w_ref, wf_ref, bf_ref,
                    fk_ref, fv_ref, sk_ref, sv_ref, lf_ref,
                    fqb_ref, fkb_ref, fvb_ref, sqb_ref, skb_ref, svb_ref, *, width):
    xn = _rms(x_ref[...], g_ref[...]).astype(BF16)

    def col(c):
        return _dot(xn, w_ref[:, c * width:(c + 1) * width])

    fqb_ref[...] = (col(0) * SCALE).astype(BF16)
    y = col(1)
    fk_ref[...] = y
    fkb_ref[...] = y.astype(BF16)
    y = col(2)
    fv_ref[...] = y
    fvb_ref[...] = y.astype(BF16)
    sqb_ref[...] = (col(3) * SCALE).astype(BF16)
    y = col(4)
    sk_ref[...] = y
    skb_ref[...] = y.astype(BF16)
    y = col(5)
    sv_ref[...] = y
    svb_ref[...] = y.astype(BF16)
    u = _dot(xn, wf_ref[...]) + bf_ref[...]
    lf = -_softplus(-u)
    lf_ref[...] = lf[:, :lf_ref.shape[1]]


def _proj_ab(x, g, w_main, w_f, b_f, n_heads):
    n, d = x.shape
    width = n_heads * HEAD_DIM
    tm = _row_tile(n, 512)
    row = lambda i: (i, 0)
    const = lambda i: (0, 0)
    f32_out = jax.ShapeDtypeStruct((n, width), F32)
    bf_out = jax.ShapeDtypeStruct((n, width), BF16)
    blk = pl.BlockSpec((tm, width), row)
    return pl.pallas_call(
        functools.partial(_proj_ab_kernel, width=width),
        grid=(n // tm,),
        in_specs=[pl.BlockSpec((tm, d), row), pl.BlockSpec((1, d), const),
                  pl.BlockSpec(w_main.shape, const), pl.BlockSpec(w_f.shape, const),
                  pl.BlockSpec(b_f.shape, const)],
        out_specs=[blk, blk, blk, blk, pl.BlockSpec((tm, n_heads), row)] + [blk] * 6,
        out_shape=[f32_out] * 4 + [jax.ShapeDtypeStruct((n, n_heads), F32)] + [bf_out] * 6,
        compiler_params=_params("parallel"),
        name="proj_ab",
    )(x, g, w_main, w_f, b_f)


def _cumsum_kernel(x_ref, o_ref, *, n_chunks):
    r = lax.broadcasted_iota(jnp.int32, (LANES, LANES), 0)
    c = lax.broadcasted_iota(jnp.int32, (LANES, LANES), 1)
    upper = (r <= c).astype(BF16)

    def body(t, carry):
        off = pl.multiple_of(t * LANES, LANES)
        x = x_ref[:, pl.ds(off, LANES)]
        x1 = x.astype(BF16)
        r1 = x - x1.astype(F32)
        x2 = r1.astype(BF16)
        x3 = (r1 - x2.astype(F32)).astype(BF16)
        y = _dot(x1, upper) + _dot(x2, upper) + _dot(x3, upper) + carry
        o_ref[:, pl.ds(off, LANES)] = y
        return y[:, LANES - 1:LANES]

    lax.fori_loop(0, n_chunks, body, jnp.zeros((x_ref.shape[0], 1), F32))


def _cumsum_rows(x):
    r, l = x.shape
    tr = _row_tile(r, 16)
    return pl.pallas_call(
        functools.partial(_cumsum_kernel, n_chunks=l // LANES),
        grid=(r // tr,),
        in_specs=[pl.BlockSpec((tr, l), lambda i: (i, 0))],
        out_specs=pl.BlockSpec((tr, l), lambda i: (i, 0)),
        out_shape=jax.ShapeDtypeStruct((r, l), F32),
        compiler_params=_params("parallel"),
        name="cumsum_logf",
    )(x)


def _fox_update(state, qh, kj, vj, ck_row, valid):
    m, l, acc = state
    s = _dot_t(qh, kj) - ck_row
    if valid is not None:
        s = jnp.where(valid, s, -jnp.inf)
    m_new = jnp.maximum(m, jnp.max(s, axis=-1, keepdims=True))
    alpha = jnp.exp(m - m_new)
    p = jnp.exp(s - m_new)
    l = alpha * l + jnp.sum(p, axis=-1, keepdims=True)
    acc = alpha * acc + _dot(p.astype(BF16), vj)
    return m_new, l, acc


def _fox_init(tq):
    return (jnp.full((tq, 1), -jnp.inf, F32), jnp.zeros((tq, 1), F32), jnp.zeros((tq, LANES), F32))


def _sb_update(state, qh, kj, vj, tri, valid):
    carry, acc = state
    z = _dot_t(qh, kj)
    log_1m = -_softplus(z)
    if valid is not None:
        log_1m = jnp.where(valid, log_1m, 0.0)
    hi = log_1m.astype(BF16)
    lo = (log_1m - hi.astype(F32)).astype(BF16)
    between = _dot(hi, tri) + _dot(lo, tri)
    a = jnp.exp(z + log_1m + between + carry)
    if valid is not None:
        a = jnp.where(valid, a, 0.0)
    acc = acc + _dot(a.astype(BF16), vj)
    carry = carry + jnp.sum(log_1m, axis=-1, keepdims=True)
    return carry, acc


def _sb_init(tq):
    return (jnp.zeros((tq, 1), F32), jnp.zeros((tq, LANES), F32))


def _strict_lower(t):
    r = lax.broadcasted_iota(jnp.int32, (t, t), 0)
    c = lax.broadcasted_iota(jnp.int32, (t, t), 1)
    return (r > c).astype(BF16)


def _fox_prompt_kernel(q_ref, k_ref, v_ref, ck_ref, o_ref, *, tq, nk):
    i = pl.program_id(2)
    qs = _split_heads(q_ref[...])

    def step(j, states, valid):
        off = pl.multiple_of(j * tq, tq)
        kj = k_ref[pl.ds(off, tq), :]
        vj = v_ref[pl.ds(off, tq), :]
        return tuple(
            _fox_update(states[h], qs[h], kj, vj, ck_ref[pl.ds(h * nk + j, 1), :], valid)
            for h in range(2))

    states = lax.fori_loop(0, i, lambda j, st: step(j, st, None), (_fox_init(tq), _fox_init(tq)))
    row = lax.broadcasted_iota(jnp.int32, (tq, tq), 0)
    col = lax.broadcasted_iota(jnp.int32, (tq, tq), 1)
    states = step(i, states, col <= row)
    outs = [acc / l for (_, l, acc) in states]
    o_ref[...] = _merge_heads(outs[0], outs[1]).astype(o_ref.dtype)


def _sb_prompt_kernel(q_ref, k_ref, v_ref, o_ref, *, tq):
    i = pl.program_id(2)
    qs = _split_heads(q_ref[...])
    tri = _strict_lower(tq)

    def step(j, states, valid):
        off = pl.multiple_of(j * tq, tq)
        kj = k_ref[pl.ds(off, tq), :]
        vj = v_ref[pl.ds(off, tq), :]
        return tuple(_sb_update(states[h], qs[h], kj, vj, tri, valid) for h in range(2))

    row = lax.broadcasted_iota(jnp.int32, (tq, tq), 0)
    col = lax.broadcasted_iota(jnp.int32, (tq, tq), 1)
    states = step(i, (_sb_init(tq), _sb_init(tq)), col < row)
    states = lax.fori_loop(0, i, lambda jj, st: step(i - 1 - jj, st, None), states)
    o_ref[...] = _merge_heads(states[0][1], states[1][1]).astype(o_ref.dtype)


def _prompt_attention(q, k, v, ck, batch, seq, tq):
    n, width = q.shape
    pairs = width // LANES
    nq = seq // tq
    q_spec = pl.BlockSpec((tq, LANES), lambda b, p, i: (b * nq + i, p))
    kv_spec = pl.BlockSpec((seq, LANES), lambda b, p, i: (b, p))
    if ck is None:
        kernel = functools.partial(_sb_prompt_kernel, tq=tq)
        in_specs, args, name = [q_spec, kv_spec, kv_spec], (q, k, v), "sb_prompt"
    else:
        kernel = functools.partial(_fox_prompt_kernel, tq=tq, nk=nq)
        ck_spec = pl.BlockSpec((None, 2 * nq, tq), lambda b, p, i: (b * pairs + p, 0, 0))
        in_specs, args, name = [q_spec, kv_spec, kv_spec, ck_spec], (q, k, v, ck), "fox_prompt"
    return pl.pallas_call(
        kernel,
        grid=(batch, pairs, nq),
        in_specs=in_specs,
        out_specs=q_spec,
        out_shape=jax.ShapeDtypeStruct((n, width), BF16),
        compiler_params=_params("parallel", "parallel", "arbitrary"),
        name=name,
    )(*args)


def _ab_sample_kernel(fq_ref, fk_ref, fv_ref, cfk_ref, cfv_ref, ckc_ref, ckn_ref,
                      sq_ref, sk_ref, sv_ref, csk_ref, csv_ref, fo_ref, so_ref, *, t, tkc, nkc):
    row = lax.broadcasted_iota(jnp.int32, (t, t), 0)
    col = lax.broadcasted_iota(jnp.int32, (t, t), 1)

    qs = _split_heads(fq_ref[...])

    def fox_cache(j, states):
        off = pl.multiple_of(j * tkc, tkc)
        kj = cfk_ref[pl.ds(off, tkc), :].astype(BF16)
        vj = cfv_ref[pl.ds(off, tkc), :].astype(BF16)
        return tuple(
            _fox_update(states[h], qs[h], kj, vj, ckc_ref[pl.ds(h * nkc + j, 1), :], None)
            for h in range(2))

    states = lax.fori_loop(0, nkc, fox_cache, (_fox_init(t), _fox_init(t)))
    states = tuple(
        _fox_update(states[h], qs[h], fk_ref[...], fv_ref[...], ckn_ref[h:h + 1, :], col <= row)
        for h in range(2))
    outs = [acc / l for (_, l, acc) in states]
    fo_ref[...] = _merge_heads(outs[0], outs[1]).astype(fo_ref.dtype)

    qs2 = _split_heads(sq_ref[...])
    tri_new = _strict_lower(t)
    tri_cache = _strict_lower(tkc)
    sstates = tuple(
        _sb_update(_sb_init(t), qs2[h], sk_ref[...], sv_ref[...], tri_new, col < row)
        for h in range(2))

    def sb_cache(jj, st):
        off = pl.multiple_of((nkc - 1 - jj) * tkc, tkc)
        kj = csk_ref[pl.ds(off, tkc), :].astype(BF16)
        vj = csv_ref[pl.ds(off, tkc), :].astype(BF16)
        return tuple(_sb_update(st[h], qs2[h], kj, vj, tri_cache, None) for h in range(2))

    sstates = lax.fori_loop(0, nkc, sb_cache, sstates)
    so_ref[...] = _merge_heads(sstates[0][1], sstates[1][1]).astype(so_ref.dtype)


def _ab_sample_attention(fq, fk, fv, cfk, cfv, ck_cache, ck_new, sq, sk, sv, csk, csv,
                         batch, t, past, tkc):
    n, width = fq.shape
    pairs = width // LANES
    nkc = past // tkc
    new_spec = pl.BlockSpec((t, LANES), lambda b, p: (b, p))
    cache_spec = pl.BlockSpec((past, LANES), lambda b, p: (b, p))
    ckc_spec = pl.BlockSpec((None, 2 * nkc, tkc), lambda b, p: (b * pairs + p, 0, 0))
    ckn_spec = pl.BlockSpec((None, 2, t), lambda b, p: (b * pairs + p, 0, 0))
    out = jax.ShapeDtypeStruct((n, width), BF16)
    return pl.pallas_call(
        functools.partial(_ab_sample_kernel, t=t, tkc=tkc, nkc=nkc),
        grid=(batch, pairs),
        in_specs=[new_spec, new_spec, new_spec, cache_spec, cache_spec, ckc_spec, ckn_spec,
                  new_spec, new_spec, new_spec, cache_spec, cache_spec],
        out_specs=[new_spec, new_spec],
        out_shape=[out, out],
        compiler_params=_params("parallel", "parallel"),
        name="ab_sample",
    )(fq, fk, fv, cfk, cfv, ck_cache, ck_new, sq, sk, sv, csk, csv)


def _out_proj_kernel(*refs, n_in):
    h_ref = refs[0]
    o_ref = refs[-1]
    acc = h_ref[...]
    for a_ref, w_ref in zip(refs[1:1 + n_in], refs[1 + n_in:1 + 2 * n_in]):
        acc = acc + _dot(a_ref[...], w_ref[...])
    o_ref[...] = acc


def _out_proj(h, acts, weights):
    n, d = h.shape
    tm = _row_tile(n, 512)
    row = lambda i: (i, 0)
    const = lambda i: (0, 0)
    in_specs = [pl.BlockSpec((tm, d), row)]
    in_specs += [pl.BlockSpec((tm, a.shape[1]), row) for a in acts]
    in_specs += [pl.BlockSpec(w.shape, const) for w in weights]
    return pl.pallas_call(
        functools.partial(_out_proj_kernel, n_in=len(acts)),
        grid=(n // tm,),
        in_specs=in_specs,
        out_specs=pl.BlockSpec((tm, d), row),
        out_shape=jax.ShapeDtypeStruct((n, d), F32),
        compiler_params=_params("parallel"),
        name="out_proj",
    )(h, *acts, *weights)


def _ffn_kernel(x_ref, g_ref, wg_ref, wu_ref, wd_ref, o_ref, xn_ref, acc_ref):
    f = pl.program_id(1)

    @pl.when(f == 0)
    def _():
        xn_ref[...] = _rms(x_ref[...], g_ref[...]).astype(BF16)
        acc_ref[...] = jnp.zeros_like(acc_ref)

    xn = xn_ref[...]
    gate = _dot(xn, wg_ref[...])
    up = _dot(xn, wu_ref[...])
    act = (gate * jax.nn.sigmoid(gate) * up).astype(BF16)
    acc_ref[...] += _dot(act, wd_ref[...])

    @pl.when(f == pl.num_programs(1) - 1)
    def _():
        o_ref[...] = x_ref[...] + acc_ref[...]


def _ffn_tile(d_ff):
    for parts in (2, 4, 11, 22):
        if d_ff % (parts * LANES) == 0:
            return d_ff // parts
    return d_ff


def _ffn(h, g, w_gate, w_up, w_down):
    n, d = h.shape
    d_ff = w_gate.shape[1]
    tm = _row_tile(n, 512)
    tf = _ffn_tile(d_ff)
    return pl.pallas_call(
        _ffn_kernel,
        grid=(n // tm, d_ff // tf),
        in_specs=[pl.BlockSpec((tm, d), lambda i, f: (i, 0)), pl.BlockSpec((1, d), lambda i, f: (0, 0)),
                  pl.BlockSpec((d, tf), lambda i, f: (0, f)), pl.BlockSpec((d, tf), lambda i, f: (0, f)),
                  pl.BlockSpec((tf, d), lambda i, f: (f, 0))],
        out_specs=pl.BlockSpec((tm, d), lambda i, f: (i, 0)),
        out_shape=jax.ShapeDtypeStruct((n, d), F32),
        scratch_shapes=[pltpu.VMEM((tm, d), BF16), pltpu.VMEM((tm, d), F32)],
        compiler_params=_params("parallel", "arbitrary"),
        name="ffn",
    )(h, g, w_gate, w_up, w_down)


def _pe_kernel(x_ref, p_ref, g_ref, wg_ref, wp_ref, gf_ref, o_ref, *, final_norm):
    x = x_ref[...]
    gate = jax.nn.sigmoid(_dot(_rms(x, g_ref[...]).astype(BF16), wg_ref[...]))
    h = x + _dot(p_ref[...].astype(BF16), wp_ref[...]) * gate
    o_ref[...] = _rms(h, gf_ref[...]) if final_norm else h


def _per_layer_embed(h, p, g, w_gate, w_proj, g_final, final_norm):
    n, d = h.shape
    tm = _row_tile(n, 512)
    row = lambda i: (i, 0)
    const = lambda i: (0, 0)
    return pl.pallas_call(
        functools.partial(_pe_kernel, final_norm=final_norm),
        grid=(n // tm,),
        in_specs=[pl.BlockSpec((tm, d), row), pl.BlockSpec((tm, p.shape[1]), row),
                  pl.BlockSpec((1, d), const), pl.BlockSpec(w_gate.shape, const),
                  pl.BlockSpec(w_proj.shape, const), pl.BlockSpec((1, d), const)],
        out_specs=pl.BlockSpec((tm, d), row),
        out_shape=jax.ShapeDtypeStruct((n, d), F32),
        compiler_params=_params("parallel"),
        name="pe_final" if final_norm else "pe",
    )(h, p, g, w_gate, w_proj, g_final)


def _rope(y, cos, sin_lo, sin_hi):
    out = []
    for c in range(y.shape[1] // LANES):
        yc = y[:, c * LANES:(c + 1) * LANES]
        out.append(yc * cos + pltpu.roll(yc, 8, 1) * sin_hi + pltpu.roll(yc, LANES - 8, 1) * sin_lo)
    return out[0] if len(out) == 1 else jnp.concatenate(out, axis=1)


def _proj_c_kernel(x_ref, g_ref, w_ref, cos_ref, slo_ref, shi_ref,
                   k_ref, v_ref, qb_ref, kd_ref, vd_ref, *, q_w, kv_w):
    xn = _rms(x_ref[...], g_ref[...]).astype(BF16)
    tabs = (cos_ref[...], slo_ref[...], shi_ref[...])
    o = 0
    qb_ref[...] = (_rope(_dot(xn, w_ref[:, o:o + q_w]), *tabs) * SCALE).astype(BF16)
    o += q_w
    k_ref[...] = _rope(_dot(xn, w_ref[:, o:o + kv_w]), *tabs)
    o += kv_w
    v_ref[...] = _dot(xn, w_ref[:, o:o + kv_w])
    o += kv_w
    kd_ref[...] = _rope(_dot(xn, w_ref[:, o:o + 2 * kv_w]), *tabs).astype(BF16)
    o += 2 * kv_w
    vd_ref[...] = _dot(xn, w_ref[:, o:o + 2 * kv_w]).astype(BF16)


def _proj_c(x, g, w_all, tables, period, q_w, kv_w):
    n, d = x.shape
    tm = _row_tile(period, 512)
    nper = period // tm
    row = lambda i: (i, 0)
    const = lambda i: (0, 0)
    tab = pl.BlockSpec((tm, LANES), lambda i: (i % nper, 0))
    return pl.pallas_call(
        functools.partial(_proj_c_kernel, q_w=q_w, kv_w=kv_w),
        grid=(n // tm,),
        in_specs=[pl.BlockSpec((tm, d), row), pl.BlockSpec((1, d), const),
                  pl.BlockSpec(w_all.shape, const), tab, tab, tab],
        out_specs=[pl.BlockSpec((tm, kv_w), row), pl.BlockSpec((tm, kv_w), row),
                   pl.BlockSpec((tm, q_w), row), pl.BlockSpec((tm, 2 * kv_w), row),
                   pl.BlockSpec((tm, 2 * kv_w), row)],
        out_shape=[jax.ShapeDtypeStruct((n, kv_w), F32), jax.ShapeDtypeStruct((n, kv_w), F32),
                   jax.ShapeDtypeStruct((n, q_w), BF16), jax.ShapeDtypeStruct((n, 2 * kv_w), BF16),
                   jax.ShapeDtypeStruct((n, 2 * kv_w), BF16)],
        compiler_params=_params("parallel"),
        name="proj_c",
    )(x, g, w_all, *tables)


def _rope_tables(pos):
    half = ROPE_DIM // 2
    inv_freq = ROPE_THETA ** (-2.0 * jnp.arange(half, dtype=F32) / ROPE_DIM)
    ang = pos.astype(F32)[:, None] * inv_freq[None, :]
    cos, sin = jnp.cos(ang), jnp.sin(ang)
    n = pos.shape[0]
    rest = HEAD_DIM - ROPE_DIM
    cos_h = jnp.concatenate([cos, cos, jnp.ones((n, rest), F32)], axis=1)
    lo_h = jnp.concatenate([-sin, jnp.zeros((n, half + rest), F32)], axis=1)
    hi_h = jnp.concatenate([jnp.zeros((n, half), F32), sin, jnp.zeros((n, rest), F32)], axis=1)
    two = lambda a: jnp.concatenate([a, a], axis=1)
    return two(cos_h), two(lo_h), two(hi_h)


def _swa_kernel(sink_ref, q_ref, kprev_ref, kcur_ref, vprev_ref, vcur_ref, o_ref, kb_ref, vb_ref,
                *, tq, n_kv, group, first_pos_of_tile):
    band = WINDOW + CHUNK
    kb_ref[0:WINDOW, :] = kprev_ref[...]
    kb_ref[WINDOW:, :] = kcur_ref[...]
    vb_ref[0:WINDOW, :] = vprev_ref[...]
    vb_ref[WINDOW:, :] = vcur_ref[...]
    rows = group * CHUNK
    ridx = lax.broadcasted_iota(jnp.int32, (rows, 1), 0) // CHUNK
    col = lax.broadcasted_iota(jnp.int32, (rows, band), 1)
    tile_pos = None if first_pos_of_tile is None else first_pos_of_tile(pl.program_id(1))

    def chunk(cc, _):
        base = pl.multiple_of(cc * CHUNK, CHUNK)
        for kvh in range(n_kv):
            lhs = []
            for pr in range(group // 2):
                lane0 = (kvh * (group // 2) + pr) * LANES
                lhs.extend(_split_heads(q_ref[pl.ds(base, CHUNK), lane0:lane0 + LANES]))
            s = _dot_t(jnp.concatenate(lhs, axis=0), kb_ref[pl.ds(base, band), kvh * LANES:(kvh + 1) * LANES])
            if tile_pos is not None:
                s = jnp.where(tile_pos - WINDOW + cc * CHUNK + col >= 0, s, -jnp.inf)
            sink = jnp.zeros((rows, 1), F32)
            for gi in range(group):
                sink = jnp.where(ridx == gi, sink_ref[kvh * group + gi], sink)
            m = jnp.maximum(jnp.max(s, axis=-1, keepdims=True), sink)
            e = jnp.exp(s - m)
            den = jnp.sum(e, axis=-1, keepdims=True) + jnp.exp(sink - m)
            o = _dot(e.astype(BF16), vb_ref[pl.ds(base, band), kvh * LANES:(kvh + 1) * LANES]) / den
            for pr in range(group // 2):
                lane0 = (kvh * (group // 2) + pr) * LANES
                pair = _merge_heads(o[(2 * pr) * CHUNK:(2 * pr + 1) * CHUNK],
                                    o[(2 * pr + 1) * CHUNK:(2 * pr + 2) * CHUNK])
                o_ref[pl.ds(base, CHUNK), lane0:lane0 + LANES] = pair.astype(o_ref.dtype)
        return 0

    lax.fori_loop(0, tq // CHUNK, chunk, 0)


def _swa_prompt(sinks, q, kd, vd, batch, seq, n_kv):
    n, q_w = q.shape
    group = q_w // HEAD_DIM // n_kv
    tq = _row_tile(seq, 512)
    nt = seq // tq
    per_win = tq // WINDOW
    cur = lambda b, i: (b * nt + i, 0)
    prev = lambda b, i: (jnp.maximum(b * nt * per_win + i * per_win - 1, b * nt * per_win), 0)
    kv_w = kd.shape[1]
    return pl.pallas_call(
        functools.partial(_swa_kernel, tq=tq, n_kv=n_kv, group=group,
                          first_pos_of_tile=lambda i: i * tq),
        grid=(batch, nt),
        in_specs=[pl.BlockSpec(memory_space=pltpu.SMEM),
                  pl.BlockSpec((tq, q_w), cur),
                  pl.BlockSpec((WINDOW, kv_w), prev), pl.BlockSpec((tq, kv_w), cur),
                  pl.BlockSpec((WINDOW, kv_w), prev), pl.BlockSpec((tq, kv_w), cur)],
        out_specs=pl.BlockSpec((tq, q_w), cur),
        out_shape=jax.ShapeDtypeStruct((n, q_w), BF16),
        scratch_shapes=[pltpu.VMEM((WINDOW + tq, kv_w), BF16), pltpu.VMEM((WINDOW + tq, kv_w), BF16)],
        compiler_params=_params("parallel", "parallel"),
        name="swa_prompt",
    )(sinks, q, kd, kd, vd, vd)


def _swa_sample(sinks, q, kd_cache, kd_new, vd_cache, vd_new, batch, t, n_kv):
    n, q_w = q.shape
    group = q_w // HEAD_DIM // n_kv
    kv_w = kd_new.shape[1]
    blk = lambda b, i: (b, 0)
    return pl.pallas_call(
        functools.partial(_swa_kernel, tq=t, n_kv=n_kv, group=group, first_pos_of_tile=None),
        grid=(batch, 1),
        in_specs=[pl.BlockSpec(memory_space=pltpu.SMEM),
                  pl.BlockSpec((t, q_w), blk),
                  pl.BlockSpec((WINDOW, kv_w), blk), pl.BlockSpec((t, kv_w), blk),
                  pl.BlockSpec((WINDOW, kv_w), blk), pl.BlockSpec((t, kv_w), blk)],
        out_specs=pl.BlockSpec((t, q_w), blk),
        out_shape=jax.ShapeDtypeStruct((n, q_w), BF16),
        scratch_shapes=[pltpu.VMEM((WINDOW + t, kv_w), BF16), pltpu.VMEM((WINDOW + t, kv_w), BF16)],
        compiler_params=_params("parallel", "parallel"),
        name="swa_sample",
    )(sinks, q, kd_cache, kd_new, vd_cache, vd_new)


def _dup_heads(a, n_heads):
    lead = a.shape[:-1]
    a = a.reshape(*lead, n_heads, 1, HEAD_DIM)
    return jnp.broadcast_to(a, (*lead, n_heads, 2, HEAD_DIM)).reshape(*lead, n_heads * LANES)


def _prep_weights(W, n_fox, n_sb, n_kv):
    fox_w = n_fox * HEAD_DIM
    w_in = W["w_ab_in"][0]
    w_main = jnp.concatenate([w_in[:, :3 * fox_w], w_in[:, 3 * fox_w + n_fox:]], axis=1).astype(BF16)
    w_flog = jnp.pad(w_in[:, 3 * fox_w:3 * fox_w + n_fox], ((0, 0), (0, LANES - n_fox))).astype(BF16)
    b_f = jnp.pad(W["b_fox_f"][0][None, :], ((0, 0), (0, LANES - n_fox)))
    w_c = W["w_c_in"][0]
    kv_w = n_kv * HEAD_DIM
    q_w = w_c.shape[1] - 2 * kv_w
    w_k, w_v = w_c[:, q_w:q_w + kv_w], w_c[:, q_w + kv_w:]
    w_c_all = jnp.concatenate([w_c, _dup_heads(w_k, n_kv), _dup_heads(w_v, n_kv)], axis=1).astype(BF16)
    bf = lambda name, i: W[name][i].astype(BF16)
    out = dict(w_main=w_main, w_flog=w_flog, b_f=b_f, w_c_all=w_c_all, q_w=q_w, kv_w=kv_w,
               w_ab_out_f=W["w_ab_out"][0][:fox_w].astype(BF16),
               w_ab_out_s=W["w_ab_out"][0][fox_w:].astype(BF16),
               w_c_out=bf("w_c_out", 0), c_sinks=W["c_sinks"][0])
    for i in range(2):
        out[f"ffn{i}"] = (W["norm_ffn"][i][None, :], bf("w_ffn_gate", i), bf("w_ffn_up", i),
                          bf("w_ffn_down", i))
        out[f"pe{i}"] = (W["norm_pe"][i][None, :], bf("w_pe_gate", i), bf("w_pe_proj", i))
    out["norm_mix"] = [W["norm_mix"][i][None, :] for i in range(2)]
    out["norm_final"] = W["norm_final"][None, :]
    return out


def _pad_lanes(a):
    pad = (-a.shape[1]) % LANES
    return jnp.pad(a, ((0, 0), (0, pad))) if pad else a


def _layer_tail(h, p, P, i):
    h = _ffn(h, *P[f"ffn{i}"])
    return _per_layer_embed(h, p, *P[f"pe{i}"], P["norm_final"], final_norm=(i == 1))


def _trunk_prompt(x, p, P, n_fox, n_kv):
    batch, seq, d = x.shape
    n = batch * seq
    h = x.reshape(n, d)
    fk, fv, sk, sv, logf, fqb, fkb, fvb, sqb, skb, svb = _proj_ab(
        h, P["norm_mix"][0], P["w_main"], P["w_flog"], P["b_f"], n_fox)
    tq = _row_tile(seq, 256)
    nq = seq // tq
    lf_rows = jnp.swapaxes(logf.reshape(batch, seq, n_fox), 1, 2).reshape(batch * n_fox, seq)
    ck = _cumsum_rows(lf_rows).reshape(batch * n_fox // 2, 2 * nq, tq)
    fo = _prompt_attention(fqb, fkb, fvb, ck, batch, seq, tq)
    so = _prompt_attention(sqb, skb, svb, None, batch, seq, tq)
    h = _out_proj(h, [fo, so], [P["w_ab_out_f"], P["w_ab_out_s"]])
    h = _layer_tail(h, p[0].reshape(n, -1), P, 0)
    tables = _rope_tables(jnp.arange(seq))
    k, v, qb, kd, vd = _proj_c(h, P["norm_mix"][1], P["w_c_all"], tables, seq, P["q_w"], P["kv_w"])
    o = _swa_prompt(P["c_sinks"], qb, kd, vd, batch, seq, n_kv)
    h = _out_proj(h, [o], [P["w_c_out"]])
    y = _layer_tail(h, p[1].reshape(n, -1), P, 1)
    heads = lambda a, nh: a.reshape(1, batch, seq, nh, HEAD_DIM)
    kw = k.reshape(batch, seq, n_kv, HEAD_DIM)[None, :, seq - WINDOW:]
    vw = v.reshape(batch, seq, n_kv, HEAD_DIM)[None, :, seq - WINDOW:]
    return y.reshape(batch, seq, d), (heads(fk, n_fox), heads(fv, n_fox),
                                      logf.reshape(1, batch, seq, n_fox),
                                      heads(sk, n_fox), heads(sv, n_fox), kw, vw)


def _trunk_sample(x, p, P, caches, n_fox, n_kv):
    c_fk, c_fv, c_fl, c_sk, c_sv, c_wk, c_wv = caches
    batch, t, d = x.shape
    past = c_fk.shape[2]
    n = batch * t
    h = x.reshape(n, d)
    width = n_fox * HEAD_DIM
    fk, fv, sk, sv, logf, fqb, fkb, fvb, sqb, skb, svb = _proj_ab(
        h, P["norm_mix"][0], P["w_main"], P["w_flog"], P["b_f"], n_fox)
    lf_all = jnp.concatenate([c_fl[0], logf.reshape(batch, t, n_fox)], axis=1)
    lf_rows = _pad_lanes(jnp.swapaxes(lf_all, 1, 2).reshape(batch * n_fox, past + t))
    c_rows = _cumsum_rows(lf_rows)
    tkc = _row_tile(past, 256)
    ck_cache = c_rows[:, :past].reshape(batch * n_fox // 2, 2 * (past // tkc), tkc)
    ck_new = c_rows[:, past:past + t].reshape(batch * n_fox // 2, 2, t)
    flat = lambda c: c[0].reshape(batch * past, width)
    fo, so = _ab_sample_attention(fqb, fkb, fvb, flat(c_fk), flat(c_fv), ck_cache, ck_new,
                                  sqb, skb, svb, flat(c_sk), flat(c_sv), batch, t, past, tkc)
    h = _out_proj(h, [fo, so], [P["w_ab_out_f"], P["w_ab_out_s"]])
    h = _layer_tail(h, p[0].reshape(n, -1), P, 0)
    tm = _row_tile(n, 512)
    pos = past + jnp.arange(t)
    tables = _rope_tables(jnp.tile(pos, max(tm // t, 1)))
    k, v, qb, kd, vd = _proj_c(h, P["norm_mix"][1], P["w_c_all"], tables, tables[0].shape[0],
                               P["q_w"], P["kv_w"])
    kd_cache = _dup_heads(c_wk[0].reshape(batch * WINDOW, n_kv * HEAD_DIM), n_kv).astype(BF16)
    vd_cache = _dup_heads(c_wv[0].reshape(batch * WINDOW, n_kv * HEAD_DIM), n_kv).astype(BF16)
    o = _swa_sample(P["c_sinks"], qb, kd_cache, kd, vd_cache, vd, batch, t, n_kv)
    h = _out_proj(h, [o], [P["w_c_out"]])
    y = _layer_tail(h, p[1].reshape(n, -1), P, 1)
    heads = lambda a, nh: a.reshape(1, batch, t, nh, HEAD_DIM)
    roll = lambda c, new: jnp.concatenate([c[0], new.reshape(batch, t, n_kv, HEAD_DIM)], axis=1)[None, :, -WINDOW:]
    return y.reshape(batch, t, d), (heads(fk, n_fox), heads(fv, n_fox),
                                    logf.reshape(1, batch, t, n_fox),
                                    heads(sk, n_fox), heads(sv, n_fox), roll(c_wk, k), roll(c_wv, v))


def kernel(x_prompt, x_sample, cache_fox_k, cache_fox_v, cache_fox_logf, cache_sb_k, cache_sb_v, cache_swa_k, cache_swa_v, p_prompt, p_sample, norm_mix, w_ab_in, b_fox_f, w_ab_out, w_c_in, c_sinks, w_c_out, norm_ffn, w_ffn_gate, w_ffn_up, w_ffn_down, norm_pe, w_pe_gate, w_pe_proj, norm_final):
    W = dict(norm_mix=norm_mix, w_ab_in=w_ab_in, b_fox_f=b_fox_f, w_ab_out=w_ab_out, w_c_in=w_c_in,
             c_sinks=c_sinks, w_c_out=w_c_out, norm_ffn=norm_ffn, w_ffn_gate=w_ffn_gate,
             w_ffn_up=w_ffn_up, w_ffn_down=w_ffn_down, norm_pe=norm_pe, w_pe_gate=w_pe_gate,
             w_pe_proj=w_pe_proj, norm_final=norm_final)
    n_fox = cache_fox_k.shape[3]
    n_sb = cache_sb_k.shape[3]
    n_kv = cache_swa_k.shape[3]
    assert n_fox == n_sb and n_fox % 2 == 0 and n_kv % 1 == 0
    assert cache_swa_k.shape[2] == WINDOW and x_sample.shape[1] == CHUNK
    P = _prep_weights(W, n_fox, n_sb, n_kv)
    caches = (cache_fox_k, cache_fox_v, cache_fox_logf, cache_sb_k, cache_sb_v, cache_swa_k, cache_swa_v)
    y_p, st_p = _trunk_prompt(x_prompt, p_prompt, P, n_fox, n_kv)
    y_s, st_s = _trunk_sample(x_sample, p_sample, P, caches, n_fox, n_kv)
    return (y_p, y_s, *st_p, *st_s)
```

```python
import functools

import jax
import jax.numpy as jnp
from jax import lax
from jax.experimental import pallas as pl
from jax.experimental.pallas import tpu as pltpu

F32 = jnp.float32
BF16 = jnp.bfloat16

HEAD_DIM = 64
CHUNK = 64
WINDOW = 128
ROPE_DIM = 16
ROPE_THETA = 500000.0
RMS_EPS = 1e-6
SCALE = HEAD_DIM ** -0.5
SKIP_LOGIT = -110.0
REACH_SLACK = 1.01

LANES = 128
V7X_VMEM_BYTES = 64 * 1024 * 1024
VMEM_LIMIT = V7X_VMEM_BYTES - 8 * 1024 * 1024


def _params(*sem):
    return pltpu.CompilerParams(dimension_semantics=sem, vmem_limit_bytes=VMEM_LIMIT)


def _row_tile(n, pref):
    t = pref
    while n % t:
        t //= 2
    return t


def _rms(x, g):
    ms = jnp.mean(x * x, axis=-1, keepdims=True)
    return x * lax.rsqrt(ms + RMS_EPS) * g


def _dot(a, b):
    return jnp.dot(a, b, preferred_element_type=F32)


def _dot_t(a, b):
    return lax.dot_general(a, b, (((1,), (1,)), ((), ())), preferred_element_type=F32)


def _softplus(z):
    return jnp.maximum(z, 0.0) + jnp.log(1.0 + jnp.exp(-jnp.abs(z)))


def _split_heads(q):
    first = lax.broadcasted_iota(jnp.int32, q.shape, 1) < HEAD_DIM
    zero = jnp.zeros_like(q)
    return jnp.where(first, q, zero), jnp.where(first, zero, q)


def _merge_heads(a, b):
    first = lax.broadcasted_iota(jnp.int32, a.shape, 1) < HEAD_DIM
    return jnp.where(first, a, b)


def _proj_ab_kernel(x_ref, g_ref, w_ref, wf_ref, bf_ref,
                    fk_ref, fv_ref, sk_ref, sv_ref, lf_ref,
                    fqb_ref, fkb_ref, fvb_ref, sqb_ref, skb_ref, svb_ref, *, width):
    xn = _rms(x_ref[...], g_ref[...]).astype(BF16)

    def col(c):
        return _dot(xn, w_ref[:, c * width:(c + 1) * width])

    fqb_ref[...] = (col(0) * SCALE).astype(BF16)
    y = col(1)
    fk_ref[...] = y
    fkb_ref[...] = y.astype(BF16)
    y = col(2)
    fv_ref[...] = y
    fvb_ref[...] = y.astype(BF16)
    sqb_ref[...] = (col(3) * SCALE).astype(BF16)
    y = col(4)
    sk_ref[...] = y
    skb_ref[...] = y.astype(BF16)
    y = col(5)
    sv_ref[...] = y
    svb_ref[...] = y.astype(BF16)
    u = _dot(xn, wf_ref[...]) + bf_ref[...]
    lf = -_softplus(-u)
    lf_ref[...] = lf[:, :lf_ref.shape[1]]


def _proj_ab(x, g, w_main, w_f, b_f, n_heads):
    n, d = x.shape
    width = n_heads * HEAD_DIM
    tm = _row_tile(n, 512)
    row = lambda i: (i, 0)
    const = lambda i: (0, 0)
    f32_out = jax.ShapeDtypeStruct((n, width), F32)
    bf_out = jax.ShapeDtypeStruct((n, width), BF16)
    blk = pl.BlockSpec((tm, width), row)
    return pl.pallas_call(
        functools.partial(_proj_ab_kernel, width=width),
        grid=(n // tm,),
        in_specs=[pl.BlockSpec((tm, d), row), pl.BlockSpec((1, d), const),
                  pl.BlockSpec(w_main.shape, const), pl.BlockSpec(w_f.shape, const),
                  pl.BlockSpec(b_f.shape, const)],
        out_specs=[blk, blk, blk, blk, pl.BlockSpec((tm, n_heads), row)] + [blk] * 6,
        out_shape=[f32_out] * 4 + [jax.ShapeDtypeStruct((n, n_heads), F32)] + [bf_out] * 6,
        compiler_params=_params("parallel"),
        name="proj_ab",
    )(x, g, w_main, w_f, b_f)


def _cumsum_kernel(x_ref, o_ref, *, n_chunks):
    r = lax.broadcasted_iota(jnp.int32, (LANES, LANES), 0)
    c = lax.broadcasted_iota(jnp.int32, (LANES, LANES), 1)
    upper = (r <= c).astype(BF16)

    def body(t, carry):
        off = pl.multiple_of(t * LANES, LANES)
        x = x_ref[:, pl.ds(off, LANES)]
        x1 = x.astype(BF16)
        r1 = x - x1.astype(F32)
        x2 = r1.astype(BF16)
        x3 = (r1 - x2.astype(F32)).astype(BF16)
        y = _dot(x1, upper) + _dot(x2, upper) + _dot(x3, upper) + carry
        o_ref[:, pl.ds(off, LANES)] = y
        return y[:, LANES - 1:LANES]

    lax.fori_loop(0, n_chunks, body, jnp.zeros((x_ref.shape[0], 1), F32))


def _cumsum_rows(x):
    r, l = x.shape
    tr = _row_tile(r, 16)
    return pl.pallas_call(
        functools.partial(_cumsum_kernel, n_chunks=l // LANES),
        grid=(r // tr,),
        in_specs=[pl.BlockSpec((tr, l), lambda i: (i, 0))],
        out_specs=pl.BlockSpec((tr, l), lambda i: (i, 0)),
        out_shape=jax.ShapeDtypeStruct((r, l), F32),
        compiler_params=_params("parallel"),
        name="cumsum_logf",
    )(x)


def _fox_update(state, qh, kj, vj, ck_row, valid):
    m, l, acc = state
    s = _dot_t(qh, kj) - ck_row
    if valid is not None:
        s = jnp.where(valid, s, -jnp.inf)
    m_new = jnp.maximum(m, jnp.max(s, axis=-1, keepdims=True))
    alpha = jnp.exp(m - m_new)
    p = jnp.exp(s - m_new)
    l = alpha * l + jnp.sum(p, axis=-1, keepdims=True)
    acc = alpha * acc + _dot(p.astype(BF16), vj)
    return m_new, l, acc


def _fox_init(tq):
    return (jnp.full((tq, 1), -jnp.inf, F32), jnp.zeros((tq, 1), F32), jnp.zeros((tq, LANES), F32))


def _sb_update(state, qh, kj, vj, tri, valid):
    carry, acc = state
    z = _dot_t(qh, kj)
    log_1m = -_softplus(z)
    if valid is not None:
        log_1m = jnp.where(valid, log_1m, 0.0)
    hi = log_1m.astype(BF16)
    lo = (log_1m - hi.astype(F32)).astype(BF16)
    between = _dot(hi, tri) + _dot(lo, tri)
    a = jnp.exp(z + log_1m + between + carry)
    if valid is not None:
        a = jnp.where(valid, a, 0.0)
    acc = acc + _dot(a.astype(BF16), vj)
    carry = carry + jnp.sum(log_1m, axis=-1, keepdims=True)
    return carry, acc


def _sb_init(tq):
    return (jnp.zeros((tq, 1), F32), jnp.zeros((tq, LANES), F32))


def _strict_lower(t):
    r = lax.broadcasted_iota(jnp.int32, (t, t), 0)
    c = lax.broadcasted_iota(jnp.int32, (t, t), 1)
    return (r > c).astype(BF16)


def _newest_first(j0, states, step, may_matter):
    def cond(c):
        return jnp.logical_and(c[0] >= 0, c[1])

    def body(c):
        j, _, st = c
        st = step(j, st)
        return j - 1, may_matter(j - 1, st), st

    return lax.while_loop(cond, body, (j0, may_matter(j0, states), states))[2]


def _fox_reach(q_half, kmax):
    qf = q_half.astype(F32)
    return jnp.sqrt(jnp.sum(qf * qf, axis=-1, keepdims=True)) * kmax * REACH_SLACK


def _fox_may_matter(j, states, reach, ck_of_block):
    jc = jnp.maximum(j, 0)
    worst = None
    for h in range(2):
        ck_last = ck_of_block(h, jc)[:, -1:]
        gap = reach[h] - ck_last - states[h][0]
        worst = gap if worst is None else jnp.maximum(worst, gap)
    return jnp.max(worst) >= SKIP_LOGIT


def _sb_may_matter(states):
    return jnp.max(jnp.maximum(states[0][0], states[1][0])) >= SKIP_LOGIT


def _max_key_norms(load_rows, n_rows, chunk):
    first = lax.broadcasted_iota(jnp.int32, (chunk, LANES), 1) < HEAD_DIM

    def body(t, best):
        kf = load_rows(pl.multiple_of(t * chunk, chunk), chunk).astype(BF16).astype(F32)
        sq = kf * kf
        n0 = jnp.sum(jnp.where(first, sq, 0.0), axis=-1, keepdims=True)
        n1 = jnp.sum(jnp.where(first, 0.0, sq), axis=-1, keepdims=True)
        return jnp.maximum(best[0], n0), jnp.maximum(best[1], n1)

    zero = jnp.zeros((chunk, 1), F32)
    best = lax.fori_loop(0, n_rows // chunk, body, (zero, zero))
    return [jnp.sqrt(jnp.max(b, axis=0, keepdims=True)) for b in best]


def _fox_prompt_kernel(q_ref, k_ref, v_ref, ck_ref, o_ref, kmax_ref, *, tq, nk):
    i = pl.program_id(2)

    @pl.when(i == 0)
    def _():
        norms = _max_key_norms(lambda off, n: k_ref[pl.ds(off, n), :], nk * tq, tq)
        for h in range(2):
            kmax_ref[h] = jnp.broadcast_to(norms[h], kmax_ref.shape[1:])

    qs = _split_heads(q_ref[...])
    reach = [_fox_reach(qs[h], kmax_ref[h, 0:1, 0:1]) for h in range(2)]
    ck_of_block = lambda h, j: ck_ref[pl.ds(h * nk + j, 1), :]

    def step(j, states, valid=None):
        off = pl.multiple_of(j * tq, tq)
        kj = k_ref[pl.ds(off, tq), :]
        vj = v_ref[pl.ds(off, tq), :]
        return tuple(
            _fox_update(states[h], qs[h], kj, vj, ck_of_block(h, j), valid) for h in range(2))

    row = lax.broadcasted_iota(jnp.int32, (tq, tq), 0)
    col = lax.broadcasted_iota(jnp.int32, (tq, tq), 1)
    states = step(i, (_fox_init(tq), _fox_init(tq)), col <= row)
    states = _newest_first(i - 1, states, step,
                           lambda j, st: _fox_may_matter(j, st, reach, ck_of_block))
    outs = [acc / l for (_, l, acc) in states]
    o_ref[...] = _merge_heads(outs[0], outs[1]).astype(o_ref.dtype)


def _sb_prompt_kernel(q_ref, k_ref, v_ref, o_ref, *, tq):
    i = pl.program_id(2)
    qs = _split_heads(q_ref[...])
    tri = _strict_lower(tq)

    def step(j, states, valid=None):
        off = pl.multiple_of(j * tq, tq)
        kj = k_ref[pl.ds(off, tq), :]
        vj = v_ref[pl.ds(off, tq), :]
        return tuple(_sb_update(states[h], qs[h], kj, vj, tri, valid) for h in range(2))

    row = lax.broadcasted_iota(jnp.int32, (tq, tq), 0)
    col = lax.broadcasted_iota(jnp.int32, (tq, tq), 1)
    states = step(i, (_sb_init(tq), _sb_init(tq)), col < row)
    states = _newest_first(i - 1, states, step, lambda j, st: _sb_may_matter(st))
    o_ref[...] = _merge_heads(states[0][1], states[1][1]).astype(o_ref.dtype)


def _prompt_attention(q, k, v, ck, batch, seq, tq):
    n, width = q.shape
    pairs = width // LANES
    nq = seq // tq
    q_spec = pl.BlockSpec((tq, LANES), lambda b, p, i: (b * nq + i, p))
    kv_spec = pl.BlockSpec((seq, LANES), lambda b, p, i: (b, p))
    if ck is None:
        kernel = functools.partial(_sb_prompt_kernel, tq=tq)
        in_specs, args, scratch, name = [q_spec, kv_spec, kv_spec], (q, k, v), [], "sb_prompt"
    else:
        kernel = functools.partial(_fox_prompt_kernel, tq=tq, nk=nq)
        ck_spec = pl.BlockSpec((None, 2 * nq, tq), lambda b, p, i: (b * pairs + p, 0, 0))
        in_specs, args, name = [q_spec, kv_spec, kv_spec, ck_spec], (q, k, v, ck), "fox_prompt"
        scratch = [pltpu.VMEM((2, 8, LANES), F32)]
    return pl.pallas_call(
        kernel,
        grid=(batch, pairs, nq),
        in_specs=in_specs,
        out_specs=q_spec,
        out_shape=jax.ShapeDtypeStruct((n, width), BF16),
        scratch_shapes=scratch,
        compiler_params=_params("parallel", "parallel", "arbitrary"),
        name=name,
    )(*args)


def _ab_sample_kernel(fq_ref, fk_ref, fv_ref, cfk_ref, cfv_ref, ckc_ref, ckn_ref,
                      sq_ref, sk_ref, sv_ref, csk_ref, csv_ref, fo_ref, so_ref, *, t, tkc, nkc):
    row = lax.broadcasted_iota(jnp.int32, (t, t), 0)
    col = lax.broadcasted_iota(jnp.int32, (t, t), 1)

    qs = _split_heads(fq_ref[...])
    kmax = _max_key_norms(lambda off, n: cfk_ref[pl.ds(off, n), :], nkc * tkc, tkc)
    reach = [_fox_reach(qs[h], kmax[h]) for h in range(2)]
    ck_of_block = lambda h, j: ckc_ref[pl.ds(h * nkc + j, 1), :]

    def fox_cache(j, states):
        off = pl.multiple_of(j * tkc, tkc)
        kj = cfk_ref[pl.ds(off, tkc), :].astype(BF16)
        vj = cfv_ref[pl.ds(off, tkc), :].astype(BF16)
        return tuple(
            _fox_update(states[h], qs[h], kj, vj, ck_of_block(h, j), None) for h in range(2))

    states = tuple(
        _fox_update(_fox_init(t), qs[h], fk_ref[...], fv_ref[...], ckn_ref[h:h + 1, :], col <= row)
        for h in range(2))
    states = _newest_first(nkc - 1, states, fox_cache,
                           lambda j, st: _fox_may_matter(j, st, reach, ck_of_block))
    outs = [acc / l for (_, l, acc) in states]
    fo_ref[...] = _merge_heads(outs[0], outs[1]).astype(fo_ref.dtype)

    qs2 = _split_heads(sq_ref[...])
    tri_new = _strict_lower(t)
    tri_cache = _strict_lower(tkc)
    sstates = tuple(
        _sb_update(_sb_init(t), qs2[h], sk_ref[...], sv_ref[...], tri_new, col < row)
        for h in range(2))

    def sb_cache(j, st):
        off = pl.multiple_of(j * tkc, tkc)
        kj = csk_ref[pl.ds(off, tkc), :].astype(BF16)
        vj = csv_ref[pl.ds(off, tkc), :].astype(BF16)
        return tuple(_sb_update(st[h], qs2[h], kj, vj, tri_cache, None) for h in range(2))

    sstates = _newest_first(nkc - 1, sstates, sb_cache, lambda j, st: _sb_may_matter(st))
    so_ref[...] = _merge_heads(sstates[0][1], sstates[1][1]).astype(so_ref.dtype)


def _ab_sample_attention(fq, fk, fv, cfk, cfv, ck_cache, ck_new, sq, sk, sv, csk, csv,
                         batch, t, past, tkc):
    n, width = fq.shape
    pairs = width // LANES
    nkc = past // tkc
    new_spec = pl.BlockSpec((t, LANES), lambda b, p: (b, p))
    cache_spec = pl.BlockSpec((past, LANES), lambda b, p: (b, p))
    ckc_spec = pl.BlockSpec((None, 2 * nkc, tkc), lambda b, p: (b * pairs + p, 0, 0))
    ckn_spec = pl.BlockSpec((None, 2, t), lambda b, p: (b * pairs + p, 0, 0))
    out = jax.ShapeDtypeStruct((n, width), BF16)
    return pl.pallas_call(
        functools.partial(_ab_sample_kernel, t=t, tkc=tkc, nkc=nkc),
        grid=(batch, pairs),
        in_specs=[new_spec, new_spec, new_spec, cache_spec, cache_spec, ckc_spec, ckn_spec,
                  new_spec, new_spec, new_spec, cache_spec, cache_spec],
        out_specs=[new_spec, new_spec],
        out_shape=[out, out],
        compiler_params=_params("parallel", "parallel"),
        name="ab_sample",
    )(fq, fk, fv, cfk, cfv, ck_cache, ck_new, sq, sk, sv, csk, csv)


def _out_proj_kernel(*refs, n_in):
    h_ref = refs[0]
    o_ref = refs[-1]
    acc = h_ref[...]
    for a_ref, w_ref in zip(refs[1:1 + n_in], refs[1 + n_in:1 + 2 * n_in]):
        acc = acc + _dot(a_ref[...], w_ref[...])
    o_ref[...] = acc


def _out_proj(h, acts, weights):
    n, d = h.shape
    tm = _row_tile(n, 512)
    row = lambda i: (i, 0)
    const = lambda i: (0, 0)
    in_specs = [pl.BlockSpec((tm, d), row)]
    in_specs += [pl.BlockSpec((tm, a.shape[1]), row) for a in acts]
    in_specs += [pl.BlockSpec(w.shape, const) for w in weights]
    return pl.pallas_call(
        functools.partial(_out_proj_kernel, n_in=len(acts)),
        grid=(n // tm,),
        in_specs=in_specs,
        out_specs=pl.BlockSpec((tm, d), row),
        out_shape=jax.ShapeDtypeStruct((n, d), F32),
        compiler_params=_params("parallel"),
        name="out_proj",
    )(h, *acts, *weights)


def _ffn_kernel(x_ref, g_ref, wg_ref, wu_ref, wd_ref, o_ref, xn_ref, acc_ref):
    f = pl.program_id(1)

    @pl.when(f == 0)
    def _():
        xn_ref[...] = _rms(x_ref[...], g_ref[...]).astype(BF16)
        acc_ref[...] = jnp.zeros_like(acc_ref)

    xn = xn_ref[...]
    gate = _dot(xn, wg_ref[...])
    up = _dot(xn, wu_ref[...])
    act = (gate * jax.nn.sigmoid(gate) * up).astype(BF16)
    acc_ref[...] += _dot(act, wd_ref[...])

    @pl.when(f == pl.num_programs(1) - 1)
    def _():
        o_ref[...] = x_ref[...] + acc_ref[...]


def _ffn_tile(d_ff):
    for parts in (2, 4, 11, 22):
        if d_ff % (parts * LANES) == 0:
            return d_ff // parts
    return d_ff


def _ffn(h, g, w_gate, w_up, w_down):
    n, d = h.shape
    d_ff = w_gate.shape[1]
    tm = _row_tile(n, 512)
    tf = _ffn_tile(d_ff)
    return pl.pallas_call(
        _ffn_kernel,
        grid=(n // tm, d_ff // tf),
        in_specs=[pl.BlockSpec((tm, d), lambda i, f: (i, 0)), pl.BlockSpec((1, d), lambda i, f: (0, 0)),
                  pl.BlockSpec((d, tf), lambda i, f: (0, f)), pl.BlockSpec((d, tf), lambda i, f: (0, f)),
                  pl.BlockSpec((tf, d), lambda i, f: (f, 0))],
        out_specs=pl.BlockSpec((tm, d), lambda i, f: (i, 0)),
        out_shape=jax.ShapeDtypeStruct((n, d), F32),
        scratch_shapes=[pltpu.VMEM((tm, d), BF16), pltpu.VMEM((tm, d), F32)],
        compiler_params=_params("parallel", "arbitrary"),
        name="ffn",
    )(h, g, w_gate, w_up, w_down)


def _pe_kernel(x_ref, p_ref, g_ref, wg_ref, wp_ref, gf_ref, o_ref, *, final_norm):
    x = x_ref[...]
    gate = jax.nn.sigmoid(_dot(_rms(x, g_ref[...]).astype(BF16), wg_ref[...]))
    h = x + _dot(p_ref[...].astype(BF16), wp_ref[...]) * gate
    o_ref[...] = _rms(h, gf_ref[...]) if final_norm else h


def _per_layer_embed(h, p, g, w_gate, w_proj, g_final, final_norm):
    n, d = h.shape
    tm = _row_tile(n, 512)
    row = lambda i: (i, 0)
    const = lambda i: (0, 0)
    return pl.pallas_call(
        functools.partial(_pe_kernel, final_norm=final_norm),
        grid=(n // tm,),
        in_specs=[pl.BlockSpec((tm, d), row), pl.BlockSpec((tm, p.shape[1]), row),
                  pl.BlockSpec((1, d), const), pl.BlockSpec(w_gate.shape, const),
                  pl.BlockSpec(w_proj.shape, const), pl.BlockSpec((1, d), const)],
        out_specs=pl.BlockSpec((tm, d), row),
        out_shape=jax.ShapeDtypeStruct((n, d), F32),
        compiler_params=_params("parallel"),
        name="pe_final" if final_norm else "pe",
    )(h, p, g, w_gate, w_proj, g_final)


def _rope(y, cos, sin_lo, sin_hi):
    out = []
    for c in range(y.shape[1] // LANES):
        yc = y[:, c * LANES:(c + 1) * LANES]
        out.append(yc * cos + pltpu.roll(yc, 8, 1) * sin_hi + pltpu.roll(yc, LANES - 8, 1) * sin_lo)
    return out[0] if len(out) == 1 else jnp.concatenate(out, axis=1)


def _proj_c_kernel(x_ref, g_ref, w_ref, cos_ref, slo_ref, shi_ref,
                   k_ref, v_ref, qb_ref, kd_ref, vd_ref, *, q_w, kv_w):
    xn = _rms(x_ref[...], g_ref[...]).astype(BF16)
    tabs = (cos_ref[...], slo_ref[...], shi_ref[...])
    o = 0
    qb_ref[...] = (_rope(_dot(xn, w_ref[:, o:o + q_w]), *tabs) * SCALE).astype(BF16)
    o += q_w
    k_ref[...] = _rope(_dot(xn, w_ref[:, o:o + kv_w]), *tabs)
    o += kv_w
    v_ref[...] = _dot(xn, w_ref[:, o:o + kv_w])
    o += kv_w
    kd_ref[...] = _rope(_dot(xn, w_ref[:, o:o + 2 * kv_w]), *tabs).astype(BF16)
    o += 2 * kv_w
    vd_ref[...] = _dot(xn, w_ref[:, o:o + 2 * kv_w]).astype(BF16)


def _proj_c(x, g, w_all, tables, period, q_w, kv_w):
    n, d = x.shape
    tm = _row_tile(period, 512)
    nper = period // tm
    row = lambda i: (i, 0)
    const = lambda i: (0, 0)
    tab = pl.BlockSpec((tm, LANES), lambda i: (i % nper, 0))
    return pl.pallas_call(
        functools.partial(_proj_c_kernel, q_w=q_w, kv_w=kv_w),
        grid=(n // tm,),
        in_specs=[pl.BlockSpec((tm, d), row), pl.BlockSpec((1, d), const),
                  pl.BlockSpec(w_all.shape, const), tab, tab, tab],
        out_specs=[pl.BlockSpec((tm, kv_w), row), pl.BlockSpec((tm, kv_w), row),
                   pl.BlockSpec((tm, q_w), row), pl.BlockSpec((tm, 2 * kv_w), row),
                   pl.BlockSpec((tm, 2 * kv_w), row)],
        out_shape=[jax.ShapeDtypeStruct((n, kv_w), F32), jax.ShapeDtypeStruct((n, kv_w), F32),
                   jax.ShapeDtypeStruct((n, q_w), BF16), jax.ShapeDtypeStruct((n, 2 * kv_w), BF16),
                   jax.ShapeDtypeStruct((n, 2 * kv_w), BF16)],
        compiler_params=_params("parallel"),
        name="proj_c",
    )(x, g, w_all, *tables)


def _rope_tables(pos):
    half = ROPE_DIM // 2
    inv_freq = ROPE_THETA ** (-2.0 * jnp.arange(half, dtype=F32) / ROPE_DIM)
    ang = pos.astype(F32)[:, None] * inv_freq[None, :]
    cos, sin = jnp.cos(ang), jnp.sin(ang)
    n = pos.shape[0]
    rest = HEAD_DIM - ROPE_DIM
    cos_h = jnp.concatenate([cos, cos, jnp.ones((n, rest), F32)], axis=1)
    lo_h = jnp.concatenate([-sin, jnp.zeros((n, half + rest), F32)], axis=1)
    hi_h = jnp.concatenate([jnp.zeros((n, half), F32), sin, jnp.zeros((n, rest), F32)], axis=1)
    two = lambda a: jnp.concatenate([a, a], axis=1)
    return two(cos_h), two(lo_h), two(hi_h)


def _swa_kernel(sink_ref, q_ref, kprev_ref, kcur_ref, vprev_ref, vcur_ref, o_ref, kb_ref, vb_ref,
                *, tq, n_kv, group, first_pos_of_tile):
    band = WINDOW + CHUNK
    kb_ref[0:WINDOW, :] = kprev_ref[...]
    kb_ref[WINDOW:, :] = kcur_ref[...]
    vb_ref[0:WINDOW, :] = vprev_ref[...]
    vb_ref[WINDOW:, :] = vcur_ref[...]
    rows = group * CHUNK
    ridx = lax.broadcasted_iota(jnp.int32, (rows, 1), 0) // CHUNK
    col = lax.broadcasted_iota(jnp.int32, (rows, band), 1)
    tile_pos = None if first_pos_of_tile is None else first_pos_of_tile(pl.program_id(1))

    def chunk(cc, _):
        base = pl.multiple_of(cc * CHUNK, CHUNK)
        for kvh in range(n_kv):
            lhs = []
            for pr in range(group // 2):
                lane0 = (kvh * (group // 2) + pr) * LANES
                lhs.extend(_split_heads(q_ref[pl.ds(base, CHUNK), lane0:lane0 + LANES]))
            s = _dot_t(jnp.concatenate(lhs, axis=0), kb_ref[pl.ds(base, band), kvh * LANES:(kvh + 1) * LANES])
            if tile_pos is not None:
                s = jnp.where(tile_pos - WINDOW + cc * CHUNK + col >= 0, s, -jnp.inf)
            sink = jnp.zeros((rows, 1), F32)
            for gi in range(group):
                sink = jnp.where(ridx == gi, sink_ref[kvh * group + gi], sink)
            m = jnp.maximum(jnp.max(s, axis=-1, keepdims=True), sink)
            e = jnp.exp(s - m)
            den = jnp.sum(e, axis=-1, keepdims=True) + jnp.exp(sink - m)
            o = _dot(e.astype(BF16), vb_ref[pl.ds(base, band), kvh * LANES:(kvh + 1) * LANES]) / den
            for pr in range(group // 2):
                lane0 = (kvh * (group // 2) + pr) * LANES
                pair = _merge_heads(o[(2 * pr) * CHUNK:(2 * pr + 1) * CHUNK],
                                    o[(2 * pr + 1) * CHUNK:(2 * pr + 2) * CHUNK])
                o_ref[pl.ds(base, CHUNK), lane0:lane0 + LANES] = pair.astype(o_ref.dtype)
        return 0

    lax.fori_loop(0, tq // CHUNK, chunk, 0)


def _swa_prompt(sinks, q, kd, vd, batch, seq, n_kv):
    n, q_w = q.shape
    group = q_w // HEAD_DIM // n_kv
    tq = _row_tile(seq, 512)
    nt = seq // tq
    per_win = tq // WINDOW
    cur = lambda b, i: (b * nt + i, 0)
    prev = lambda b, i: (jnp.maximum(b * nt * per_win + i * per_win - 1, b * nt * per_win), 0)
    kv_w = kd.shape[1]
    return pl.pallas_call(
        functools.partial(_swa_kernel, tq=tq, n_kv=n_kv, group=group,
                          first_pos_of_tile=lambda i: i * tq),
        grid=(batch, nt),
        in_specs=[pl.BlockSpec(memory_space=pltpu.SMEM),
                  pl.BlockSpec((tq, q_w), cur),
                  pl.BlockSpec((WINDOW, kv_w), prev), pl.BlockSpec((tq, kv_w), cur),
                  pl.BlockSpec((WINDOW, kv_w), prev), pl.BlockSpec((tq, kv_w), cur)],
        out_specs=pl.BlockSpec((tq, q_w), cur),
        out_shape=jax.ShapeDtypeStruct((n, q_w), BF16),
        scratch_shapes=[pltpu.VMEM((WINDOW + tq, kv_w), BF16), pltpu.VMEM((WINDOW + tq, kv_w), BF16)],
        compiler_params=_params("parallel", "parallel"),
        name="swa_prompt",
    )(sinks, q, kd, kd, vd, vd)


def _swa_sample(sinks, q, kd_cache, kd_new, vd_cache, vd_new, batch, t, n_kv):
    n, q_w = q.shape
    group = q_w // HEAD_DIM // n_kv
    kv_w = kd_new.shape[1]
    blk = lambda b, i: (b, 0)
    return pl.pallas_call(
        functools.partial(_swa_kernel, tq=t, n_kv=n_kv, group=group, first_pos_of_tile=None),
        grid=(batch, 1),
        in_specs=[pl.BlockSpec(memory_space=pltpu.SMEM),
                  pl.BlockSpec((t, q_w), blk),
                  pl.BlockSpec((WINDOW, kv_w), blk), pl.BlockSpec((t, kv_w), blk),
                  pl.BlockSpec((WINDOW, kv_w), blk), pl.BlockSpec((t, kv_w), blk)],
        out_specs=pl.BlockSpec((t, q_w), blk),
        out_shape=jax.ShapeDtypeStruct((n, q_w), BF16),
        scratch_shapes=[pltpu.VMEM((WINDOW + t, kv_w), BF16), pltpu.VMEM((WINDOW + t, kv_w), BF16)],
        compiler_params=_params("parallel", "parallel"),
        name="swa_sample",
    )(sinks, q, kd_cache, kd_new, vd_cache, vd_new)


def _dup_heads(a, n_heads):
    lead = a.shape[:-1]
    a = a.reshape(*lead, n_heads, 1, HEAD_DIM)
    return jnp.broadcast_to(a, (*lead, n_heads, 2, HEAD_DIM)).reshape(*lead, n_heads * LANES)


def _prep_weights(W, n_fox, n_sb, n_kv):
    fox_w = n_fox * HEAD_DIM
    w_in = W["w_ab_in"][0]
    w_main = jnp.concatenate([w_in[:, :3 * fox_w], w_in[:, 3 * fox_w + n_fox:]], axis=1).astype(BF16)
    w_flog = jnp.pad(w_in[:, 3 * fox_w:3 * fox_w + n_fox], ((0, 0), (0, LANES - n_fox))).astype(BF16)
    b_f = jnp.pad(W["b_fox_f"][0][None, :], ((0, 0), (0, LANES - n_fox)))
    w_c = W["w_c_in"][0]
    kv_w = n_kv * HEAD_DIM
    q_w = w_c.shape[1] - 2 * kv_w
    w_k, w_v = w_c[:, q_w:q_w + kv_w], w_c[:, q_w + kv_w:]
    w_c_all = jnp.concatenate([w_c, _dup_heads(w_k, n_kv), _dup_heads(w_v, n_kv)], axis=1).astype(BF16)
    bf = lambda name, i: W[name][i].astype(BF16)
    out = dict(w_main=w_main, w_flog=w_flog, b_f=b_f, w_c_all=w_c_all, q_w=q_w, kv_w=kv_w,
               w_ab_out_f=W["w_ab_out"][0][:fox_w].astype(BF16),
               w_ab_out_s=W["w_ab_out"][0][fox_w:].astype(BF16),
               w_c_out=bf("w_c_out", 0), c_sinks=W["c_sinks"][0])
    for i in range(2):
        out[f"ffn{i}"] = (W["norm_ffn"][i][None, :], bf("w_ffn_gate", i), bf("w_ffn_up", i),
                          bf("w_ffn_down", i))
        out[f"pe{i}"] = (W["norm_pe"][i][None, :], bf("w_pe_gate", i), bf("w_pe_proj", i))
    out["norm_mix"] = [W["norm_mix"][i][None, :] for i in range(2)]
    out["norm_final"] = W["norm_final"][None, :]
    return out


def _pad_lanes(a):
    pad = (-a.shape[1]) % LANES
    return jnp.pad(a, ((0, 0), (0, pad))) if pad else a


def _layer_tail(h, p, P, i):
    h = _ffn(h, *P[f"ffn{i}"])
    return _per_layer_embed(h, p, *P[f"pe{i}"], P["norm_final"], final_norm=(i == 1))


def _trunk_prompt(x, p, P, n_fox, n_kv):
    batch, seq, d = x.shape
    n = batch * seq
    h = x.reshape(n, d)
    fk, fv, sk, sv, logf, fqb, fkb, fvb, sqb, skb, svb = _proj_ab(
        h, P["norm_mix"][0], P["w_main"], P["w_flog"], P["b_f"], n_fox)
    tq = _row_tile(seq, 256)
    nq = seq // tq
    lf_rows = jnp.swapaxes(logf.reshape(batch, seq, n_fox), 1, 2).reshape(batch * n_fox, seq)
    ck = _cumsum_rows(lf_rows).reshape(batch * n_fox // 2, 2 * nq, tq)
    fo = _prompt_attention(fqb, fkb, fvb, ck, batch, seq, tq)
    so = _prompt_attention(sqb, skb, svb, None, batch, seq, tq)
    h = _out_proj(h, [fo, so], [P["w_ab_out_f"], P["w_ab_out_s"]])
    h = _layer_tail(h, p[0].reshape(n, -1), P, 0)
    tables = _rope_tables(jnp.arange(seq))
    k, v, qb, kd, vd = _proj_c(h, P["norm_mix"][1], P["w_c_all"], tables, seq, P["q_w"], P["kv_w"])
    o = _swa_prompt(P["c_sinks"], qb, kd, vd, batch, seq, n_kv)
    h = _out_proj(h, [o], [P["w_c_out"]])
    y = _layer_tail(h, p[1].reshape(n, -1), P, 1)
    heads = lambda a, nh: a.reshape(1, batch, seq, nh, HEAD_DIM)
    kw = k.reshape(batch, seq, n_kv, HEAD_DIM)[None, :, seq - WINDOW:]
    vw = v.reshape(batch, seq, n_kv, HEAD_DIM)[None, :, seq - WINDOW:]
    return y.reshape(batch, seq, d), (heads(fk, n_fox), heads(fv, n_fox),
                                      logf.reshape(1, batch, seq, n_fox),
                                      heads(sk, n_fox), heads(sv, n_fox), kw, vw)


def _trunk_sample(x, p, P, caches, n_fox, n_kv):
    c_fk, c_fv, c_fl, c_sk, c_sv, c_wk, c_wv = caches
    batch, t, d = x.shape
    past = c_fk.shape[2]
    n = batch * t
    h = x.reshape(n, d)
    width = n_fox * HEAD_DIM
    fk, fv, sk, sv, logf, fqb, fkb, fvb, sqb, skb, svb = _proj_ab(
        h, P["norm_mix"][0], P["w_main"], P["w_flog"], P["b_f"], n_fox)
    lf_all = jnp.concatenate([c_fl[0], logf.reshape(batch, t, n_fox)], axis=1)
    lf_rows = _pad_lanes(jnp.swapaxes(lf_all, 1, 2).reshape(batch * n_fox, past + t))
    c_rows = _cumsum_rows(lf_rows)
    tkc = _row_tile(past, 256)
    ck_cache = c_rows[:, :past].reshape(batch * n_fox // 2, 2 * (past // tkc), tkc)
    ck_new = c_rows[:, past:past + t].reshape(batch * n_fox // 2, 2, t)
    flat = lambda c: c[0].reshape(batch * past, width)
    fo, so = _ab_sample_attention(fqb, fkb, fvb, flat(c_fk), flat(c_fv), ck_cache, ck_new,
                                  sqb, skb, svb, flat(c_sk), flat(c_sv), batch, t, past, tkc)
    h = _out_proj(h, [fo, so], [P["w_ab_out_f"], P["w_ab_out_s"]])
    h = _layer_tail(h, p[0].reshape(n, -1), P, 0)
    tm = _row_tile(n, 512)
    pos = past + jnp.arange(t)
    tables = _rope_tables(jnp.tile(pos, max(tm // t, 1)))
    k, v, qb, kd, vd = _proj_c(h, P["norm_mix"][1], P["w_c_all"], tables, tables[0].shape[0],
                               P["q_w"], P["kv_w"])
    kd_cache = _dup_heads(c_wk[0].reshape(batch * WINDOW, n_kv * HEAD_DIM), n_kv).astype(BF16)
    vd_cache = _dup_heads(c_wv[0].reshape(batch * WINDOW, n_kv * HEAD_DIM), n_kv).astype(BF16)
    o = _swa_sample(P["c_sinks"], qb, kd_cache, kd, vd_cache, vd, batch, t, n_kv)
    h = _out_proj(h, [o], [P["w_c_out"]])
    y = _layer_tail(h, p[1].reshape(n, -1), P, 1)
    heads = lambda a, nh: a.reshape(1, batch, t, nh, HEAD_DIM)
    roll = lambda c, new: jnp.concatenate([c[0], new.reshape(batch, t, n_kv, HEAD_DIM)], axis=1)[None, :, -WINDOW:]
    return y.reshape(batch, t, d), (heads(fk, n_fox), heads(fv, n_fox),
                                    logf.reshape(1, batch, t, n_fox),
                                    heads(sk, n_fox), heads(sv, n_fox), roll(c_wk, k), roll(c_wv, v))


def kernel(x_prompt, x_sample, cache_fox_k, cache_fox_v, cache_fox_logf, cache_sb_k, cache_sb_v, cache_swa_k, cache_swa_v, p_prompt, p_sample, norm_mix, w_ab_in, b_fox_f, w_ab_out, w_c_in, c_sinks, w_c_out, norm_ffn, w_ffn_gate, w_ffn_up, w_ffn_down, norm_pe, w_pe_gate, w_pe_proj, norm_final):
    W = dict(norm_mix=norm_mix, w_ab_in=w_ab_in, b_fox_f=b_fox_f, w_ab_out=w_ab_out, w_c_in=w_c_in,
             c_sinks=c_sinks, w_c_out=w_c_out, norm_ffn=norm_ffn, w_ffn_gate=w_ffn_gate,
             w_ffn_up=w_ffn_up, w_ffn_down=w_ffn_down, norm_pe=norm_pe, w_pe_gate=w_pe_gate,
             w_pe_proj=w_pe_proj, norm_final=norm_final)
    n_fox = cache_fox_k.shape[3]
    n_sb = cache_sb_k.shape[3]
    n_kv = cache_swa_k.shape[3]
    assert n_fox == n_sb and n_fox % 2 == 0 and n_kv % 1 == 0
    assert cache_swa_k.shape[2] == WINDOW and x_sample.shape[1] == CHUNK
    P = _prep_weights(W, n_fox, n_sb, n_kv)
    caches = (cache_fox_k, cache_fox_v, cache_fox_logf, cache_sb_k, cache_sb_v, cache_swa_k, cache_swa_v)
    y_p, st_p = _trunk_prompt(x_prompt, p_prompt, P, n_fox, n_kv)
    y_s, st_s = _trunk_sample(x_sample, p_sample, P, caches, n_fox, n_kv)
    return (y_p, y_s, *st_p, *st_s)
```

```python
import functools

import jax
import jax.numpy as jnp
from jax import lax
from jax.experimental import pallas as pl
from jax.experimental.pallas import tpu as pltpu

F32 = jnp.float32
BF16 = jnp.bfloat16

HEAD_DIM = 64
CHUNK = 64
WINDOW = 128
ROPE_DIM = 16
ROPE_THETA = 500000.0
RMS_EPS = 1e-6
SCALE = HEAD_DIM ** -0.5
SKIP_LOGIT = -110.0
REACH_SLACK = 1.01
LOG2_E = 1.4426950408889634
LN_2 = 0.6931471805599453

LANES = 128
V7X_VMEM_BYTES = 64 * 1024 * 1024
VMEM_LIMIT = V7X_VMEM_BYTES - 8 * 1024 * 1024


def _params(*sem):
    return pltpu.CompilerParams(dimension_semantics=sem, vmem_limit_bytes=VMEM_LIMIT)


def _row_tile(n, pref):
    t = pref
    while n % t:
        t //= 2
    return t


def _rms(x, g):
    ms = jnp.mean(x * x, axis=-1, keepdims=True)
    return x * lax.rsqrt(ms + RMS_EPS) * g


def _dot(a, b):
    return jnp.dot(a, b, preferred_element_type=F32)


def _dot_t(a, b):
    return lax.dot_general(a, b, (((1,), (1,)), ((), ())), preferred_element_type=F32)


def _softplus(z):
    return jnp.maximum(z, 0.0) + jnp.log(1.0 + jnp.exp(-jnp.abs(z)))


def _split_heads(q):
    first = lax.broadcasted_iota(jnp.int32, q.shape, 1) < HEAD_DIM
    zero = jnp.zeros_like(q)
    return jnp.where(first, q, zero), jnp.where(first, zero, q)


def _merge_heads(a, b):
    first = lax.broadcasted_iota(jnp.int32, a.shape, 1) < HEAD_DIM
    return jnp.where(first, a, b)


def _proj_ab_kernel(x_ref, g_ref, w_ref, wf_ref, bf_ref,
                    fk_ref, fv_ref, sk_ref, sv_ref, lf_ref,
                    fqb_ref, fkb_ref, fvb_ref, sqb_ref, skb_ref, svb_ref, *, width):
    xn = _rms(x_ref[...], g_ref[...]).astype(BF16)

    def col(c):
        return _dot(xn, w_ref[:, c * width:(c + 1) * width])

    fqb_ref[...] = (col(0) * SCALE).astype(BF16)
    y = col(1)
    fk_ref[...] = y
    fkb_ref[...] = y.astype(BF16)
    y = col(2)
    fv_ref[...] = y
    fvb_ref[...] = y.astype(BF16)
    sqb_ref[...] = (col(3) * SCALE).astype(BF16)
    y = col(4)
    sk_ref[...] = y
    skb_ref[...] = y.astype(BF16)
    y = col(5)
    sv_ref[...] = y
    svb_ref[...] = y.astype(BF16)
    u = _dot(xn, wf_ref[...]) + bf_ref[...]
    lf = -_softplus(-u)
    lf_ref[...] = lf[:, :lf_ref.shape[1]]


def _proj_ab(x, g, w_main, w_f, b_f, n_heads):
    n, d = x.shape
    width = n_heads * HEAD_DIM
    tm = _row_tile(n, 512)
    row = lambda i: (i, 0)
    const = lambda i: (0, 0)
    f32_out = jax.ShapeDtypeStruct((n, width), F32)
    bf_out = jax.ShapeDtypeStruct((n, width), BF16)
    blk = pl.BlockSpec((tm, width), row)
    return pl.pallas_call(
        functools.partial(_proj_ab_kernel, width=width),
        grid=(n // tm,),
        in_specs=[pl.BlockSpec((tm, d), row), pl.BlockSpec((1, d), const),
                  pl.BlockSpec(w_main.shape, const), pl.BlockSpec(w_f.shape, const),
                  pl.BlockSpec(b_f.shape, const)],
        out_specs=[blk, blk, blk, blk, pl.BlockSpec((tm, n_heads), row)] + [blk] * 6,
        out_shape=[f32_out] * 4 + [jax.ShapeDtypeStruct((n, n_heads), F32)] + [bf_out] * 6,
        compiler_params=_params("parallel"),
        name="proj_ab",
    )(x, g, w_main, w_f, b_f)


def _cumsum_kernel(x_ref, o_ref, *, n_chunks):
    r = lax.broadcasted_iota(jnp.int32, (LANES, LANES), 0)
    c = lax.broadcasted_iota(jnp.int32, (LANES, LANES), 1)
    upper = (r <= c).astype(BF16)

    def body(t, carry):
        off = pl.multiple_of(t * LANES, LANES)
        x = x_ref[:, pl.ds(off, LANES)]
        x1 = x.astype(BF16)
        r1 = x - x1.astype(F32)
        x2 = r1.astype(BF16)
        x3 = (r1 - x2.astype(F32)).astype(BF16)
        y = _dot(x1, upper) + _dot(x2, upper) + _dot(x3, upper) + carry
        o_ref[:, pl.ds(off, LANES)] = y
        return y[:, LANES - 1:LANES]

    lax.fori_loop(0, n_chunks, body, jnp.zeros((x_ref.shape[0], 1), F32))


def _cumsum_rows(x):
    r, l = x.shape
    tr = _row_tile(r, 16)
    return pl.pallas_call(
        functools.partial(_cumsum_kernel, n_chunks=l // LANES),
        grid=(r // tr,),
        in_specs=[pl.BlockSpec((tr, l), lambda i: (i, 0))],
        out_specs=pl.BlockSpec((tr, l), lambda i: (i, 0)),
        out_shape=jax.ShapeDtypeStruct((r, l), F32),
        compiler_params=_params("parallel"),
        name="cumsum_logf",
    )(x)


def _fox_update(state, qh, kj, vj, ck_row, valid):
    m, l, acc = state
    s = _dot_t(qh, kj) - ck_row
    if valid is not None:
        s = jnp.where(valid, s, -jnp.inf)
    m_new = jnp.maximum(m, jnp.max(s, axis=-1, keepdims=True))
    alpha = jnp.exp(m - m_new)
    p = jnp.exp(s - m_new)
    l = alpha * l + jnp.sum(p, axis=-1, keepdims=True)
    acc = alpha * acc + _dot(p.astype(BF16), vj)
    return m_new, l, acc


def _fox_init(tq):
    return (jnp.full((tq, 1), -jnp.inf, F32), jnp.zeros((tq, 1), F32), jnp.zeros((tq, LANES), F32))


def _sb_update(state, qh, kj, vj, tri, valid):
    carry, acc = state
    z = _dot_t(qh, kj)
    log_1m = -_softplus(z)
    if valid is not None:
        log_1m = jnp.where(valid, log_1m, 0.0)
    hi = log_1m.astype(BF16)
    lo = (log_1m - hi.astype(F32)).astype(BF16)
    between = _dot(hi, tri) + _dot(lo, tri)
    a = jnp.exp(z + log_1m + between + carry)
    if valid is not None:
        a = jnp.where(valid, a, 0.0)
    acc = acc + _dot(a.astype(BF16), vj)
    carry = carry + jnp.sum(log_1m, axis=-1, keepdims=True)
    return carry, acc


def _sb_init(tq):
    return (jnp.zeros((tq, 1), F32), jnp.zeros((tq, LANES), F32))


def _strict_lower(t):
    r = lax.broadcasted_iota(jnp.int32, (t, t), 0)
    c = lax.broadcasted_iota(jnp.int32, (t, t), 1)
    return (r > c).astype(BF16)


def _newest_first(j0, states, step, may_matter):
    def cond(c):
        return jnp.logical_and(c[0] >= 0, c[1])

    def body(c):
        j, _, st = c
        st = step(j, st)
        return j - 1, may_matter(j - 1, st), st

    return lax.while_loop(cond, body, (j0, may_matter(j0, states), states))[2]


def _fox_reach(q_half, kmax):
    qf = q_half.astype(F32)
    return jnp.sqrt(jnp.sum(qf * qf, axis=-1, keepdims=True)) * kmax * REACH_SLACK


def _fox_may_matter(j, states, reach, ck_of_block):
    jc = jnp.maximum(j, 0)
    worst = None
    for h in range(2):
        ck_last = ck_of_block(h, jc)[:, -1:]
        gap = reach[h] - ck_last - states[h][0]
        worst = gap if worst is None else jnp.maximum(worst, gap)
    return jnp.max(worst) >= SKIP_LOGIT


def _sb_may_matter(states):
    return jnp.max(jnp.maximum(states[0][0], states[1][0])) >= SKIP_LOGIT


def _max_key_norms(load_rows, n_rows, chunk):
    first = lax.broadcasted_iota(jnp.int32, (chunk, LANES), 1) < HEAD_DIM

    def body(t, best):
        kf = load_rows(pl.multiple_of(t * chunk, chunk), chunk).astype(BF16).astype(F32)
        sq = kf * kf
        n0 = jnp.sum(jnp.where(first, sq, 0.0), axis=-1, keepdims=True)
        n1 = jnp.sum(jnp.where(first, 0.0, sq), axis=-1, keepdims=True)
        return jnp.maximum(best[0], n0), jnp.maximum(best[1], n1)

    zero = jnp.zeros((chunk, 1), F32)
    best = lax.fori_loop(0, n_rows // chunk, body, (zero, zero))
    return [jnp.sqrt(jnp.max(b, axis=0, keepdims=True)) for b in best]


STRIP = 16
FOX_TILE = 512
SB_TILE = 256


def _strips(tq, tk, diag):
    for r in range(tq // STRIP):
        hi = (r + 1) * STRIP
        ncol = min(tk, -(-hi // LANES) * LANES) if diag else tk
        yield r * STRIP, slice(r * STRIP, hi), ncol


def _group_mask(row0, g, strict):
    lo = g * LANES
    if lo + LANES - 1 <= row0 - (1 if strict else 0):
        return None
    rr = row0 + lax.broadcasted_iota(jnp.int32, (STRIP, LANES), 0)
    cc = lo + lax.broadcasted_iota(jnp.int32, (STRIP, LANES), 1)
    return cc < rr if strict else cc <= rr


def _row_bcast(x):
    return jnp.broadcast_to(x, (x.shape[0], LANES))


HEADS = (0, 1)


def _fox_block(qs, kj, vj, ck_rows, diag, scr):
    s_scr, p_scr, m_scr, l_scr, a_scr, acc_scr = scr
    tq, tk = s_scr.shape[1:]
    for h in HEADS:
        s_scr[h] = _dot_t(qs[h], kj)

    def logits(h, row0, rows, ncol):
        out = []
        for g in range(ncol // LANES):
            cols = slice(g * LANES, (g + 1) * LANES)
            sg = s_scr[h, rows, cols] - ck_rows[h][:, cols]
            mask = _group_mask(row0, g, strict=False) if diag else None
            out.append(sg if mask is None else jnp.where(mask, sg, -jnp.inf))
        return out

    for h in HEADS:
        for row0, rows, ncol in _strips(tq, tk, diag):
            part = functools.reduce(jnp.maximum, logits(h, row0, rows, ncol))
            m_new = _row_bcast(jnp.max(part, axis=-1, keepdims=True))
            if not diag:
                m_old = m_scr[h, rows, :]
                m_new = jnp.maximum(m_old, m_new)
                a_scr[h, rows, :] = jnp.exp(m_old - m_new)
            m_scr[h, rows, :] = m_new
    for h in HEADS:
        for row0, rows, ncol in _strips(tq, tk, diag):
            m_new = m_scr[h, rows, :]
            pg = [jnp.exp(sg - m_new) for sg in logits(h, row0, rows, ncol)]
            lsum = functools.reduce(jnp.add, pg)
            if diag:
                l_scr[h, rows, :] = lsum
            else:
                alpha = a_scr[h, rows, :]
                l_scr[h, rows, :] = alpha * l_scr[h, rows, :] + lsum
                acc_scr[h, rows, :] = alpha * acc_scr[h, rows, :]
            pg = [p.astype(BF16) for p in pg]
            pg += [jnp.zeros((STRIP, LANES), BF16)] * ((tk - ncol) // LANES)
            p_scr[h, rows, :] = jnp.concatenate(pg, axis=1)
    for h in HEADS:
        pv = _dot(p_scr[h], vj)
        acc_scr[h] = pv if diag else acc_scr[h] + pv


def _log_1m_sigmoid(z):
    t = jnp.log2(1.0 + jnp.exp2(jnp.abs(z) * (-LOG2_E)))
    return t * (-LN_2) - jnp.maximum(z, 0.0)


def _sb_block(qs, kj, vj, tri, diag, scr):
    z_scr, hi_scr, lo_scr, b_scr, p_scr, c_scr, d_scr, acc_scr = scr
    tq, tk = z_scr.shape[1:]
    zero = jnp.zeros((STRIP, LANES), BF16)
    for h in HEADS:
        z_scr[h] = _dot_t(qs[h], kj)
    for h in HEADS:
        for row0, rows, ncol in _strips(tq, tk, diag):
            his, los, dsum = [], [], None
            for g in range(ncol // LANES):
                cols = slice(g * LANES, (g + 1) * LANES)
                z = z_scr[h, rows, cols]
                log_1m = _log_1m_sigmoid(z)
                mask = _group_mask(row0, g, strict=True) if diag else None
                if mask is not None:
                    log_1m = jnp.where(mask, log_1m, 0.0)
                hi = log_1m.astype(BF16)
                his.append(hi)
                los.append((log_1m - hi.astype(F32)).astype(BF16))
                z_scr[h, rows, cols] = z + log_1m
                dsum = log_1m if dsum is None else dsum + log_1m
            pad = [zero] * ((tk - ncol) // LANES)
            hi_scr[h, rows, :] = jnp.concatenate(his + pad, axis=1)
            lo_scr[h, rows, :] = jnp.concatenate(los + pad, axis=1)
            d_scr[h, rows, :] = dsum
    for h in HEADS:
        b_scr[h] = _dot(hi_scr[h], tri) + _dot(lo_scr[h], tri)
    for h in HEADS:
        for row0, rows, ncol in _strips(tq, tk, diag):
            carry = None if diag else c_scr[h, rows, :]
            pg = []
            for g in range(ncol // LANES):
                cols = slice(g * LANES, (g + 1) * LANES)
                e = z_scr[h, rows, cols] + b_scr[h, rows, cols]
                a = jnp.exp(e if carry is None else e + carry)
                mask = _group_mask(row0, g, strict=True) if diag else None
                pg.append((a if mask is None else jnp.where(mask, a, 0.0)).astype(BF16))
            p_scr[h, rows, :] = jnp.concatenate(pg + [zero] * ((tk - ncol) // LANES), axis=1)
    for h in HEADS:
        pv = _dot(p_scr[h], vj)
        acc_scr[h] = pv if diag else acc_scr[h] + pv
        d = _row_bcast(jnp.sum(d_scr[h], axis=-1, keepdims=True))
        c_scr[h] = d if diag else c_scr[h] + d


def _fox_prompt_kernel(q_ref, k_ref, v_ref, ck_ref, o_ref, kmax_ref, *scr, tq, nk):
    i = pl.program_id(2)

    @pl.when(i == 0)
    def _():
        norms = _max_key_norms(lambda off, n: k_ref[pl.ds(off, n), :], nk * tq, tq)
        for h in range(2):
            kmax_ref[h] = jnp.broadcast_to(norms[h], kmax_ref.shape[1:])

    qs = _split_heads(q_ref[...])
    reach = [_fox_reach(qs[h], kmax_ref[h, 0:1, 0:1]) for h in range(2)]
    ck_of_block = lambda h, j: ck_ref[pl.ds(h * nk + j, 1), :]
    m_scr = scr[2]

    def step(j, carry, diag=False):
        off = pl.multiple_of(j * tq, tq)
        kj = k_ref[pl.ds(off, tq), :]
        vj = v_ref[pl.ds(off, tq), :]
        _fox_block(qs, kj, vj, [ck_of_block(h, j) for h in HEADS], diag, scr)
        return carry

    def may_matter(j, carry):
        return _fox_may_matter(j, [(m_scr[h],) for h in range(2)], reach, ck_of_block)

    step(i, 0, diag=True)
    _newest_first(i - 1, 0, step, may_matter)
    l_scr, acc_scr = scr[3], scr[5]
    outs = [acc_scr[h] / jnp.sum(l_scr[h], axis=-1, keepdims=True) for h in range(2)]
    o_ref[...] = _merge_heads(outs[0], outs[1]).astype(o_ref.dtype)


def _sb_prompt_kernel(q_ref, k_ref, v_ref, o_ref, *scr, tq):
    i = pl.program_id(2)
    qs = _split_heads(q_ref[...])
    tri = _strict_lower(tq)
    c_scr, acc_scr = scr[5], scr[7]

    def step(j, carry, diag=False):
        off = pl.multiple_of(j * tq, tq)
        kj = k_ref[pl.ds(off, tq), :]
        vj = v_ref[pl.ds(off, tq), :]
        _sb_block(qs, kj, vj, tri, diag, scr)
        return carry

    step(i, 0, diag=True)
    _newest_first(i - 1, 0, step, lambda j, carry: _sb_may_matter([(c_scr[0],), (c_scr[1],)]))
    o_ref[...] = _merge_heads(acc_scr[0], acc_scr[1]).astype(o_ref.dtype)


def _prompt_attention(q, k, v, ck, batch, seq, tq):
    n, width = q.shape
    pairs = width // LANES
    nq = seq // tq
    q_spec = pl.BlockSpec((tq, LANES), lambda b, p, i: (b * nq + i, p))
    kv_spec = pl.BlockSpec((seq, LANES), lambda b, p, i: (b, p))
    tile = lambda dt: pltpu.VMEM((2, tq, tq), dt)
    acc = pltpu.VMEM((2, tq, LANES), F32)
    column = acc
    if ck is None:
        kernel = functools.partial(_sb_prompt_kernel, tq=tq)
        in_specs, args, name = [q_spec, kv_spec, kv_spec], (q, k, v), "sb_prompt"
        scratch = [tile(F32), tile(BF16), tile(BF16), tile(F32), tile(BF16), column, column, acc]
    else:
        kernel = functools.partial(_fox_prompt_kernel, tq=tq, nk=nq)
        ck_spec = pl.BlockSpec((None, 2 * nq, tq), lambda b, p, i: (b * pairs + p, 0, 0))
        in_specs, args, name = [q_spec, kv_spec, kv_spec, ck_spec], (q, k, v, ck), "fox_prompt"
        scratch = [pltpu.VMEM((2, 8, LANES), F32), tile(F32), tile(BF16), column, column, column, acc]
    return pl.pallas_call(
        kernel,
        grid=(batch, pairs, nq),
        in_specs=in_specs,
        out_specs=q_spec,
        out_shape=jax.ShapeDtypeStruct((n, width), BF16),
        scratch_shapes=scratch,
        compiler_params=_params("parallel", "parallel", "arbitrary"),
        name=name,
    )(*args)


def _ab_sample_kernel(fq_ref, fk_ref, fv_ref, cfk_ref, cfv_ref, ckc_ref, ckn_ref,
                      sq_ref, sk_ref, sv_ref, csk_ref, csv_ref, fo_ref, so_ref, *, t, tkc, nkc):
    row = lax.broadcasted_iota(jnp.int32, (t, t), 0)
    col = lax.broadcasted_iota(jnp.int32, (t, t), 1)

    qs = _split_heads(fq_ref[...])
    kmax = _max_key_norms(lambda off, n: cfk_ref[pl.ds(off, n), :], nkc * tkc, tkc)
    reach = [_fox_reach(qs[h], kmax[h]) for h in range(2)]
    ck_of_block = lambda h, j: ckc_ref[pl.ds(h * nkc + j, 1), :]

    def fox_cache(j, states):
        off = pl.multiple_of(j * tkc, tkc)
        kj = cfk_ref[pl.ds(off, tkc), :].astype(BF16)
        vj = cfv_ref[pl.ds(off, tkc), :].astype(BF16)
        return tuple(
            _fox_update(states[h], qs[h], kj, vj, ck_of_block(h, j), None) for h in range(2))

    states = tuple(
        _fox_update(_fox_init(t), qs[h], fk_ref[...], fv_ref[...], ckn_ref[h:h + 1, :], col <= row)
        for h in range(2))
    states = _newest_first(nkc - 1, states, fox_cache,
                           lambda j, st: _fox_may_matter(j, st, reach, ck_of_block))
    outs = [acc / l for (_, l, acc) in states]
    fo_ref[...] = _merge_heads(outs[0], outs[1]).astype(fo_ref.dtype)

    qs2 = _split_heads(sq_ref[...])
    tri_new = _strict_lower(t)
    tri_cache = _strict_lower(tkc)
    sstates = tuple(
        _sb_update(_sb_init(t), qs2[h], sk_ref[...], sv_ref[...], tri_new, col < row)
        for h in range(2))

    def sb_cache(j, st):
        off = pl.multiple_of(j * tkc, tkc)
        kj = csk_ref[pl.ds(off, tkc), :].astype(BF16)
        vj = csv_ref[pl.ds(off, tkc), :].astype(BF16)
        return tuple(_sb_update(st[h], qs2[h], kj, vj, tri_cache, None) for h in range(2))

    sstates = _newest_first(nkc - 1, sstates, sb_cache, lambda j, st: _sb_may_matter(st))
    so_ref[...] = _merge_heads(sstates[0][1], sstates[1][1]).astype(so_ref.dtype)


def _ab_sample_attention(fq, fk, fv, cfk, cfv, ck_cache, ck_new, sq, sk, sv, csk, csv,
                         batch, t, past, tkc):
    n, width = fq.shape
    pairs = width // LANES
    nkc = past // tkc
    new_spec = pl.BlockSpec((t, LANES), lambda b, p: (b, p))
    cache_spec = pl.BlockSpec((past, LANES), lambda b, p: (b, p))
    ckc_spec = pl.BlockSpec((None, 2 * nkc, tkc), lambda b, p: (b * pairs + p, 0, 0))
    ckn_spec = pl.BlockSpec((None, 2, t), lambda b, p: (b * pairs + p, 0, 0))
    out = jax.ShapeDtypeStruct((n, width), BF16)
    return pl.pallas_call(
        functools.partial(_ab_sample_kernel, t=t, tkc=tkc, nkc=nkc),
        grid=(batch, pairs),
        in_specs=[new_spec, new_spec, new_spec, cache_spec, cache_spec, ckc_spec, ckn_spec,
                  new_spec, new_spec, new_spec, cache_spec, cache_spec],
        out_specs=[new_spec, new_spec],
        out_shape=[out, out],
        compiler_params=_params("parallel", "parallel"),
        name="ab_sample",
    )(fq, fk, fv, cfk, cfv, ck_cache, ck_new, sq, sk, sv, csk, csv)


def _out_proj_kernel(*refs, n_in):
    h_ref = refs[0]
    o_ref = refs[-1]
    acc = h_ref[...]
    for a_ref, w_ref in zip(refs[1:1 + n_in], refs[1 + n_in:1 + 2 * n_in]):
        acc = acc + _dot(a_ref[...], w_ref[...])
    o_ref[...] = acc


def _out_proj(h, acts, weights):
    n, d = h.shape
    tm = _row_tile(n, 512)
    row = lambda i: (i, 0)
    const = lambda i: (0, 0)
    in_specs = [pl.BlockSpec((tm, d), row)]
    in_specs += [pl.BlockSpec((tm, a.shape[1]), row) for a in acts]
    in_specs += [pl.BlockSpec(w.shape, const) for w in weights]
    return pl.pallas_call(
        functools.partial(_out_proj_kernel, n_in=len(acts)),
        grid=(n // tm,),
        in_specs=in_specs,
        out_specs=pl.BlockSpec((tm, d), row),
        out_shape=jax.ShapeDtypeStruct((n, d), F32),
        compiler_params=_params("parallel"),
        name="out_proj",
    )(h, *acts, *weights)


def _ffn_kernel(x_ref, g_ref, wg_ref, wu_ref, wd_ref, o_ref, xn_ref, acc_ref):
    f = pl.program_id(1)

    @pl.when(f == 0)
    def _():
        xn_ref[...] = _rms(x_ref[...], g_ref[...]).astype(BF16)
        acc_ref[...] = jnp.zeros_like(acc_ref)

    xn = xn_ref[...]
    gate = _dot(xn, wg_ref[...])
    up = _dot(xn, wu_ref[...])
    act = (gate * jax.nn.sigmoid(gate) * up).astype(BF16)
    acc_ref[...] += _dot(act, wd_ref[...])

    @pl.when(f == pl.num_programs(1) - 1)
    def _():
        o_ref[...] = x_ref[...] + acc_ref[...]


def _ffn_tile(d_ff):
    for parts in (2, 4, 11, 22):
        if d_ff % (parts * LANES) == 0:
            return d_ff // parts
    return d_ff


def _ffn(h, g, w_gate, w_up, w_down):
    n, d = h.shape
    d_ff = w_gate.shape[1]
    tm = _row_tile(n, 512)
    tf = _ffn_tile(d_ff)
    return pl.pallas_call(
        _ffn_kernel,
        grid=(n // tm, d_ff // tf),
        in_specs=[pl.BlockSpec((tm, d), lambda i, f: (i, 0)), pl.BlockSpec((1, d), lambda i, f: (0, 0)),
                  pl.BlockSpec((d, tf), lambda i, f: (0, f)), pl.BlockSpec((d, tf), lambda i, f: (0, f)),
                  pl.BlockSpec((tf, d), lambda i, f: (f, 0))],
        out_specs=pl.BlockSpec((tm, d), lambda i, f: (i, 0)),
        out_shape=jax.ShapeDtypeStruct((n, d), F32),
        scratch_shapes=[pltpu.VMEM((tm, d), BF16), pltpu.VMEM((tm, d), F32)],
        compiler_params=_params("parallel", "arbitrary"),
        name="ffn",
    )(h, g, w_gate, w_up, w_down)


def _pe_kernel(x_ref, p_ref, g_ref, wg_ref, wp_ref, gf_ref, o_ref, *, final_norm):
    x = x_ref[...]
    gate = jax.nn.sigmoid(_dot(_rms(x, g_ref[...]).astype(BF16), wg_ref[...]))
    h = x + _dot(p_ref[...].astype(BF16), wp_ref[...]) * gate
    o_ref[...] = _rms(h, gf_ref[...]) if final_norm else h


def _per_layer_embed(h, p, g, w_gate, w_proj, g_final, final_norm):
    n, d = h.shape
    tm = _row_tile(n, 512)
    row = lambda i: (i, 0)
    const = lambda i: (0, 0)
    return pl.pallas_call(
        functools.partial(_pe_kernel, final_norm=final_norm),
        grid=(n // tm,),
        in_specs=[pl.BlockSpec((tm, d), row), pl.BlockSpec((tm, p.shape[1]), row),
                  pl.BlockSpec((1, d), const), pl.BlockSpec(w_gate.shape, const),
                  pl.BlockSpec(w_proj.shape, const), pl.BlockSpec((1, d), const)],
        out_specs=pl.BlockSpec((tm, d), row),
        out_shape=jax.ShapeDtypeStruct((n, d), F32),
        compiler_params=_params("parallel"),
        name="pe_final" if final_norm else "pe",
    )(h, p, g, w_gate, w_proj, g_final)


def _rope(y, cos, sin_lo, sin_hi):
    out = []
    for c in range(y.shape[1] // LANES):
        yc = y[:, c * LANES:(c + 1) * LANES]
        out.append(yc * cos + pltpu.roll(yc, 8, 1) * sin_hi + pltpu.roll(yc, LANES - 8, 1) * sin_lo)
    return out[0] if len(out) == 1 else jnp.concatenate(out, axis=1)


def _proj_c_kernel(x_ref, g_ref, w_ref, cos_ref, slo_ref, shi_ref,
                   k_ref, v_ref, qb_ref, kd_ref, vd_ref, *, q_w, kv_w):
    xn = _rms(x_ref[...], g_ref[...]).astype(BF16)
    tabs = (cos_ref[...], slo_ref[...], shi_ref[...])
    o = 0
    qb_ref[...] = (_rope(_dot(xn, w_ref[:, o:o + q_w]), *tabs) * SCALE).astype(BF16)
    o += q_w
    k_ref[...] = _rope(_dot(xn, w_ref[:, o:o + kv_w]), *tabs)
    o += kv_w
    v_ref[...] = _dot(xn, w_ref[:, o:o + kv_w])
    o += kv_w
    kd_ref[...] = _rope(_dot(xn, w_ref[:, o:o + 2 * kv_w]), *tabs).astype(BF16)
    o += 2 * kv_w
    vd_ref[...] = _dot(xn, w_ref[:, o:o + 2 * kv_w]).astype(BF16)


def _proj_c(x, g, w_all, tables, period, q_w, kv_w):
    n, d = x.shape
    tm = _row_tile(period, 512)
    nper = period // tm
    row = lambda i: (i, 0)
    const = lambda i: (0, 0)
    tab = pl.BlockSpec((tm, LANES), lambda i: (i % nper, 0))
    return pl.pallas_call(
        functools.partial(_proj_c_kernel, q_w=q_w, kv_w=kv_w),
        grid=(n // tm,),
        in_specs=[pl.BlockSpec((tm, d), row), pl.BlockSpec((1, d), const),
                  pl.BlockSpec(w_all.shape, const), tab, tab, tab],
        out_specs=[pl.BlockSpec((tm, kv_w), row), pl.BlockSpec((tm, kv_w), row),
                   pl.BlockSpec((tm, q_w), row), pl.BlockSpec((tm, 2 * kv_w), row),
                   pl.BlockSpec((tm, 2 * kv_w), row)],
        out_shape=[jax.ShapeDtypeStruct((n, kv_w), F32), jax.ShapeDtypeStruct((n, kv_w), F32),
                   jax.ShapeDtypeStruct((n, q_w), BF16), jax.ShapeDtypeStruct((n, 2 * kv_w), BF16),
                   jax.ShapeDtypeStruct((n, 2 * kv_w), BF16)],
        compiler_params=_params("parallel"),
        name="proj_c",
    )(x, g, w_all, *tables)


def _rope_tables(pos):
    half = ROPE_DIM // 2
    inv_freq = ROPE_THETA ** (-2.0 * jnp.arange(half, dtype=F32) / ROPE_DIM)
    ang = pos.astype(F32)[:, None] * inv_freq[None, :]
    cos, sin = jnp.cos(ang), jnp.sin(ang)
    n = pos.shape[0]
    rest = HEAD_DIM - ROPE_DIM
    cos_h = jnp.concatenate([cos, cos, jnp.ones((n, rest), F32)], axis=1)
    lo_h = jnp.concatenate([-sin, jnp.zeros((n, half + rest), F32)], axis=1)
    hi_h = jnp.concatenate([jnp.zeros((n, half), F32), sin, jnp.zeros((n, rest), F32)], axis=1)
    two = lambda a: jnp.concatenate([a, a], axis=1)
    return two(cos_h), two(lo_h), two(hi_h)


def _swa_kernel(sink_ref, q_ref, kprev_ref, kcur_ref, vprev_ref, vcur_ref, o_ref, kb_ref, vb_ref,
                *, tq, n_kv, group, first_pos_of_tile):
    band = WINDOW + CHUNK
    kb_ref[0:WINDOW, :] = kprev_ref[...]
    kb_ref[WINDOW:, :] = kcur_ref[...]
    vb_ref[0:WINDOW, :] = vprev_ref[...]
    vb_ref[WINDOW:, :] = vcur_ref[...]
    rows = group * CHUNK
    ridx = lax.broadcasted_iota(jnp.int32, (rows, 1), 0) // CHUNK
    col = lax.broadcasted_iota(jnp.int32, (rows, band), 1)
    tile_pos = None if first_pos_of_tile is None else first_pos_of_tile(pl.program_id(1))

    def chunk(cc, _):
        base = pl.multiple_of(cc * CHUNK, CHUNK)
        for kvh in range(n_kv):
            lhs = []
            for pr in range(group // 2):
                lane0 = (kvh * (group // 2) + pr) * LANES
                lhs.extend(_split_heads(q_ref[pl.ds(base, CHUNK), lane0:lane0 + LANES]))
            s = _dot_t(jnp.concatenate(lhs, axis=0), kb_ref[pl.ds(base, band), kvh * LANES:(kvh + 1) * LANES])
            if tile_pos is not None:
                s = jnp.where(tile_pos - WINDOW + cc * CHUNK + col >= 0, s, -jnp.inf)
            sink = jnp.zeros((rows, 1), F32)
            for gi in range(group):
                sink = jnp.where(ridx == gi, sink_ref[kvh * group + gi], sink)
            m = jnp.maximum(jnp.max(s, axis=-1, keepdims=True), sink)
            e = jnp.exp(s - m)
            den = jnp.sum(e, axis=-1, keepdims=True) + jnp.exp(sink - m)
            o = _dot(e.astype(BF16), vb_ref[pl.ds(base, band), kvh * LANES:(kvh + 1) * LANES]) / den
            for pr in range(group // 2):
                lane0 = (kvh * (group // 2) + pr) * LANES
                pair = _merge_heads(o[(2 * pr) * CHUNK:(2 * pr + 1) * CHUNK],
                                    o[(2 * pr + 1) * CHUNK:(2 * pr + 2) * CHUNK])
                o_ref[pl.ds(base, CHUNK), lane0:lane0 + LANES] = pair.astype(o_ref.dtype)
        return 0

    lax.fori_loop(0, tq // CHUNK, chunk, 0)


def _swa_prompt(sinks, q, kd, vd, batch, seq, n_kv):
    n, q_w = q.shape
    group = q_w // HEAD_DIM // n_kv
    tq = _row_tile(seq, 512)
    nt = seq // tq
    per_win = tq // WINDOW
    cur = lambda b, i: (b * nt + i, 0)
    prev = lambda b, i: (jnp.maximum(b * nt * per_win + i * per_win - 1, b * nt * per_win), 0)
    kv_w = kd.shape[1]
    return pl.pallas_call(
        functools.partial(_swa_kernel, tq=tq, n_kv=n_kv, group=group,
                          first_pos_of_tile=lambda i: i * tq),
        grid=(batch, nt),
        in_specs=[pl.BlockSpec(memory_space=pltpu.SMEM),
                  pl.BlockSpec((tq, q_w), cur),
                  pl.BlockSpec((WINDOW, kv_w), prev), pl.BlockSpec((tq, kv_w), cur),
                  pl.BlockSpec((WINDOW, kv_w), prev), pl.BlockSpec((tq, kv_w), cur)],
        out_specs=pl.BlockSpec((tq, q_w), cur),
        out_shape=jax.ShapeDtypeStruct((n, q_w), BF16),
        scratch_shapes=[pltpu.VMEM((WINDOW + tq, kv_w), BF16), pltpu.VMEM((WINDOW + tq, kv_w), BF16)],
        compiler_params=_params("parallel", "parallel"),
        name="swa_prompt",
    )(sinks, q, kd, kd, vd, vd)


def _swa_sample(sinks, q, kd_cache, kd_new, vd_cache, vd_new, batch, t, n_kv):
    n, q_w = q.shape
    group = q_w // HEAD_DIM // n_kv
    kv_w = kd_new.shape[1]
    blk = lambda b, i: (b, 0)
    return pl.pallas_call(
        functools.partial(_swa_kernel, tq=t, n_kv=n_kv, group=group, first_pos_of_tile=None),
        grid=(batch, 1),
        in_specs=[pl.BlockSpec(memory_space=pltpu.SMEM),
                  pl.BlockSpec((t, q_w), blk),
                  pl.BlockSpec((WINDOW, kv_w), blk), pl.BlockSpec((t, kv_w), blk),
                  pl.BlockSpec((WINDOW, kv_w), blk), pl.BlockSpec((t, kv_w), blk)],
        out_specs=pl.BlockSpec((t, q_w), blk),
        out_shape=jax.ShapeDtypeStruct((n, q_w), BF16),
        scratch_shapes=[pltpu.VMEM((WINDOW + t, kv_w), BF16), pltpu.VMEM((WINDOW + t, kv_w), BF16)],
        compiler_params=_params("parallel", "parallel"),
        name="swa_sample",
    )(sinks, q, kd_cache, kd_new, vd_cache, vd_new)


def _dup_heads(a, n_heads):
    lead = a.shape[:-1]
    a = a.reshape(*lead, n_heads, 1, HEAD_DIM)
    return jnp.broadcast_to(a, (*lead, n_heads, 2, HEAD_DIM)).reshape(*lead, n_heads * LANES)


def _prep_weights(W, n_fox, n_sb, n_kv):
    fox_w = n_fox * HEAD_DIM
    w_in = W["w_ab_in"][0]
    w_main = jnp.concatenate([w_in[:, :3 * fox_w], w_in[:, 3 * fox_w + n_fox:]], axis=1).astype(BF16)
    w_flog = jnp.pad(w_in[:, 3 * fox_w:3 * fox_w + n_fox], ((0, 0), (0, LANES - n_fox))).astype(BF16)
    b_f = jnp.pad(W["b_fox_f"][0][None, :], ((0, 0), (0, LANES - n_fox)))
    w_c = W["w_c_in"][0]
    kv_w = n_kv * HEAD_DIM
    q_w = w_c.shape[1] - 2 * kv_w
    w_k, w_v = w_c[:, q_w:q_w + kv_w], w_c[:, q_w + kv_w:]
    w_c_all = jnp.concatenate([w_c, _dup_heads(w_k, n_kv), _dup_heads(w_v, n_kv)], axis=1).astype(BF16)
    bf = lambda name, i: W[name][i].astype(BF16)
    out = dict(w_main=w_main, w_flog=w_flog, b_f=b_f, w_c_all=w_c_all, q_w=q_w, kv_w=kv_w,
               w_ab_out_f=W["w_ab_out"][0][:fox_w].astype(BF16),
               w_ab_out_s=W["w_ab_out"][0][fox_w:].astype(BF16),
               w_c_out=bf("w_c_out", 0), c_sinks=W["c_sinks"][0])
    for i in range(2):
        out[f"ffn{i}"] = (W["norm_ffn"][i][None, :], bf("w_ffn_gate", i), bf("w_ffn_up", i),
                          bf("w_ffn_down", i))
        out[f"pe{i}"] = (W["norm_pe"][i][None, :], bf("w_pe_gate", i), bf("w_pe_proj", i))
    out["norm_mix"] = [W["norm_mix"][i][None, :] for i in range(2)]
    out["norm_final"] = W["norm_final"][None, :]
    return out


def _pad_lanes(a):
    pad = (-a.shape[1]) % LANES
    return jnp.pad(a, ((0, 0), (0, pad))) if pad else a


def _layer_tail(h, p, P, i):
    h = _ffn(h, *P[f"ffn{i}"])
    return _per_layer_embed(h, p, *P[f"pe{i}"], P["norm_final"], final_norm=(i == 1))


def _trunk_prompt(x, p, P, n_fox, n_kv):
    batch, seq, d = x.shape
    n = batch * seq
    h = x.reshape(n, d)
    fk, fv, sk, sv, logf, fqb, fkb, fvb, sqb, skb, svb = _proj_ab(
        h, P["norm_mix"][0], P["w_main"], P["w_flog"], P["b_f"], n_fox)
    tq_fox = _row_tile(seq, FOX_TILE)
    lf_rows = jnp.swapaxes(logf.reshape(batch, seq, n_fox), 1, 2).reshape(batch * n_fox, seq)
    ck = _cumsum_rows(lf_rows).reshape(batch * n_fox // 2, 2 * (seq // tq_fox), tq_fox)
    fo = _prompt_attention(fqb, fkb, fvb, ck, batch, seq, tq_fox)
    so = _prompt_attention(sqb, skb, svb, None, batch, seq, _row_tile(seq, SB_TILE))
    h = _out_proj(h, [fo, so], [P["w_ab_out_f"], P["w_ab_out_s"]])
    h = _layer_tail(h, p[0].reshape(n, -1), P, 0)
    tables = _rope_tables(jnp.arange(seq))
    k, v, qb, kd, vd = _proj_c(h, P["norm_mix"][1], P["w_c_all"], tables, seq, P["q_w"], P["kv_w"])
    o = _swa_prompt(P["c_sinks"], qb, kd, vd, batch, seq, n_kv)
    h = _out_proj(h, [o], [P["w_c_out"]])
    y = _layer_tail(h, p[1].reshape(n, -1), P, 1)
    heads = lambda a, nh: a.reshape(1, batch, seq, nh, HEAD_DIM)
    kw = k.reshape(batch, seq, n_kv, HEAD_DIM)[None, :, seq - WINDOW:]
    vw = v.reshape(batch, seq, n_kv, HEAD_DIM)[None, :, seq - WINDOW:]
    return y.reshape(batch, seq, d), (heads(fk, n_fox), heads(fv, n_fox),
                                      logf.reshape(1, batch, seq, n_fox),
                                      heads(sk, n_fox), heads(sv, n_fox), kw, vw)


def _trunk_sample(x, p, P, caches, n_fox, n_kv):
    c_fk, c_fv, c_fl, c_sk, c_sv, c_wk, c_wv = caches
    batch, t, d = x.shape
    past = c_fk.shape[2]
    n = batch * t
    h = x.reshape(n, d)
    width = n_fox * HEAD_DIM
    fk, fv, sk, sv, logf, fqb, fkb, fvb, sqb, skb, svb = _proj_ab(
        h, P["norm_mix"][0], P["w_main"], P["w_flog"], P["b_f"], n_fox)
    lf_all = jnp.concatenate([c_fl[0], logf.reshape(batch, t, n_fox)], axis=1)
    lf_rows = _pad_lanes(jnp.swapaxes(lf_all, 1, 2).reshape(batch * n_fox, past + t))
    c_rows = _cumsum_rows(lf_rows)
    tkc = _row_tile(past, 256)
    ck_cache = c_rows[:, :past].reshape(batch * n_fox // 2, 2 * (past // tkc), tkc)
    ck_new = c_rows[:, past:past + t].reshape(batch * n_fox // 2, 2, t)
    flat = lambda c: c[0].reshape(batch * past, width)
    fo, so = _ab_sample_attention(fqb, fkb, fvb, flat(c_fk), flat(c_fv), ck_cache, ck_new,
                                  sqb, skb, svb, flat(c_sk), flat(c_sv), batch, t, past, tkc)
    h = _out_proj(h, [fo, so], [P["w_ab_out_f"], P["w_ab_out_s"]])
    h = _layer_tail(h, p[0].reshape(n, -1), P, 0)
    tm = _row_tile(n, 512)
    pos = past + jnp.arange(t)
    tables = _rope_tables(jnp.tile(pos, max(tm // t, 1)))
    k, v, qb, kd, vd = _proj_c(h, P["norm_mix"][1], P["w_c_all"], tables, tables[0].shape[0],
                               P["q_w"], P["kv_w"])
    kd_cache = _dup_heads(c_wk[0].reshape(batch * WINDOW, n_kv * HEAD_DIM), n_kv).astype(BF16)
    vd_cache = _dup_heads(c_wv[0].reshape(batch * WINDOW, n_kv * HEAD_DIM), n_kv).astype(BF16)
    o = _swa_sample(P["c_sinks"], qb, kd_cache, kd, vd_cache, vd, batch, t, n_kv)
    h = _out_proj(h, [o], [P["w_c_out"]])
    y = _layer_tail(h, p[1].reshape(n, -1), P, 1)
    heads = lambda a, nh: a.reshape(1, batch, t, nh, HEAD_DIM)
    roll = lambda c, new: jnp.concatenate([c[0], new.reshape(batch, t, n_kv, HEAD_DIM)], axis=1)[None, :, -WINDOW:]
    return y.reshape(batch, t, d), (heads(fk, n_fox), heads(fv, n_fox),
                                    logf.reshape(1, batch, t, n_fox),
                                    heads(sk, n_fox), heads(sv, n_fox), roll(c_wk, k), roll(c_wv, v))


def kernel(x_prompt, x_sample, cache_fox_k, cache_fox_v, cache_fox_logf, cache_sb_k, cache_sb_v, cache_swa_k, cache_swa_v, p_prompt, p_sample, norm_mix, w_ab_in, b_fox_f, w_ab_out, w_c_in, c_sinks, w_c_out, norm_ffn, w_ffn_gate, w_ffn_up, w_ffn_down, norm_pe, w_pe_gate, w_pe_proj, norm_final):
    W = dict(norm_mix=norm_mix, w_ab_in=w_ab_in, b_fox_f=b_fox_f, w_ab_out=w_ab_out, w_c_in=w_c_in,
             c_sinks=c_sinks, w_c_out=w_c_out, norm_ffn=norm_ffn, w_ffn_gate=w_ffn_gate,
             w_ffn_up=w_ffn_up, w_ffn_down=w_ffn_down, norm_pe=norm_pe, w_pe_gate=w_pe_gate,
             w_pe_proj=w_pe_proj, norm_final=norm_final)
    n_fox = cache_fox_k.shape[3]
    n_sb = cache_sb_k.shape[3]
    n_kv = cache_swa_k.shape[3]
    assert n_fox == n_sb and n_fox % 2 == 0 and n_kv % 1 == 0
    assert cache_swa_k.shape[2] == WINDOW and x_sample.shape[1] == CHUNK
    P = _prep_weights(W, n_fox, n_sb, n_kv)
    caches = (cache_fox_k, cache_fox_v, cache_fox_logf, cache_sb_k, cache_sb_v, cache_swa_k, cache_swa_v)
    y_p, st_p = _trunk_prompt(x_prompt, p_prompt, P, n_fox, n_kv)
    y_s, st_s = _trunk_sample(x_sample, p_sample, P, caches, n_fox, n_kv)
    return (y_p, y_s, *st_p, *st_s)
```

```python
import functools

import jax
import jax.numpy as jnp
from jax import lax
from jax.experimental import pallas as pl
from jax.experimental.pallas import tpu as pltpu

F32 = jnp.float32
BF16 = jnp.bfloat16

HEAD_DIM = 64
CHUNK = 64
WINDOW = 128
ROPE_DIM = 16
ROPE_THETA = 500000.0
RMS_EPS = 1e-6
SCALE = HEAD_DIM ** -0.5
SKIP_LOGIT = -110.0
REACH_SLACK = 1.01
LOG2_E = 1.4426950408889634
LN_2 = 0.6931471805599453

LANES = 128
V7X_VMEM_BYTES = 64 * 1024 * 1024
VMEM_LIMIT = V7X_VMEM_BYTES - 8 * 1024 * 1024


def _params(*sem):
    return pltpu.CompilerParams(dimension_semantics=sem, vmem_limit_bytes=VMEM_LIMIT)


def _row_tile(n, pref):
    t = pref
    while n % t:
        t //= 2
    return t


def _rms(x, g):
    ms = jnp.mean(x * x, axis=-1, keepdims=True)
    return x * lax.rsqrt(ms + RMS_EPS) * g


def _dot(a, b):
    return jnp.dot(a, b, preferred_element_type=F32)


def _dot_t(a, b):
    return lax.dot_general(a, b, (((1,), (1,)), ((), ())), preferred_element_type=F32)


def _softplus(z):
    return jnp.maximum(z, 0.0) + jnp.log(1.0 + jnp.exp(-jnp.abs(z)))


def _split_heads(q):
    first = lax.broadcasted_iota(jnp.int32, q.shape, 1) < HEAD_DIM
    zero = jnp.zeros_like(q)
    return jnp.where(first, q, zero), jnp.where(first, zero, q)


def _merge_heads(a, b):
    first = lax.broadcasted_iota(jnp.int32, a.shape, 1) < HEAD_DIM
    return jnp.where(first, a, b)


def _proj_ab_kernel(x_ref, g_ref, w_ref, wf_ref, bf_ref,
                    fk_ref, fv_ref, sk_ref, sv_ref, lf_ref,
                    fqb_ref, fkb_ref, fvb_ref, sqb_ref, skb_ref, svb_ref, *, width):
    xn = _rms(x_ref[...], g_ref[...]).astype(BF16)

    def col(c):
        return _dot(xn, w_ref[:, c * width:(c + 1) * width])

    fqb_ref[...] = (col(0) * SCALE).astype(BF16)
    y = col(1)
    fk_ref[...] = y.reshape(fk_ref.shape)
    fkb_ref[...] = y.astype(BF16)
    y = col(2)
    fv_ref[...] = y.reshape(fv_ref.shape)
    fvb_ref[...] = y.astype(BF16)
    sqb_ref[...] = (col(3) * SCALE).astype(BF16)
    y = col(4)
    sk_ref[...] = y.reshape(sk_ref.shape)
    skb_ref[...] = y.astype(BF16)
    y = col(5)
    sv_ref[...] = y.reshape(sv_ref.shape)
    svb_ref[...] = y.astype(BF16)
    u = _dot(xn, wf_ref[...]) + bf_ref[...]
    lf = -_softplus(-u)
    lf_ref[...] = lf[:, :lf_ref.shape[1]]


def _proj_ab(x, g, w_main, w_f, b_f, n_heads):
    n, d = x.shape
    width = n_heads * HEAD_DIM
    tm = _row_tile(n, 512)
    row = lambda i: (i, 0)
    const = lambda i: (0, 0)
    f32_out = jax.ShapeDtypeStruct((n, n_heads, HEAD_DIM), F32)
    hblk = pl.BlockSpec((tm, n_heads, HEAD_DIM), lambda i: (i, 0, 0))
    bf_out = jax.ShapeDtypeStruct((n, width), BF16)
    blk = pl.BlockSpec((tm, width), row)
    return pl.pallas_call(
        functools.partial(_proj_ab_kernel, width=width),
        grid=(n // tm,),
        in_specs=[pl.BlockSpec((tm, d), row), pl.BlockSpec((1, d), const),
                  pl.BlockSpec(w_main.shape, const), pl.BlockSpec(w_f.shape, const),
                  pl.BlockSpec(b_f.shape, const)],
        out_specs=[hblk, hblk, hblk, hblk, pl.BlockSpec((tm, n_heads), row)] + [blk] * 6,
        out_shape=[f32_out] * 4 + [jax.ShapeDtypeStruct((n, n_heads), F32)] + [bf_out] * 6,
        compiler_params=_params("parallel"),
        name="proj_ab",
    )(x, g, w_main, w_f, b_f)


def _cumsum_kernel(x_ref, o_ref, *, n_chunks):
    r = lax.broadcasted_iota(jnp.int32, (LANES, LANES), 0)
    c = lax.broadcasted_iota(jnp.int32, (LANES, LANES), 1)
    upper = (r <= c).astype(BF16)

    def body(t, carry):
        off = pl.multiple_of(t * LANES, LANES)
        x = x_ref[:, pl.ds(off, LANES)]
        x1 = x.astype(BF16)
        r1 = x - x1.astype(F32)
        x2 = r1.astype(BF16)
        x3 = (r1 - x2.astype(F32)).astype(BF16)
        y = _dot(x1, upper) + _dot(x2, upper) + _dot(x3, upper) + carry
        o_ref[:, pl.ds(off, LANES)] = y
        return y[:, LANES - 1:LANES]

    lax.fori_loop(0, n_chunks, body, jnp.zeros((x_ref.shape[0], 1), F32))


def _cumsum_rows(x):
    r, l = x.shape
    tr = _row_tile(r, 16)
    return pl.pallas_call(
        functools.partial(_cumsum_kernel, n_chunks=l // LANES),
        grid=(r // tr,),
        in_specs=[pl.BlockSpec((tr, l), lambda i: (i, 0))],
        out_specs=pl.BlockSpec((tr, l), lambda i: (i, 0)),
        out_shape=jax.ShapeDtypeStruct((r, l), F32),
        compiler_params=_params("parallel"),
        name="cumsum_logf",
    )(x)


def _fox_update(state, qh, kj, vj, ck_row, valid):
    m, l, acc = state
    s = _dot_t(qh, kj) - ck_row
    if valid is not None:
        s = jnp.where(valid, s, -jnp.inf)
    m_new = jnp.maximum(m, jnp.max(s, axis=-1, keepdims=True))
    alpha = jnp.exp(m - m_new)
    p = jnp.exp(s - m_new)
    l = alpha * l + jnp.sum(p, axis=-1, keepdims=True)
    acc = alpha * acc + _dot(p.astype(BF16), vj)
    return m_new, l, acc


def _fox_init(tq):
    return (jnp.full((tq, 1), -jnp.inf, F32), jnp.zeros((tq, 1), F32), jnp.zeros((tq, LANES), F32))


def _sb_update(state, qh, kj, vj, tri, valid):
    carry, acc = state
    z = _dot_t(qh, kj)
    log_1m = -_softplus(z)
    if valid is not None:
        log_1m = jnp.where(valid, log_1m, 0.0)
    hi = log_1m.astype(BF16)
    lo = (log_1m - hi.astype(F32)).astype(BF16)
    between = _dot(hi, tri) + _dot(lo, tri)
    a = jnp.exp(z + log_1m + between + carry)
    if valid is not None:
        a = jnp.where(valid, a, 0.0)
    acc = acc + _dot(a.astype(BF16), vj)
    carry = carry + jnp.sum(log_1m, axis=-1, keepdims=True)
    return carry, acc


def _sb_init(tq):
    return (jnp.zeros((tq, 1), F32), jnp.zeros((tq, LANES), F32))


def _strict_lower(t):
    r = lax.broadcasted_iota(jnp.int32, (t, t), 0)
    c = lax.broadcasted_iota(jnp.int32, (t, t), 1)
    return (r > c).astype(BF16)


def _newest_first(j0, states, step, may_matter):
    def cond(c):
        return jnp.logical_and(c[0] >= 0, c[1])

    def body(c):
        j, _, st = c
        st = step(j, st)
        return j - 1, may_matter(j - 1, st), st

    return lax.while_loop(cond, body, (j0, may_matter(j0, states), states))[2]


def _fox_reach(q_half, kmax):
    qf = q_half.astype(F32)
    return jnp.sqrt(jnp.sum(qf * qf, axis=-1, keepdims=True)) * kmax * REACH_SLACK


def _fox_may_matter(j, states, reach, ck_of_block):
    jc = jnp.maximum(j, 0)
    worst = None
    for h in range(2):
        ck_last = ck_of_block(h, jc)[:, -1:]
        gap = reach[h] - ck_last - states[h][0]
        worst = gap if worst is None else jnp.maximum(worst, gap)
    return jnp.max(worst) >= SKIP_LOGIT


def _sb_may_matter(states):
    return jnp.max(jnp.maximum(states[0][0], states[1][0])) >= SKIP_LOGIT


def _max_key_norms(load_rows, n_rows, chunk):
    first = lax.broadcasted_iota(jnp.int32, (chunk, LANES), 1) < HEAD_DIM

    def body(t, best):
        kf = load_rows(pl.multiple_of(t * chunk, chunk), chunk).astype(BF16).astype(F32)
        sq = kf * kf
        n0 = jnp.sum(jnp.where(first, sq, 0.0), axis=-1, keepdims=True)
        n1 = jnp.sum(jnp.where(first, 0.0, sq), axis=-1, keepdims=True)
        return jnp.maximum(best[0], n0), jnp.maximum(best[1], n1)

    zero = jnp.zeros((chunk, 1), F32)
    best = lax.fori_loop(0, n_rows // chunk, body, (zero, zero))
    return [jnp.sqrt(jnp.max(b, axis=0, keepdims=True)) for b in best]


STRIP = 16
FOX_TILE = 512
SB_TILE = 256


def _strips(tq, tk, diag):
    for r in range(tq // STRIP):
        hi = (r + 1) * STRIP
        ncol = min(tk, -(-hi // LANES) * LANES) if diag else tk
        yield r * STRIP, slice(r * STRIP, hi), ncol


def _group_mask(row0, g, strict):
    lo = g * LANES
    if lo + LANES - 1 <= row0 - (1 if strict else 0):
        return None
    rr = row0 + lax.broadcasted_iota(jnp.int32, (STRIP, LANES), 0)
    cc = lo + lax.broadcasted_iota(jnp.int32, (STRIP, LANES), 1)
    return cc < rr if strict else cc <= rr


def _row_bcast(x):
    return jnp.broadcast_to(x, (x.shape[0], LANES))


HEADS = (0, 1)


def _fox_block(qs, kj, vj, ck_rows, diag, scr):
    s_scr, p_scr, m_scr, l_scr, a_scr, acc_scr = scr
    tq, tk = s_scr.shape[1:]
    for h in HEADS:
        s_scr[h] = _dot_t(qs[h], kj)

    def logits(h, row0, rows, ncol):
        out = []
        for g in range(ncol // LANES):
            cols = slice(g * LANES, (g + 1) * LANES)
            sg = s_scr[h, rows, cols] - ck_rows[h][:, cols]
            mask = _group_mask(row0, g, strict=False) if diag else None
            out.append(sg if mask is None else jnp.where(mask, sg, -jnp.inf))
        return out

    for h in HEADS:
        for row0, rows, ncol in _strips(tq, tk, diag):
            part = functools.reduce(jnp.maximum, logits(h, row0, rows, ncol))
            m_new = _row_bcast(jnp.max(part, axis=-1, keepdims=True))
            if not diag:
                m_old = m_scr[h, rows, :]
                m_new = jnp.maximum(m_old, m_new)
                a_scr[h, rows, :] = jnp.exp(m_old - m_new)
            m_scr[h, rows, :] = m_new
    for h in HEADS:
        for row0, rows, ncol in _strips(tq, tk, diag):
            m_new = m_scr[h, rows, :]
            pg = [jnp.exp(sg - m_new) for sg in logits(h, row0, rows, ncol)]
            lsum = functools.reduce(jnp.add, pg)
            if diag:
                l_scr[h, rows, :] = lsum
            else:
                alpha = a_scr[h, rows, :]
                l_scr[h, rows, :] = alpha * l_scr[h, rows, :] + lsum
                acc_scr[h, rows, :] = alpha * acc_scr[h, rows, :]
            pg = [p.astype(BF16) for p in pg]
            pg += [jnp.zeros((STRIP, LANES), BF16)] * ((tk - ncol) // LANES)
            p_scr[h, rows, :] = jnp.concatenate(pg, axis=1)
    for h in HEADS:
        pv = _dot(p_scr[h], vj)
        acc_scr[h] = pv if diag else acc_scr[h] + pv


def _log_1m_sigmoid(z):
    t = jnp.log2(1.0 + jnp.exp2(jnp.abs(z) * (-LOG2_E)))
    return t * (-LN_2) - jnp.maximum(z, 0.0)


def _sb_block(qs, kj, vj, tri, diag, scr):
    z_scr, hi_scr, lo_scr, b_scr, p_scr, c_scr, d_scr, acc_scr = scr
    tq, tk = z_scr.shape[1:]
    zero = jnp.zeros((STRIP, LANES), BF16)
    for h in HEADS:
        z_scr[h] = _dot_t(qs[h], kj)
    for h in HEADS:
        for row0, rows, ncol in _strips(tq, tk, diag):
            his, los, dsum = [], [], None
            for g in range(ncol // LANES):
                cols = slice(g * LANES, (g + 1) * LANES)
                z = z_scr[h, rows, cols]
                log_1m = _log_1m_sigmoid(z)
                mask = _group_mask(row0, g, strict=True) if diag else None
                if mask is not None:
                    log_1m = jnp.where(mask, log_1m, 0.0)
                hi = log_1m.astype(BF16)
                his.append(hi)
                los.append((log_1m - hi.astype(F32)).astype(BF16))
                z_scr[h, rows, cols] = z + log_1m
                dsum = log_1m if dsum is None else dsum + log_1m
            pad = [zero] * ((tk - ncol) // LANES)
            hi_scr[h, rows, :] = jnp.concatenate(his + pad, axis=1)
            lo_scr[h, rows, :] = jnp.concatenate(los + pad, axis=1)
            d_scr[h, rows, :] = dsum
    for h in HEADS:
        b_scr[h] = _dot(hi_scr[h], tri) + _dot(lo_scr[h], tri)
    for h in HEADS:
        for row0, rows, ncol in _strips(tq, tk, diag):
            carry = None if diag else c_scr[h, rows, :]
            pg = []
            for g in range(ncol // LANES):
                cols = slice(g * LANES, (g + 1) * LANES)
                e = z_scr[h, rows, cols] + b_scr[h, rows, cols]
                a = jnp.exp(e if carry is None else e + carry)
                mask = _group_mask(row0, g, strict=True) if diag else None
                pg.append((a if mask is None else jnp.where(mask, a, 0.0)).astype(BF16))
            p_scr[h, rows, :] = jnp.concatenate(pg + [zero] * ((tk - ncol) // LANES), axis=1)
    for h in HEADS:
        pv = _dot(p_scr[h], vj)
        acc_scr[h] = pv if diag else acc_scr[h] + pv
        d = _row_bcast(jnp.sum(d_scr[h], axis=-1, keepdims=True))
        c_scr[h] = d if diag else c_scr[h] + d


def _fox_prompt_kernel(q_ref, k_ref, v_ref, ck_ref, o_ref, kmax_ref, *scr, tq, nk):
    i = pl.program_id(2)

    @pl.when(i == 0)
    def _():
        norms = _max_key_norms(lambda off, n: k_ref[pl.ds(off, n), :], nk * tq, tq)
        for h in range(2):
            kmax_ref[h] = jnp.broadcast_to(norms[h], kmax_ref.shape[1:])

    qs = _split_heads(q_ref[...])
    reach = [_fox_reach(qs[h], kmax_ref[h, 0:1, 0:1]) for h in range(2)]
    ck_of_block = lambda h, j: ck_ref[pl.ds(h * nk + j, 1), :]
    m_scr = scr[2]

    def step(j, carry, diag=False):
        off = pl.multiple_of(j * tq, tq)
        kj = k_ref[pl.ds(off, tq), :]
        vj = v_ref[pl.ds(off, tq), :]
        _fox_block(qs, kj, vj, [ck_of_block(h, j) for h in HEADS], diag, scr)
        return carry

    def may_matter(j, carry):
        return _fox_may_matter(j, [(m_scr[h],) for h in range(2)], reach, ck_of_block)

    step(i, 0, diag=True)
    _newest_first(i - 1, 0, step, may_matter)
    l_scr, acc_scr = scr[3], scr[5]
    outs = [acc_scr[h] / jnp.sum(l_scr[h], axis=-1, keepdims=True) for h in range(2)]
    o_ref[...] = _merge_heads(outs[0], outs[1]).astype(o_ref.dtype)


def _sb_prompt_kernel(q_ref, k_ref, v_ref, o_ref, *scr, tq):
    i = pl.program_id(2)
    qs = _split_heads(q_ref[...])
    tri = _strict_lower(tq)
    c_scr, acc_scr = scr[5], scr[7]

    def step(j, carry, diag=False):
        off = pl.multiple_of(j * tq, tq)
        kj = k_ref[pl.ds(off, tq), :]
        vj = v_ref[pl.ds(off, tq), :]
        _sb_block(qs, kj, vj, tri, diag, scr)
        return carry

    step(i, 0, diag=True)
    _newest_first(i - 1, 0, step, lambda j, carry: _sb_may_matter([(c_scr[0],), (c_scr[1],)]))
    o_ref[...] = _merge_heads(acc_scr[0], acc_scr[1]).astype(o_ref.dtype)


def _prompt_attention(q, k, v, ck, batch, seq, tq):
    n, width = q.shape
    pairs = width // LANES
    nq = seq // tq
    q_spec = pl.BlockSpec((tq, LANES), lambda b, p, i: (b * nq + i, p))
    kv_spec = pl.BlockSpec((seq, LANES), lambda b, p, i: (b, p))
    tile = lambda dt: pltpu.VMEM((2, tq, tq), dt)
    acc = pltpu.VMEM((2, tq, LANES), F32)
    column = acc
    if ck is None:
        kernel = functools.partial(_sb_prompt_kernel, tq=tq)
        in_specs, args, name = [q_spec, kv_spec, kv_spec], (q, k, v), "sb_prompt"
        scratch = [tile(F32), tile(BF16), tile(BF16), tile(F32), tile(BF16), column, column, acc]
    else:
        kernel = functools.partial(_fox_prompt_kernel, tq=tq, nk=nq)
        ck_spec = pl.BlockSpec((None, 2 * nq, tq), lambda b, p, i: (b * pairs + p, 0, 0))
        in_specs, args, name = [q_spec, kv_spec, kv_spec, ck_spec], (q, k, v, ck), "fox_prompt"
        scratch = [pltpu.VMEM((2, 8, LANES), F32), tile(F32), tile(BF16), column, column, column, acc]
    return pl.pallas_call(
        kernel,
        grid=(batch, pairs, nq),
        in_specs=in_specs,
        out_specs=q_spec,
        out_shape=jax.ShapeDtypeStruct((n, width), BF16),
        scratch_shapes=scratch,
        compiler_params=_params("parallel", "parallel", "arbitrary"),
        name=name,
    )(*args)


def _ab_sample_kernel(fq_ref, fk_ref, fv_ref, cfk_ref, cfv_ref, ckc_ref, ckn_ref,
                      sq_ref, sk_ref, sv_ref, csk_ref, csv_ref, fo_ref, so_ref, *, t, tkc, nkc, tks, nks):
    row = lax.broadcasted_iota(jnp.int32, (t, t), 0)
    col = lax.broadcasted_iota(jnp.int32, (t, t), 1)

    qs = _split_heads(fq_ref[...])
    kmax = _max_key_norms(lambda off, n: cfk_ref[pl.ds(off, n), :], nkc * tkc, tks)
    reach = [_fox_reach(qs[h], kmax[h]) for h in range(2)]
    ck_of_block = lambda h, j: ckc_ref[pl.ds(h * nkc + j, 1), :]

    def fox_cache(j, states):
        off = pl.multiple_of(j * tkc, tkc)
        kj = cfk_ref[pl.ds(off, tkc), :].astype(BF16)
        vj = cfv_ref[pl.ds(off, tkc), :].astype(BF16)
        return tuple(
            _fox_update(states[h], qs[h], kj, vj, ck_of_block(h, j), None) for h in range(2))

    states = tuple(
        _fox_update(_fox_init(t), qs[h], fk_ref[...], fv_ref[...], ckn_ref[h:h + 1, :], col <= row)
        for h in range(2))
    states = _newest_first(nkc - 1, states, fox_cache,
                           lambda j, st: _fox_may_matter(j, st, reach, ck_of_block))
    outs = [acc / l for (_, l, acc) in states]
    fo_ref[...] = _merge_heads(outs[0], outs[1]).astype(fo_ref.dtype)

    qs2 = _split_heads(sq_ref[...])
    tri_new = _strict_lower(t)
    tri_cache = _strict_lower(tks)
    sstates = tuple(
        _sb_update(_sb_init(t), qs2[h], sk_ref[...], sv_ref[...], tri_new, col < row)
        for h in range(2))

    def sb_cache(j, st):
        off = pl.multiple_of(j * tks, tks)
        kj = csk_ref[pl.ds(off, tks), :].astype(BF16)
        vj = csv_ref[pl.ds(off, tks), :].astype(BF16)
        return tuple(_sb_update(st[h], qs2[h], kj, vj, tri_cache, None) for h in range(2))

    sstates = _newest_first(nks - 1, sstates, sb_cache, lambda j, st: _sb_may_matter(st))
    so_ref[...] = _merge_heads(sstates[0][1], sstates[1][1]).astype(so_ref.dtype)


def _flatten_heads_kernel(*refs):
    n = len(refs) // 2
    for x_ref, o_ref in zip(refs[:n], refs[n:]):
        o_ref[...] = x_ref[...].reshape(o_ref.shape).astype(o_ref.dtype)


def _flatten_heads(caches):
    rows, n_heads, dh = caches[0].shape
    tb = _row_tile(rows, 512)
    in_spec = pl.BlockSpec((tb, n_heads, dh), lambda i: (i, 0, 0))
    out_spec = pl.BlockSpec((tb, n_heads * dh), lambda i: (i, 0))
    return pl.pallas_call(
        _flatten_heads_kernel,
        grid=(rows // tb,),
        in_specs=[in_spec] * len(caches),
        out_specs=[out_spec] * len(caches),
        out_shape=[jax.ShapeDtypeStruct((rows, n_heads * dh), BF16)] * len(caches),
        compiler_params=_params("parallel"),
        name="flatten_cache",
    )(*caches)


def _ab_sample_attention(fq, fk, fv, cfk, cfv, ck_cache, ck_new, sq, sk, sv, csk, csv,
                         batch, t, past, tkc, tks):
    n, width = fq.shape
    pairs = width // LANES
    nkc = past // tkc
    new_spec = pl.BlockSpec((t, LANES), lambda b, p: (b, p))
    cache_spec = pl.BlockSpec((past, LANES), lambda b, p: (b, p))
    ckc_spec = pl.BlockSpec((None, 2 * nkc, tkc), lambda b, p: (b * pairs + p, 0, 0))
    ckn_spec = pl.BlockSpec((None, 2, t), lambda b, p: (b * pairs + p, 0, 0))
    out = jax.ShapeDtypeStruct((n, width), BF16)
    return pl.pallas_call(
        functools.partial(_ab_sample_kernel, t=t, tkc=tkc, nkc=nkc, tks=tks, nks=past // tks),
        grid=(batch, pairs),
        in_specs=[new_spec, new_spec, new_spec, cache_spec, cache_spec, ckc_spec, ckn_spec,
                  new_spec, new_spec, new_spec, cache_spec, cache_spec],
        out_specs=[new_spec, new_spec],
        out_shape=[out, out],
        compiler_params=_params("parallel", "parallel"),
        name="ab_sample",
    )(fq, fk, fv, cfk, cfv, ck_cache, ck_new, sq, sk, sv, csk, csv)


FF_CHUNK = 256


def _tail_kernel(*refs, n_in, final_norm):
    h_ref, p_ref = refs[0], refs[1 + n_in]
    a_refs, wo_refs = refs[1:1 + n_in], refs[2 + n_in:2 + 2 * n_in]
    g1_ref, wg_ref, wu_ref, wd_ref, g2_ref, wpg_ref, wpp_ref, gf_ref, o_ref, act_ref = refs[2 + 2 * n_in:]
    h1 = h_ref[...]
    for a_ref, w_ref in zip(a_refs, wo_refs):
        h1 = h1 + _dot(a_ref[...], w_ref[...])
    o_ref[...] = h1
    xn = _rms(h1, g1_ref[...]).astype(BF16)
    for c in range(wg_ref.shape[1] // FF_CHUNK):
        cols = slice(c * FF_CHUNK, (c + 1) * FF_CHUNK)
        gate = _dot(xn, wg_ref[:, cols])
        up = _dot(xn, wu_ref[:, cols])
        act_ref[:, cols] = (gate * jax.nn.sigmoid(gate) * up).astype(BF16)
    h2 = o_ref[...] + _dot(act_ref[...], wd_ref[...])
    gate = jax.nn.sigmoid(_dot(_rms(h2, g2_ref[...]).astype(BF16), wpg_ref[...]))
    h3 = h2 + _dot(p_ref[...].astype(BF16), wpp_ref[...]) * gate
    o_ref[...] = _rms(h3, gf_ref[...]) if final_norm else h3


def _layer_tail(h, acts, p_all, layer, w_outs, ffn, pe, g_final, final_norm):
    n, d = h.shape
    g1, w_gate, w_up, w_down = ffn
    g2, w_pg, w_pp = pe
    assert w_gate.shape[1] % FF_CHUNK == 0
    tm = _row_tile(n, 512)
    row = lambda i: (i, 0)
    const = lambda i: (0, 0)
    resident = lambda w: pl.BlockSpec(w.shape, const, pipeline_mode=pl.Buffered(1))
    vec = pl.BlockSpec((1, d), const)
    in_specs = [pl.BlockSpec((tm, d), row)]
    in_specs += [pl.BlockSpec((tm, a.shape[1]), row) for a in acts]
    in_specs += [pl.BlockSpec((None, tm, p_all.shape[2]), lambda i: (layer, i, 0))]
    in_specs += [resident(w) for w in w_outs]
    in_specs += [vec, resident(w_gate), resident(w_up), resident(w_down), vec, resident(w_pg),
                 resident(w_pp), vec]
    return pl.pallas_call(
        functools.partial(_tail_kernel, n_in=len(acts), final_norm=final_norm),
        grid=(n // tm,),
        in_specs=in_specs,
        out_specs=pl.BlockSpec((tm, d), row),
        out_shape=jax.ShapeDtypeStruct((n, d), F32),
        scratch_shapes=[pltpu.VMEM((tm, w_gate.shape[1]), BF16)],
        compiler_params=_params("parallel"),
        name="tail_final" if final_norm else "tail",
    )(h, *acts, p_all, *w_outs, g1, w_gate, w_up, w_down, g2, w_pg, w_pp, g_final)


def _rope(y, cos, sin_lo, sin_hi):
    out = []
    for c in range(y.shape[1] // LANES):
        yc = y[:, c * LANES:(c + 1) * LANES]
        out.append(yc * cos + pltpu.roll(yc, 8, 1) * sin_hi + pltpu.roll(yc, LANES - 8, 1) * sin_lo)
    return out[0] if len(out) == 1 else jnp.concatenate(out, axis=1)


def _proj_c_kernel(x_ref, g_ref, w_ref, cos_ref, slo_ref, shi_ref,
                   k_ref, v_ref, qb_ref, kd_ref, vd_ref, *, q_w, kv_w):
    xn = _rms(x_ref[...], g_ref[...]).astype(BF16)
    tabs = (cos_ref[...], slo_ref[...], shi_ref[...])
    o = 0
    qb_ref[...] = (_rope(_dot(xn, w_ref[:, o:o + q_w]), *tabs) * SCALE).astype(BF16)
    o += q_w
    k_ref[...] = _rope(_dot(xn, w_ref[:, o:o + kv_w]), *tabs)
    o += kv_w
    v_ref[...] = _dot(xn, w_ref[:, o:o + kv_w])
    o += kv_w
    kd_ref[...] = _rope(_dot(xn, w_ref[:, o:o + 2 * kv_w]), *tabs).astype(BF16)
    o += 2 * kv_w
    vd_ref[...] = _dot(xn, w_ref[:, o:o + 2 * kv_w]).astype(BF16)


def _proj_c(x, g, w_all, tables, period, q_w, kv_w):
    n, d = x.shape
    tm = _row_tile(period, 512)
    nper = period // tm
    row = lambda i: (i, 0)
    const = lambda i: (0, 0)
    tab = pl.BlockSpec((tm, LANES), lambda i: (i % nper, 0))
    return pl.pallas_call(
        functools.partial(_proj_c_kernel, q_w=q_w, kv_w=kv_w),
        grid=(n // tm,),
        in_specs=[pl.BlockSpec((tm, d), row), pl.BlockSpec((1, d), const),
                  pl.BlockSpec(w_all.shape, const), tab, tab, tab],
        out_specs=[pl.BlockSpec((tm, kv_w), row), pl.BlockSpec((tm, kv_w), row),
                   pl.BlockSpec((tm, q_w), row), pl.BlockSpec((tm, 2 * kv_w), row),
                   pl.BlockSpec((tm, 2 * kv_w), row)],
        out_shape=[jax.ShapeDtypeStruct((n, kv_w), F32), jax.ShapeDtypeStruct((n, kv_w), F32),
                   jax.ShapeDtypeStruct((n, q_w), BF16), jax.ShapeDtypeStruct((n, 2 * kv_w), BF16),
                   jax.ShapeDtypeStruct((n, 2 * kv_w), BF16)],
        compiler_params=_params("parallel"),
        name="proj_c",
    )(x, g, w_all, *tables)


def _rope_tables(pos):
    half = ROPE_DIM // 2
    inv_freq = ROPE_THETA ** (-2.0 * jnp.arange(half, dtype=F32) / ROPE_DIM)
    ang = pos.astype(F32)[:, None] * inv_freq[None, :]
    cos, sin = jnp.cos(ang), jnp.sin(ang)
    n = pos.shape[0]
    rest = HEAD_DIM - ROPE_DIM
    cos_h = jnp.concatenate([cos, cos, jnp.ones((n, rest), F32)], axis=1)
    lo_h = jnp.concatenate([-sin, jnp.zeros((n, half + rest), F32)], axis=1)
    hi_h = jnp.concatenate([jnp.zeros((n, half), F32), sin, jnp.zeros((n, rest), F32)], axis=1)
    two = lambda a: jnp.concatenate([a, a], axis=1)
    return two(cos_h), two(lo_h), two(hi_h)


def _swa_kernel(sink_ref, q_ref, kprev_ref, kcur_ref, vprev_ref, vcur_ref, o_ref, kb_ref, vb_ref,
                *, tq, n_kv, group, first_pos_of_tile):
    band = WINDOW + CHUNK
    kb_ref[0:WINDOW, :] = kprev_ref[...]
    kb_ref[WINDOW:, :] = kcur_ref[...]
    vb_ref[0:WINDOW, :] = vprev_ref[...]
    vb_ref[WINDOW:, :] = vcur_ref[...]
    rows = group * CHUNK
    ridx = lax.broadcasted_iota(jnp.int32, (rows, 1), 0) // CHUNK
    col = lax.broadcasted_iota(jnp.int32, (rows, band), 1)
    tile_pos = None if first_pos_of_tile is None else first_pos_of_tile(pl.program_id(1))

    def chunk(cc, _):
        base = pl.multiple_of(cc * CHUNK, CHUNK)
        for kvh in range(n_kv):
            lhs = []
            for pr in range(group // 2):
                lane0 = (kvh * (group // 2) + pr) * LANES
                lhs.extend(_split_heads(q_ref[pl.ds(base, CHUNK), lane0:lane0 + LANES]))
            s = _dot_t(jnp.concatenate(lhs, axis=0), kb_ref[pl.ds(base, band), kvh * LANES:(kvh + 1) * LANES])
            if tile_pos is not None:
                s = jnp.where(tile_pos - WINDOW + cc * CHUNK + col >= 0, s, -jnp.inf)
            sink = jnp.zeros((rows, 1), F32)
            for gi in range(group):
                sink = jnp.where(ridx == gi, sink_ref[kvh * group + gi], sink)
            m = jnp.maximum(jnp.max(s, axis=-1, keepdims=True), sink)
            e = jnp.exp(s - m)
            den = jnp.sum(e, axis=-1, keepdims=True) + jnp.exp(sink - m)
            o = _dot(e.astype(BF16), vb_ref[pl.ds(base, band), kvh * LANES:(kvh + 1) * LANES]) / den
            for pr in range(group // 2):
                lane0 = (kvh * (group // 2) + pr) * LANES
                pair = _merge_heads(o[(2 * pr) * CHUNK:(2 * pr + 1) * CHUNK],
                                    o[(2 * pr + 1) * CHUNK:(2 * pr + 2) * CHUNK])
                o_ref[pl.ds(base, CHUNK), lane0:lane0 + LANES] = pair.astype(o_ref.dtype)
        return 0

    lax.fori_loop(0, tq // CHUNK, chunk, 0)


def _swa_prompt(sinks, q, kd, vd, batch, seq, n_kv):
    n, q_w = q.shape
    group = q_w // HEAD_DIM // n_kv
    tq = _row_tile(seq, 512)
    nt = seq // tq
    per_win = tq // WINDOW
    cur = lambda b, i: (b * nt + i, 0)
    prev = lambda b, i: (jnp.maximum(b * nt * per_win + i * per_win - 1, b * nt * per_win), 0)
    kv_w = kd.shape[1]
    return pl.pallas_call(
        functools.partial(_swa_kernel, tq=tq, n_kv=n_kv, group=group,
                          first_pos_of_tile=lambda i: i * tq),
        grid=(batch, nt),
        in_specs=[pl.BlockSpec(memory_space=pltpu.SMEM),
                  pl.BlockSpec((tq, q_w), cur),
                  pl.BlockSpec((WINDOW, kv_w), prev), pl.BlockSpec((tq, kv_w), cur),
                  pl.BlockSpec((WINDOW, kv_w), prev), pl.BlockSpec((tq, kv_w), cur)],
        out_specs=pl.BlockSpec((tq, q_w), cur),
        out_shape=jax.ShapeDtypeStruct((n, q_w), BF16),
        scratch_shapes=[pltpu.VMEM((WINDOW + tq, kv_w), BF16), pltpu.VMEM((WINDOW + tq, kv_w), BF16)],
        compiler_params=_params("parallel", "parallel"),
        name="swa_prompt",
    )(sinks, q, kd, kd, vd, vd)


def _swa_sample(sinks, q, kd_cache, kd_new, vd_cache, vd_new, batch, t, n_kv):
    n, q_w = q.shape
    group = q_w // HEAD_DIM // n_kv
    kv_w = kd_new.shape[1]
    blk = lambda b, i: (b, 0)
    return pl.pallas_call(
        functools.partial(_swa_kernel, tq=t, n_kv=n_kv, group=group, first_pos_of_tile=None),
        grid=(batch, 1),
        in_specs=[pl.BlockSpec(memory_space=pltpu.SMEM),
                  pl.BlockSpec((t, q_w), blk),
                  pl.BlockSpec((WINDOW, kv_w), blk), pl.BlockSpec((t, kv_w), blk),
                  pl.BlockSpec((WINDOW, kv_w), blk), pl.BlockSpec((t, kv_w), blk)],
        out_specs=pl.BlockSpec((t, q_w), blk),
        out_shape=jax.ShapeDtypeStruct((n, q_w), BF16),
        scratch_shapes=[pltpu.VMEM((WINDOW + t, kv_w), BF16), pltpu.VMEM((WINDOW + t, kv_w), BF16)],
        compiler_params=_params("parallel", "parallel"),
        name="swa_sample",
    )(sinks, q, kd_cache, kd_new, vd_cache, vd_new)


def _dup_heads(a, n_heads):
    lead = a.shape[:-1]
    a = a.reshape(*lead, n_heads, 1, HEAD_DIM)
    return jnp.broadcast_to(a, (*lead, n_heads, 2, HEAD_DIM)).reshape(*lead, n_heads * LANES)


def _prep_weights(W, n_fox, n_sb, n_kv):
    fox_w = n_fox * HEAD_DIM
    w_in = W["w_ab_in"][0]
    w_main = jnp.concatenate([w_in[:, :3 * fox_w], w_in[:, 3 * fox_w + n_fox:]], axis=1).astype(BF16)
    w_flog = jnp.pad(w_in[:, 3 * fox_w:3 * fox_w + n_fox], ((0, 0), (0, LANES - n_fox))).astype(BF16)
    b_f = jnp.pad(W["b_fox_f"][0][None, :], ((0, 0), (0, LANES - n_fox)))
    w_c = W["w_c_in"][0]
    kv_w = n_kv * HEAD_DIM
    q_w = w_c.shape[1] - 2 * kv_w
    w_k, w_v = w_c[:, q_w:q_w + kv_w], w_c[:, q_w + kv_w:]
    w_c_all = jnp.concatenate([w_c, _dup_heads(w_k, n_kv), _dup_heads(w_v, n_kv)], axis=1).astype(BF16)
    bf = lambda name, i: W[name][i].astype(BF16)
    out = dict(w_main=w_main, w_flog=w_flog, b_f=b_f, w_c_all=w_c_all, q_w=q_w, kv_w=kv_w,
               w_ab_out_f=W["w_ab_out"][0][:fox_w].astype(BF16),
               w_ab_out_s=W["w_ab_out"][0][fox_w:].astype(BF16),
               w_c_out=bf("w_c_out", 0), c_sinks=W["c_sinks"][0])
    for i in range(2):
        out[f"ffn{i}"] = (W["norm_ffn"][i][None, :], bf("w_ffn_gate", i), bf("w_ffn_up", i),
                          bf("w_ffn_down", i))
        out[f"pe{i}"] = (W["norm_pe"][i][None, :], bf("w_pe_gate", i), bf("w_pe_proj", i))
    out["norm_mix"] = [W["norm_mix"][i][None, :] for i in range(2)]
    out["norm_final"] = W["norm_final"][None, :]
    return out


def _pad_lanes(a):
    pad = (-a.shape[1]) % LANES
    return jnp.pad(a, ((0, 0), (0, pad))) if pad else a


def _tail(h, acts, w_outs, p, P, i):
    p_all = p.reshape(p.shape[0], h.shape[0], p.shape[-1])
    return _layer_tail(h, acts, p_all, i, w_outs, P[f"ffn{i}"], P[f"pe{i}"], P["norm_final"],
                       final_norm=(i == p.shape[0] - 1))


def _trunk_prompt(x, p, P, n_fox, n_kv):
    batch, seq, d = x.shape
    n = batch * seq
    h = x.reshape(n, d)
    fk, fv, sk, sv, logf, fqb, fkb, fvb, sqb, skb, svb = _proj_ab(
        h, P["norm_mix"][0], P["w_main"], P["w_flog"], P["b_f"], n_fox)
    tq_fox = _row_tile(seq, FOX_TILE)
    lf_rows = jnp.swapaxes(logf.reshape(batch, seq, n_fox), 1, 2).reshape(batch * n_fox, seq)
    ck = _cumsum_rows(lf_rows).reshape(batch * n_fox // 2, 2 * (seq // tq_fox), tq_fox)
    fo = _prompt_attention(fqb, fkb, fvb, ck, batch, seq, tq_fox)
    so = _prompt_attention(sqb, skb, svb, None, batch, seq, _row_tile(seq, SB_TILE))
    h = _tail(h, [fo, so], [P["w_ab_out_f"], P["w_ab_out_s"]], p, P, 0)
    tables = _rope_tables(jnp.arange(seq))
    k, v, qb, kd, vd = _proj_c(h, P["norm_mix"][1], P["w_c_all"], tables, seq, P["q_w"], P["kv_w"])
    o = _swa_prompt(P["c_sinks"], qb, kd, vd, batch, seq, n_kv)
    y = _tail(h, [o], [P["w_c_out"]], p, P, 1)
    heads = lambda a, nh: a.reshape(1, batch, seq, nh, HEAD_DIM)
    kw = k.reshape(batch, seq, n_kv, HEAD_DIM)[None, :, seq - WINDOW:]
    vw = v.reshape(batch, seq, n_kv, HEAD_DIM)[None, :, seq - WINDOW:]
    return y.reshape(batch, seq, d), (heads(fk, n_fox), heads(fv, n_fox),
                                      logf.reshape(1, batch, seq, n_fox),
                                      heads(sk, n_fox), heads(sv, n_fox), kw, vw)


def _trunk_sample(x, p, P, caches, n_fox, n_kv):
    c_fk, c_fv, c_fl, c_sk, c_sv, c_wk, c_wv = caches
    batch, t, d = x.shape
    past = c_fk.shape[2]
    n = batch * t
    h = x.reshape(n, d)
    width = n_fox * HEAD_DIM
    fk, fv, sk, sv, logf, fqb, fkb, fvb, sqb, skb, svb = _proj_ab(
        h, P["norm_mix"][0], P["w_main"], P["w_flog"], P["b_f"], n_fox)
    lf_all = jnp.concatenate([c_fl[0], logf.reshape(batch, t, n_fox)], axis=1)
    lf_rows = _pad_lanes(jnp.swapaxes(lf_all, 1, 2).reshape(batch * n_fox, past + t))
    c_rows = _cumsum_rows(lf_rows)
    tkc = _row_tile(past, 1024)
    ck_cache = c_rows[:, :past].reshape(batch * n_fox // 2, 2 * (past // tkc), tkc)
    ck_new = c_rows[:, past:past + t].reshape(batch * n_fox // 2, 2, t)
    cfk, cfv, csk, csv = _flatten_heads(
        [c.reshape(batch * past, n_fox, HEAD_DIM) for c in (c_fk, c_fv, c_sk, c_sv)])
    fo, so = _ab_sample_attention(fqb, fkb, fvb, cfk, cfv, ck_cache, ck_new,
                                  sqb, skb, svb, csk, csv, batch, t, past, tkc, _row_tile(past, 256))
    h = _tail(h, [fo, so], [P["w_ab_out_f"], P["w_ab_out_s"]], p, P, 0)
    tm = _row_tile(n, 512)
    pos = past + jnp.arange(t)
    tables = _rope_tables(jnp.tile(pos, max(tm // t, 1)))
    k, v, qb, kd, vd = _proj_c(h, P["norm_mix"][1], P["w_c_all"], tables, tables[0].shape[0],
                               P["q_w"], P["kv_w"])
    kd_cache = _dup_heads(c_wk[0].reshape(batch * WINDOW, n_kv * HEAD_DIM), n_kv).astype(BF16)
    vd_cache = _dup_heads(c_wv[0].reshape(batch * WINDOW, n_kv * HEAD_DIM), n_kv).astype(BF16)
    o = _swa_sample(P["c_sinks"], qb, kd_cache, kd, vd_cache, vd, batch, t, n_kv)
    y = _tail(h, [o], [P["w_c_out"]], p, P, 1)
    heads = lambda a, nh: a.reshape(1, batch, t, nh, HEAD_DIM)
    roll = lambda c, new: jnp.concatenate([c[0], new.reshape(batch, t, n_kv, HEAD_DIM)], axis=1)[None, :, -WINDOW:]
    return y.reshape(batch, t, d), (heads(fk, n_fox), heads(fv, n_fox),
                                    logf.reshape(1, batch, t, n_fox),
                                    heads(sk, n_fox), heads(sv, n_fox), roll(c_wk, k), roll(c_wv, v))


def kernel(x_prompt, x_sample, cache_fox_k, cache_fox_v, cache_fox_logf, cache_sb_k, cache_sb_v, cache_swa_k, cache_swa_v, p_prompt, p_sample, norm_mix, w_ab_in, b_fox_f, w_ab_out, w_c_in, c_sinks, w_c_out, norm_ffn, w_ffn_gate, w_ffn_up, w_ffn_down, norm_pe, w_pe_gate, w_pe_proj, norm_final):
    W = dict(norm_mix=norm_mix, w_ab_in=w_ab_in, b_fox_f=b_fox_f, w_ab_out=w_ab_out, w_c_in=w_c_in,
             c_sinks=c_sinks, w_c_out=w_c_out, norm_ffn=norm_ffn, w_ffn_gate=w_ffn_gate,
             w_ffn_up=w_ffn_up, w_ffn_down=w_ffn_down, norm_pe=norm_pe, w_pe_gate=w_pe_gate,
             w_pe_proj=w_pe_proj, norm_final=norm_final)
    n_fox = cache_fox_k.shape[3]
    n_sb = cache_sb_k.shape[3]
    n_kv = cache_swa_k.shape[3]
    assert n_fox == n_sb and n_fox % 2 == 0 and n_kv % 1 == 0
    assert cache_swa_k.shape[2] == WINDOW and x_sample.shape[1] == CHUNK
    P = _prep_weights(W, n_fox, n_sb, n_kv)
    caches = (cache_fox_k, cache_fox_v, cache_fox_logf, cache_sb_k, cache_sb_v, cache_swa_k, cache_swa_v)
    y_p, st_p = _trunk_prompt(x_prompt, p_prompt, P, n_fox, n_kv)
    y_s, st_s = _trunk_sample(x_sample, p_sample, P, caches, n_fox, n_kv)
    return (y_p, y_s, *st_p, *st_s)
```

```python
import functools

import jax
import jax.numpy as jnp
from jax import lax
from jax.experimental import pallas as pl
from jax.experimental.pallas import tpu as pltpu

F32 = jnp.float32
BF16 = jnp.bfloat16

HEAD_DIM = 64
CHUNK = 64
WINDOW = 128
ROPE_DIM = 16
ROPE_THETA = 500000.0
RMS_EPS = 1e-6
SCALE = HEAD_DIM ** -0.5
SKIP_LOGIT = -110.0
REACH_SLACK = 1.01
LOG2_E = 1.4426950408889634
LN_2 = 0.6931471805599453

LANES = 128
V7X_VMEM_BYTES = 64 * 1024 * 1024
VMEM_LIMIT = V7X_VMEM_BYTES - 8 * 1024 * 1024


def _params(*sem):
    return pltpu.CompilerParams(dimension_semantics=sem, vmem_limit_bytes=VMEM_LIMIT)


def _row_tile(n, pref):
    t = pref
    while n % t:
        t //= 2
    return t


def _rms(x, g):
    ms = jnp.mean(x * x, axis=-1, keepdims=True)
    return x * lax.rsqrt(ms + RMS_EPS) * g


def _dot(a, b):
    return jnp.dot(a, b, preferred_element_type=F32)


def _dot_t(a, b):
    return lax.dot_general(a, b, (((1,), (1,)), ((), ())), preferred_element_type=F32)


def _softplus(z):
    return jnp.maximum(z, 0.0) + jnp.log(1.0 + jnp.exp(-jnp.abs(z)))


def _split_heads(q):
    first = lax.broadcasted_iota(jnp.int32, q.shape, 1) < HEAD_DIM
    zero = jnp.zeros_like(q)
    return jnp.where(first, q, zero), jnp.where(first, zero, q)


def _merge_heads(a, b):
    first = lax.broadcasted_iota(jnp.int32, a.shape, 1) < HEAD_DIM
    return jnp.where(first, a, b)


def _proj_ab_kernel(x_ref, g_ref, w_ref, wf_ref, bf_ref,
                    fk_ref, fv_ref, sk_ref, sv_ref, lf_ref,
                    fqb_ref, fkb_ref, fvb_ref, sqb_ref, skb_ref, svb_ref, *, width):
    xn = _rms(x_ref[...], g_ref[...]).astype(BF16)

    def col(c):
        return _dot(xn, w_ref[:, c * width:(c + 1) * width])

    fqb_ref[...] = (col(0) * SCALE).astype(BF16)
    y = col(1)
    fk_ref[...] = y.reshape(fk_ref.shape)
    fkb_ref[...] = y.astype(BF16)
    y = col(2)
    fv_ref[...] = y.reshape(fv_ref.shape)
    fvb_ref[...] = y.astype(BF16)
    sqb_ref[...] = (col(3) * SCALE).astype(BF16)
    y = col(4)
    sk_ref[...] = y.reshape(sk_ref.shape)
    skb_ref[...] = y.astype(BF16)
    y = col(5)
    sv_ref[...] = y.reshape(sv_ref.shape)
    svb_ref[...] = y.astype(BF16)
    u = _dot(xn, wf_ref[...]) + bf_ref[...]
    lf = -_softplus(-u)
    lf_ref[...] = lf[:, :lf_ref.shape[1]]


def _proj_ab(x, g, w_main, w_f, b_f, n_heads):
    n, d = x.shape
    width = n_heads * HEAD_DIM
    tm = _row_tile(n, 512)
    row = lambda i: (i, 0)
    const = lambda i: (0, 0)
    f32_out = jax.ShapeDtypeStruct((n, n_heads, HEAD_DIM), F32)
    hblk = pl.BlockSpec((tm, n_heads, HEAD_DIM), lambda i: (i, 0, 0))
    bf_out = jax.ShapeDtypeStruct((n, width), BF16)
    blk = pl.BlockSpec((tm, width), row)
    return pl.pallas_call(
        functools.partial(_proj_ab_kernel, width=width),
        grid=(n // tm,),
        in_specs=[pl.BlockSpec((tm, d), row), pl.BlockSpec((1, d), const),
                  pl.BlockSpec(w_main.shape, const), pl.BlockSpec(w_f.shape, const),
                  pl.BlockSpec(b_f.shape, const)],
        out_specs=[hblk, hblk, hblk, hblk, pl.BlockSpec((tm, n_heads), row)] + [blk] * 6,
        out_shape=[f32_out] * 4 + [jax.ShapeDtypeStruct((n, n_heads), F32)] + [bf_out] * 6,
        compiler_params=_params("parallel"),
        name="proj_ab",
    )(x, g, w_main, w_f, b_f)


def _proj_ab_t_kernel(x_ref, g_ref, w_ref, wt_ref, bf_ref,
                      fkt_ref, fvt_ref, skt_ref, svt_ref, lft_ref,
                      fqb_ref, fvb_ref, sqb_ref, svb_ref, fktb_ref, sktb_ref, *, width, n_heads):
    xn = _rms(x_ref[...], g_ref[...]).astype(BF16)
    for c, (ref, scale) in enumerate(((fqb_ref, SCALE), (fvb_ref, 1.0), (sqb_ref, SCALE), (svb_ref, 1.0))):
        ref[...] = (_dot(xn, w_ref[:, c * width:(c + 1) * width]) * scale).astype(BF16)
    for c, (ref, bf_ref_) in enumerate(((fkt_ref, fktb_ref), (fvt_ref, None), (skt_ref, sktb_ref),
                                        (svt_ref, None))):
        yt = _dot_t(wt_ref[c * width:(c + 1) * width, :], xn)
        ref[...] = yt
        if bf_ref_ is not None:
            bf_ref_[...] = yt.astype(BF16)
    ut = _dot_t(wt_ref[4 * width:, :], xn) + bf_ref[...]
    lft_ref[...] = (-_softplus(-ut))[:n_heads]


def _proj_ab_t(x, g, w_nt, w_t, b_col, n_heads, batch, seq):
    n, d = x.shape
    width = n_heads * HEAD_DIM
    tm = _row_tile(seq, 512)
    per = seq // tm
    row = lambda i: (i, 0)
    const = lambda i: (0, 0)
    t_blk = pl.BlockSpec((None, width, tm), lambda i: (i // per, 0, i % per))
    blk = pl.BlockSpec((tm, width), row)
    t32 = jax.ShapeDtypeStruct((batch, width, seq), F32)
    tbf = jax.ShapeDtypeStruct((batch, width, seq), BF16)
    bf_out = jax.ShapeDtypeStruct((n, width), BF16)
    return pl.pallas_call(
        functools.partial(_proj_ab_t_kernel, width=width, n_heads=n_heads),
        grid=(n // tm,),
        in_specs=[pl.BlockSpec((tm, d), row), pl.BlockSpec((1, d), const),
                  pl.BlockSpec(w_nt.shape, const), pl.BlockSpec(w_t.shape, const),
                  pl.BlockSpec(b_col.shape, const)],
        out_specs=[t_blk] * 4 + [pl.BlockSpec((None, n_heads, tm), lambda i: (i // per, 0, i % per))]
        + [blk] * 4 + [t_blk] * 2,
        out_shape=[t32] * 4 + [jax.ShapeDtypeStruct((batch, n_heads, seq), F32)] + [bf_out] * 4 + [tbf] * 2,
        compiler_params=_params("parallel"),
        name="proj_ab_t",
    )(x, g, w_nt, w_t, b_col)


def _cumsum_kernel(x_ref, o_ref, *, n_chunks):
    r = lax.broadcasted_iota(jnp.int32, (LANES, LANES), 0)
    c = lax.broadcasted_iota(jnp.int32, (LANES, LANES), 1)
    upper = (r <= c).astype(BF16)

    def body(t, carry):
        off = pl.multiple_of(t * LANES, LANES)
        x = x_ref[:, pl.ds(off, LANES)]
        x1 = x.astype(BF16)
        r1 = x - x1.astype(F32)
        x2 = r1.astype(BF16)
        x3 = (r1 - x2.astype(F32)).astype(BF16)
        y = _dot(x1, upper) + _dot(x2, upper) + _dot(x3, upper) + carry
        o_ref[:, pl.ds(off, LANES)] = y
        return y[:, LANES - 1:LANES]

    lax.fori_loop(0, n_chunks, body, jnp.zeros((x_ref.shape[0], 1), F32))


def _cumsum_rows(x):
    r, l = x.shape
    tr = _row_tile(r, 16)
    return pl.pallas_call(
        functools.partial(_cumsum_kernel, n_chunks=l // LANES),
        grid=(r // tr,),
        in_specs=[pl.BlockSpec((tr, l), lambda i: (i, 0))],
        out_specs=pl.BlockSpec((tr, l), lambda i: (i, 0)),
        out_shape=jax.ShapeDtypeStruct((r, l), F32),
        compiler_params=_params("parallel"),
        name="cumsum_logf",
    )(x)


def _qk(qh, kj, transposed):
    return _dot(qh, kj) if transposed else _dot_t(qh, kj)


def _pv(p, vj, transposed):
    return _dot_t(p, vj) if transposed else _dot(p, vj)


def _fox_update(state, qh, kj, vj, ck_row, valid, transposed=False):
    m, l, acc = state
    s = _qk(qh, kj, transposed) - ck_row
    if valid is not None:
        s = jnp.where(valid, s, -jnp.inf)
    m_new = jnp.maximum(m, jnp.max(s, axis=-1, keepdims=True))
    alpha = jnp.exp(m - m_new)
    p = jnp.exp(s - m_new)
    l = alpha * l + jnp.sum(p, axis=-1, keepdims=True)
    acc = alpha * acc + _pv(p.astype(BF16), vj, transposed)
    return m_new, l, acc


def _fox_init(tq):
    return (jnp.full((tq, 1), -jnp.inf, F32), jnp.zeros((tq, 1), F32), jnp.zeros((tq, LANES), F32))


def _sb_update(state, qh, kj, vj, tri, valid, transposed=False):
    carry, acc = state
    z = _qk(qh, kj, transposed)
    log_1m = -_softplus(z)
    if valid is not None:
        log_1m = jnp.where(valid, log_1m, 0.0)
    hi = log_1m.astype(BF16)
    lo = (log_1m - hi.astype(F32)).astype(BF16)
    between = _dot(hi, tri) + _dot(lo, tri)
    a = jnp.exp(z + log_1m + between + carry)
    if valid is not None:
        a = jnp.where(valid, a, 0.0)
    acc = acc + _pv(a.astype(BF16), vj, transposed)
    carry = carry + jnp.sum(log_1m, axis=-1, keepdims=True)
    return carry, acc


def _sb_init(tq):
    return (jnp.zeros((tq, 1), F32), jnp.zeros((tq, LANES), F32))


def _strict_lower(t):
    r = lax.broadcasted_iota(jnp.int32, (t, t), 0)
    c = lax.broadcasted_iota(jnp.int32, (t, t), 1)
    return (r > c).astype(BF16)


def _newest_first(j0, states, step, may_matter):
    def cond(c):
        return jnp.logical_and(c[0] >= 0, c[1])

    def body(c):
        j, _, st = c
        st = step(j, st)
        return j - 1, may_matter(j - 1, st), st

    return lax.while_loop(cond, body, (j0, may_matter(j0, states), states))[2]


def _fox_reach(q_half, kmax):
    qf = q_half.astype(F32)
    return jnp.sqrt(jnp.sum(qf * qf, axis=-1, keepdims=True)) * kmax * REACH_SLACK


def _fox_may_matter(j, states, reach, ck_of_block):
    jc = jnp.maximum(j, 0)
    worst = None
    for h in range(2):
        ck_last = ck_of_block(h, jc)[:, -1:]
        gap = reach[h] - ck_last - states[h][0]
        worst = gap if worst is None else jnp.maximum(worst, gap)
    return jnp.max(worst) >= SKIP_LOGIT


def _sb_may_matter(states):
    return jnp.max(jnp.maximum(states[0][0], states[1][0])) >= SKIP_LOGIT


def _max_key_norms(load_keys, n_keys, chunk):
    def body(t, best):
        kf = load_keys(pl.multiple_of(t * chunk, chunk), chunk).astype(BF16).astype(F32)
        sq = kf * kf
        n0 = jnp.sum(sq[:HEAD_DIM], axis=0, keepdims=True)
        n1 = jnp.sum(sq[HEAD_DIM:], axis=0, keepdims=True)
        return jnp.maximum(best[0], n0), jnp.maximum(best[1], n1)

    zero = jnp.zeros((1, chunk), F32)
    best = lax.fori_loop(0, n_keys // chunk, body, (zero, zero))
    return [jnp.sqrt(jnp.max(b, axis=1, keepdims=True)) for b in best]


STRIP = 16
FOX_TILE = 512
SB_TILE = 256


def _strips(tq, tk, diag):
    for r in range(tq // STRIP):
        hi = (r + 1) * STRIP
        ncol = min(tk, -(-hi // LANES) * LANES) if diag else tk
        yield r * STRIP, slice(r * STRIP, hi), ncol


def _group_mask(row0, g, strict):
    lo = g * LANES
    if lo + LANES - 1 <= row0 - (1 if strict else 0):
        return None
    rr = row0 + lax.broadcasted_iota(jnp.int32, (STRIP, LANES), 0)
    cc = lo + lax.broadcasted_iota(jnp.int32, (STRIP, LANES), 1)
    return cc < rr if strict else cc <= rr


def _row_bcast(x):
    return jnp.broadcast_to(x, (x.shape[0], LANES))


HEADS = (0, 1)


def _fox_block(qs, kj, vj, ck_rows, diag, scr):
    s_scr, p_scr, m_scr, l_scr, a_scr, acc_scr = scr
    tq, tk = s_scr.shape[1:]
    for h in HEADS:
        s_scr[h] = _dot(qs[h], kj)

    def logits(h, row0, rows, ncol):
        out = []
        for g in range(ncol // LANES):
            cols = slice(g * LANES, (g + 1) * LANES)
            sg = s_scr[h, rows, cols] - ck_rows[h][:, cols]
            mask = _group_mask(row0, g, strict=False) if diag else None
            out.append(sg if mask is None else jnp.where(mask, sg, -jnp.inf))
        return out

    for h in HEADS:
        for row0, rows, ncol in _strips(tq, tk, diag):
            part = functools.reduce(jnp.maximum, logits(h, row0, rows, ncol))
            m_new = _row_bcast(jnp.max(part, axis=-1, keepdims=True))
            if not diag:
                m_old = m_scr[h, rows, :]
                m_new = jnp.maximum(m_old, m_new)
                a_scr[h, rows, :] = jnp.exp(m_old - m_new)
            m_scr[h, rows, :] = m_new
    for h in HEADS:
        for row0, rows, ncol in _strips(tq, tk, diag):
            m_new = m_scr[h, rows, :]
            pg = [jnp.exp(sg - m_new) for sg in logits(h, row0, rows, ncol)]
            lsum = functools.reduce(jnp.add, pg)
            if diag:
                l_scr[h, rows, :] = lsum
            else:
                alpha = a_scr[h, rows, :]
                l_scr[h, rows, :] = alpha * l_scr[h, rows, :] + lsum
                acc_scr[h, rows, :] = alpha * acc_scr[h, rows, :]
            pg = [p.astype(BF16) for p in pg]
            pg += [jnp.zeros((STRIP, LANES), BF16)] * ((tk - ncol) // LANES)
            p_scr[h, rows, :] = jnp.concatenate(pg, axis=1)
    for h in HEADS:
        pv = _dot(p_scr[h], vj)
        acc_scr[h] = pv if diag else acc_scr[h] + pv


def _log_1m_sigmoid(z):
    t = jnp.log2(1.0 + jnp.exp2(jnp.abs(z) * (-LOG2_E)))
    return t * (-LN_2) - jnp.maximum(z, 0.0)


def _sb_block(qs, kj, vj, tri, diag, scr):
    z_scr, hi_scr, lo_scr, b_scr, p_scr, c_scr, d_scr, acc_scr = scr
    tq, tk = z_scr.shape[1:]
    zero = jnp.zeros((STRIP, LANES), BF16)
    for h in HEADS:
        z_scr[h] = _dot(qs[h], kj)
    for h in HEADS:
        for row0, rows, ncol in _strips(tq, tk, diag):
            his, los, dsum = [], [], None
            for g in range(ncol // LANES):
                cols = slice(g * LANES, (g + 1) * LANES)
                z = z_scr[h, rows, cols]
                log_1m = _log_1m_sigmoid(z)
                mask = _group_mask(row0, g, strict=True) if diag else None
                if mask is not None:
                    log_1m = jnp.where(mask, log_1m, 0.0)
                hi = log_1m.astype(BF16)
                his.append(hi)
                los.append((log_1m - hi.astype(F32)).astype(BF16))
                z_scr[h, rows, cols] = z + log_1m
                dsum = log_1m if dsum is None else dsum + log_1m
            pad = [zero] * ((tk - ncol) // LANES)
            hi_scr[h, rows, :] = jnp.concatenate(his + pad, axis=1)
            lo_scr[h, rows, :] = jnp.concatenate(los + pad, axis=1)
            d_scr[h, rows, :] = dsum
    for h in HEADS:
        b_scr[h] = _dot(hi_scr[h], tri) + _dot(lo_scr[h], tri)
    for h in HEADS:
        for row0, rows, ncol in _strips(tq, tk, diag):
            carry = None if diag else c_scr[h, rows, :]
            pg = []
            for g in range(ncol // LANES):
                cols = slice(g * LANES, (g + 1) * LANES)
                e = z_scr[h, rows, cols] + b_scr[h, rows, cols]
                a = jnp.exp(e if carry is None else e + carry)
                mask = _group_mask(row0, g, strict=True) if diag else None
                pg.append((a if mask is None else jnp.where(mask, a, 0.0)).astype(BF16))
            p_scr[h, rows, :] = jnp.concatenate(pg + [zero] * ((tk - ncol) // LANES), axis=1)
    for h in HEADS:
        pv = _dot(p_scr[h], vj)
        acc_scr[h] = pv if diag else acc_scr[h] + pv
        d = _row_bcast(jnp.sum(d_scr[h], axis=-1, keepdims=True))
        c_scr[h] = d if diag else c_scr[h] + d


def _fox_prompt_kernel(q_ref, k_ref, v_ref, ck_ref, o_ref, kmax_ref, *scr, tq, nk):
    i = pl.program_id(2)

    @pl.when(i == 0)
    def _():
        norms = _max_key_norms(lambda off, n: k_ref[:, pl.ds(off, n)], nk * tq, tq)
        for h in range(2):
            kmax_ref[h] = jnp.broadcast_to(norms[h], kmax_ref.shape[1:])

    qs = _split_heads(q_ref[...])
    reach = [_fox_reach(qs[h], kmax_ref[h, 0:1, 0:1]) for h in range(2)]
    ck_of_block = lambda h, j: ck_ref[pl.ds(h * nk + j, 1), :]
    m_scr = scr[2]

    def step(j, carry, diag=False):
        off = pl.multiple_of(j * tq, tq)
        kj = k_ref[:, pl.ds(off, tq)]
        vj = v_ref[pl.ds(off, tq), :]
        _fox_block(qs, kj, vj, [ck_of_block(h, j) for h in HEADS], diag, scr)
        return carry

    def may_matter(j, carry):
        return _fox_may_matter(j, [(m_scr[h],) for h in range(2)], reach, ck_of_block)

    step(i, 0, diag=True)
    _newest_first(i - 1, 0, step, may_matter)
    l_scr, acc_scr = scr[3], scr[5]
    outs = [acc_scr[h] / jnp.sum(l_scr[h], axis=-1, keepdims=True) for h in range(2)]
    o_ref[...] = _merge_heads(outs[0], outs[1]).astype(o_ref.dtype)


def _sb_prompt_kernel(q_ref, k_ref, v_ref, o_ref, *scr, tq):
    i = pl.program_id(2)
    qs = _split_heads(q_ref[...])
    tri = _strict_lower(tq)
    c_scr, acc_scr = scr[5], scr[7]

    def step(j, carry, diag=False):
        off = pl.multiple_of(j * tq, tq)
        kj = k_ref[:, pl.ds(off, tq)]
        vj = v_ref[pl.ds(off, tq), :]
        _sb_block(qs, kj, vj, tri, diag, scr)
        return carry

    step(i, 0, diag=True)
    _newest_first(i - 1, 0, step, lambda j, carry: _sb_may_matter([(c_scr[0],), (c_scr[1],)]))
    o_ref[...] = _merge_heads(acc_scr[0], acc_scr[1]).astype(o_ref.dtype)


def _prompt_attention(q, k, v, ck, batch, seq, tq):
    n, width = q.shape
    pairs = width // LANES
    nq = seq // tq
    q_spec = pl.BlockSpec((tq, LANES), lambda b, p, i: (b * nq + i, p))
    k_spec = pl.BlockSpec((None, LANES, seq), lambda b, p, i: (b, p, 0))
    v_spec = pl.BlockSpec((seq, LANES), lambda b, p, i: (b, p))
    tile = lambda dt: pltpu.VMEM((2, tq, tq), dt)
    acc = pltpu.VMEM((2, tq, LANES), F32)
    column = acc
    if ck is None:
        kernel = functools.partial(_sb_prompt_kernel, tq=tq)
        in_specs, args, name = [q_spec, k_spec, v_spec], (q, k, v), "sb_prompt"
        scratch = [tile(F32), tile(BF16), tile(BF16), tile(F32), tile(BF16), column, column, acc]
    else:
        kernel = functools.partial(_fox_prompt_kernel, tq=tq, nk=nq)
        ck_spec = pl.BlockSpec((None, 2 * nq, tq), lambda b, p, i: (b * pairs + p, 0, 0))
        in_specs, args, name = [q_spec, k_spec, v_spec, ck_spec], (q, k, v, ck), "fox_prompt"
        scratch = [pltpu.VMEM((2, 8, LANES), F32), tile(F32), tile(BF16), column, column, column, acc]
    return pl.pallas_call(
        kernel,
        grid=(batch, pairs, nq),
        in_specs=in_specs,
        out_specs=q_spec,
        out_shape=jax.ShapeDtypeStruct((n, width), BF16),
        scratch_shapes=scratch,
        compiler_params=_params("parallel", "parallel", "arbitrary"),
        name=name,
    )(*args)


def _ab_sample_kernel(fq_ref, fk_ref, fv_ref, cfk_ref, cfv_ref, ckc_ref, ckn_ref,
                      sq_ref, sk_ref, sv_ref, csk_ref, csv_ref, fo_ref, so_ref, *, t, tkc, nkc, tks, nks):
    row = lax.broadcasted_iota(jnp.int32, (t, t), 0)
    col = lax.broadcasted_iota(jnp.int32, (t, t), 1)

    qs = _split_heads(fq_ref[...])
    kmax = _max_key_norms(lambda off, n: cfk_ref[:, pl.ds(off, n)], nkc * tkc, tks)
    reach = [_fox_reach(qs[h], kmax[h]) for h in range(2)]
    ck_of_block = lambda h, j: ckc_ref[pl.ds(h * nkc + j, 1), :]

    def fox_cache(j, states):
        off = pl.multiple_of(j * tkc, tkc)
        kj = cfk_ref[:, pl.ds(off, tkc)].astype(BF16)
        vj = cfv_ref[:, pl.ds(off, tkc)].astype(BF16)
        return tuple(
            _fox_update(states[h], qs[h], kj, vj, ck_of_block(h, j), None, transposed=True)
            for h in range(2))

    states = tuple(
        _fox_update(_fox_init(t), qs[h], fk_ref[...], fv_ref[...], ckn_ref[h:h + 1, :], col <= row)
        for h in range(2))
    states = _newest_first(nkc - 1, states, fox_cache,
                           lambda j, st: _fox_may_matter(j, st, reach, ck_of_block))
    outs = [acc / l for (_, l, acc) in states]
    fo_ref[...] = _merge_heads(outs[0], outs[1]).astype(fo_ref.dtype)

    qs2 = _split_heads(sq_ref[...])
    tri_new = _strict_lower(t)
    tri_cache = _strict_lower(tks)
    sstates = tuple(
        _sb_update(_sb_init(t), qs2[h], sk_ref[...], sv_ref[...], tri_new, col < row)
        for h in range(2))

    def sb_cache(j, st):
        off = pl.multiple_of(j * tks, tks)
        kj = csk_ref[:, pl.ds(off, tks)].astype(BF16)
        vj = csv_ref[:, pl.ds(off, tks)].astype(BF16)
        return tuple(_sb_update(st[h], qs2[h], kj, vj, tri_cache, None, transposed=True)
                     for h in range(2))

    sstates = _newest_first(nks - 1, sstates, sb_cache, lambda j, st: _sb_may_matter(st))
    so_ref[...] = _merge_heads(sstates[0][1], sstates[1][1]).astype(so_ref.dtype)


def _ab_sample_attention(fq, fk, fv, cfk, cfv, ck_cache, ck_new, sq, sk, sv, csk, csv,
                         batch, t, past, tkc, tks):
    n, width = fq.shape
    pairs = width // LANES
    nkc = past // tkc
    new_spec = pl.BlockSpec((t, LANES), lambda b, p: (b, p))
    cache_spec = pl.BlockSpec((None, LANES, past), lambda b, p: (b, p, 0))
    ckc_spec = pl.BlockSpec((None, 2 * nkc, tkc), lambda b, p: (b * pairs + p, 0, 0))
    ckn_spec = pl.BlockSpec((None, 2, t), lambda b, p: (b * pairs + p, 0, 0))
    out = jax.ShapeDtypeStruct((n, width), BF16)
    return pl.pallas_call(
        functools.partial(_ab_sample_kernel, t=t, tkc=tkc, nkc=nkc, tks=tks, nks=past // tks),
        grid=(batch, pairs),
        in_specs=[new_spec, new_spec, new_spec, cache_spec, cache_spec, ckc_spec, ckn_spec,
                  new_spec, new_spec, new_spec, cache_spec, cache_spec],
        out_specs=[new_spec, new_spec],
        out_shape=[out, out],
        compiler_params=_params("parallel", "parallel"),
        name="ab_sample",
    )(fq, fk, fv, cfk, cfv, ck_cache, ck_new, sq, sk, sv, csk, csv)


FF_CHUNK = 256


def _tail_kernel(*refs, n_in, final_norm):
    h_ref, p_ref = refs[0], refs[1 + n_in]
    a_refs, wo_refs = refs[1:1 + n_in], refs[2 + n_in:2 + 2 * n_in]
    g1_ref, wg_ref, wu_ref, wd_ref, g2_ref, wpg_ref, wpp_ref, gf_ref, o_ref, act_ref = refs[2 + 2 * n_in:]
    h1 = h_ref[...]
    for a_ref, w_ref in zip(a_refs, wo_refs):
        h1 = h1 + _dot(a_ref[...], w_ref[...])
    o_ref[...] = h1
    xn = _rms(h1, g1_ref[...]).astype(BF16)
    for c in range(wg_ref.shape[1] // FF_CHUNK):
        cols = slice(c * FF_CHUNK, (c + 1) * FF_CHUNK)
        gate = _dot(xn, wg_ref[:, cols])
        up = _dot(xn, wu_ref[:, cols])
        act_ref[:, cols] = (gate * jax.nn.sigmoid(gate) * up).astype(BF16)
    h2 = o_ref[...] + _dot(act_ref[...], wd_ref[...])
    gate = jax.nn.sigmoid(_dot(_rms(h2, g2_ref[...]).astype(BF16), wpg_ref[...]))
    h3 = h2 + _dot(p_ref[...].astype(BF16), wpp_ref[...]) * gate
    o_ref[...] = _rms(h3, gf_ref[...]) if final_norm else h3


def _layer_tail(h, acts, p_all, layer, w_outs, ffn, pe, g_final, final_norm):
    n, d = h.shape
    g1, w_gate, w_up, w_down = ffn
    g2, w_pg, w_pp = pe
    assert w_gate.shape[1] % FF_CHUNK == 0
    tm = _row_tile(n, 512)
    row = lambda i: (i, 0)
    const = lambda i: (0, 0)
    resident = lambda w: pl.BlockSpec(w.shape, const, pipeline_mode=pl.Buffered(1))
    vec = pl.BlockSpec((1, d), const)
    in_specs = [pl.BlockSpec((tm, d), row)]
    in_specs += [pl.BlockSpec((tm, a.shape[1]), row) for a in acts]
    in_specs += [pl.BlockSpec((None, tm, p_all.shape[2]), lambda i: (layer, i, 0))]
    in_specs += [resident(w) for w in w_outs]
    in_specs += [vec, resident(w_gate), resident(w_up), resident(w_down), vec, resident(w_pg),
                 resident(w_pp), vec]
    return pl.pallas_call(
        functools.partial(_tail_kernel, n_in=len(acts), final_norm=final_norm),
        grid=(n // tm,),
        in_specs=in_specs,
        out_specs=pl.BlockSpec((tm, d), row),
        out_shape=jax.ShapeDtypeStruct((n, d), F32),
        scratch_shapes=[pltpu.VMEM((tm, w_gate.shape[1]), BF16)],
        compiler_params=_params("parallel"),
        name="tail_final" if final_norm else "tail",
    )(h, *acts, p_all, *w_outs, g1, w_gate, w_up, w_down, g2, w_pg, w_pp, g_final)


def _rope(y, cos, sin_lo, sin_hi):
    out = []
    for c in range(y.shape[1] // LANES):
        yc = y[:, c * LANES:(c + 1) * LANES]
        out.append(yc * cos + pltpu.roll(yc, 8, 1) * sin_hi + pltpu.roll(yc, LANES - 8, 1) * sin_lo)
    return out[0] if len(out) == 1 else jnp.concatenate(out, axis=1)


def _proj_c_kernel(x_ref, g_ref, w_ref, cos_ref, slo_ref, shi_ref,
                   k_ref, v_ref, qb_ref, kd_ref, vd_ref, *, q_w, kv_w, window_only, nper):
    xn = _rms(x_ref[...], g_ref[...]).astype(BF16)
    tabs = (cos_ref[...], slo_ref[...], shi_ref[...])
    qb_ref[...] = (_rope(_dot(xn, w_ref[:, :q_w]), *tabs) * SCALE).astype(BF16)
    o = q_w + 2 * kv_w
    kd_ref[...] = _rope(_dot(xn, w_ref[:, o:o + 2 * kv_w]), *tabs).astype(BF16)
    o += 2 * kv_w
    vd_ref[...] = _dot(xn, w_ref[:, o:o + 2 * kv_w]).astype(BF16)

    def f32_state():
        k = _rope(_dot(xn, w_ref[:, q_w:q_w + kv_w]), *tabs)
        v = _dot(xn, w_ref[:, q_w + kv_w:q_w + 2 * kv_w])
        k_ref[...] = k[-k_ref.shape[0]:]
        v_ref[...] = v[-v_ref.shape[0]:]

    if window_only:
        pl.when(pl.program_id(0) % nper == nper - 1)(f32_state)
    else:
        f32_state()


def _proj_c(x, g, w_all, tables, period, q_w, kv_w, window_only):
    n, d = x.shape
    tm = _row_tile(period, 512)
    nper = period // tm
    row = lambda i: (i, 0)
    const = lambda i: (0, 0)
    tab = pl.BlockSpec((tm, LANES), lambda i: (i % nper, 0))
    if window_only:
        assert tm >= WINDOW
        state_spec = pl.BlockSpec((None, WINDOW, kv_w), lambda i: (i // nper, 0, 0))
        state_shape = jax.ShapeDtypeStruct((n // period, WINDOW, kv_w), F32)
    else:
        state_spec = pl.BlockSpec((tm, kv_w), row)
        state_shape = jax.ShapeDtypeStruct((n, kv_w), F32)
    return pl.pallas_call(
        functools.partial(_proj_c_kernel, q_w=q_w, kv_w=kv_w, window_only=window_only, nper=nper),
        grid=(n // tm,),
        in_specs=[pl.BlockSpec((tm, d), row), pl.BlockSpec((1, d), const),
                  pl.BlockSpec(w_all.shape, const), tab, tab, tab],
        out_specs=[state_spec, state_spec,
                   pl.BlockSpec((tm, q_w), row), pl.BlockSpec((tm, 2 * kv_w), row),
                   pl.BlockSpec((tm, 2 * kv_w), row)],
        out_shape=[state_shape, state_shape,
                   jax.ShapeDtypeStruct((n, q_w), BF16), jax.ShapeDtypeStruct((n, 2 * kv_w), BF16),
                   jax.ShapeDtypeStruct((n, 2 * kv_w), BF16)],
        compiler_params=_params("arbitrary"),
        name="proj_c",
    )(x, g, w_all, *tables)


def _rope_tables(pos):
    half = ROPE_DIM // 2
    inv_freq = ROPE_THETA ** (-2.0 * jnp.arange(half, dtype=F32) / ROPE_DIM)
    ang = pos.astype(F32)[:, None] * inv_freq[None, :]
    cos, sin = jnp.cos(ang), jnp.sin(ang)
    n = pos.shape[0]
    rest = HEAD_DIM - ROPE_DIM
    cos_h = jnp.concatenate([cos, cos, jnp.ones((n, rest), F32)], axis=1)
    lo_h = jnp.concatenate([-sin, jnp.zeros((n, half + rest), F32)], axis=1)
    hi_h = jnp.concatenate([jnp.zeros((n, half), F32), sin, jnp.zeros((n, rest), F32)], axis=1)
    two = lambda a: jnp.concatenate([a, a], axis=1)
    return two(cos_h), two(lo_h), two(hi_h)


def _swa_kernel(sink_ref, q_ref, kprev_ref, kcur_ref, vprev_ref, vcur_ref, o_ref, kb_ref, vb_ref,
                *, tq, n_kv, group, first_pos_of_tile):
    band = WINDOW + CHUNK
    kb_ref[0:WINDOW, :] = kprev_ref[...]
    kb_ref[WINDOW:, :] = kcur_ref[...]
    vb_ref[0:WINDOW, :] = vprev_ref[...]
    vb_ref[WINDOW:, :] = vcur_ref[...]
    rows = group * CHUNK
    ridx = lax.broadcasted_iota(jnp.int32, (rows, 1), 0) // CHUNK
    col = lax.broadcasted_iota(jnp.int32, (rows, band), 1)
    tile_pos = None if first_pos_of_tile is None else first_pos_of_tile(pl.program_id(1))

    def chunk(cc, _):
        base = pl.multiple_of(cc * CHUNK, CHUNK)
        for kvh in range(n_kv):
            lhs = []
            for pr in range(group // 2):
                lane0 = (kvh * (group // 2) + pr) * LANES
                lhs.extend(_split_heads(q_ref[pl.ds(base, CHUNK), lane0:lane0 + LANES]))
            s = _dot_t(jnp.concatenate(lhs, axis=0), kb_ref[pl.ds(base, band), kvh * LANES:(kvh + 1) * LANES])
            if tile_pos is not None:
                s = jnp.where(tile_pos - WINDOW + cc * CHUNK + col >= 0, s, -jnp.inf)
            sink = jnp.zeros((rows, 1), F32)
            for gi in range(group):
                sink = jnp.where(ridx == gi, sink_ref[kvh * group + gi], sink)
            m = jnp.maximum(jnp.max(s, axis=-1, keepdims=True), sink)
            e = jnp.exp(s - m)
            den = jnp.sum(e, axis=-1, keepdims=True) + jnp.exp(sink - m)
            o = _dot(e.astype(BF16), vb_ref[pl.ds(base, band), kvh * LANES:(kvh + 1) * LANES]) / den
            for pr in range(group // 2):
                lane0 = (kvh * (group // 2) + pr) * LANES
                pair = _merge_heads(o[(2 * pr) * CHUNK:(2 * pr + 1) * CHUNK],
                                    o[(2 * pr + 1) * CHUNK:(2 * pr + 2) * CHUNK])
                o_ref[pl.ds(base, CHUNK), lane0:lane0 + LANES] = pair.astype(o_ref.dtype)
        return 0

    lax.fori_loop(0, tq // CHUNK, chunk, 0)


def _swa_prompt(sinks, q, kd, vd, batch, seq, n_kv):
    n, q_w = q.shape
    group = q_w // HEAD_DIM // n_kv
    tq = _row_tile(seq, 512)
    nt = seq // tq
    per_win = tq // WINDOW
    cur = lambda b, i: (b * nt + i, 0)
    prev = lambda b, i: (jnp.maximum(b * nt * per_win + i * per_win - 1, b * nt * per_win), 0)
    kv_w = kd.shape[1]
    return pl.pallas_call(
        functools.partial(_swa_kernel, tq=tq, n_kv=n_kv, group=group,
                          first_pos_of_tile=lambda i: i * tq),
        grid=(batch, nt),
        in_specs=[pl.BlockSpec(memory_space=pltpu.SMEM),
                  pl.BlockSpec((tq, q_w), cur),
                  pl.BlockSpec((WINDOW, kv_w), prev), pl.BlockSpec((tq, kv_w), cur),
                  pl.BlockSpec((WINDOW, kv_w), prev), pl.BlockSpec((tq, kv_w), cur)],
        out_specs=pl.BlockSpec((tq, q_w), cur),
        out_shape=jax.ShapeDtypeStruct((n, q_w), BF16),
        scratch_shapes=[pltpu.VMEM((WINDOW + tq, kv_w), BF16), pltpu.VMEM((WINDOW + tq, kv_w), BF16)],
        compiler_params=_params("parallel", "parallel"),
        name="swa_prompt",
    )(sinks, q, kd, kd, vd, vd)


def _swa_sample(sinks, q, kd_cache, kd_new, vd_cache, vd_new, batch, t, n_kv):
    n, q_w = q.shape
    group = q_w // HEAD_DIM // n_kv
    kv_w = kd_new.shape[1]
    blk = lambda b, i: (b, 0)
    return pl.pallas_call(
        functools.partial(_swa_kernel, tq=t, n_kv=n_kv, group=group, first_pos_of_tile=None),
        grid=(batch, 1),
        in_specs=[pl.BlockSpec(memory_space=pltpu.SMEM),
                  pl.BlockSpec((t, q_w), blk),
                  pl.BlockSpec((WINDOW, kv_w), blk), pl.BlockSpec((t, kv_w), blk),
                  pl.BlockSpec((WINDOW, kv_w), blk), pl.BlockSpec((t, kv_w), blk)],
        out_specs=pl.BlockSpec((t, q_w), blk),
        out_shape=jax.ShapeDtypeStruct((n, q_w), BF16),
        scratch_shapes=[pltpu.VMEM((WINDOW + t, kv_w), BF16), pltpu.VMEM((WINDOW + t, kv_w), BF16)],
        compiler_params=_params("parallel", "parallel"),
        name="swa_sample",
    )(sinks, q, kd_cache, kd_new, vd_cache, vd_new)


def _dup_heads(a, n_heads):
    lead = a.shape[:-1]
    a = a.reshape(*lead, n_heads, 1, HEAD_DIM)
    return jnp.broadcast_to(a, (*lead, n_heads, 2, HEAD_DIM)).reshape(*lead, n_heads * LANES)


def _prep_weights(W, n_fox, n_sb, n_kv):
    fox_w = n_fox * HEAD_DIM
    w_in = W["w_ab_in"][0]
    w_main = jnp.concatenate([w_in[:, :3 * fox_w], w_in[:, 3 * fox_w + n_fox:]], axis=1).astype(BF16)
    w_flog = jnp.pad(w_in[:, 3 * fox_w:3 * fox_w + n_fox], ((0, 0), (0, LANES - n_fox))).astype(BF16)
    b_f = jnp.pad(W["b_fox_f"][0][None, :], ((0, 0), (0, LANES - n_fox)))
    fq, fk, fv, flog = (w_in[:, :fox_w], w_in[:, fox_w:2 * fox_w], w_in[:, 2 * fox_w:3 * fox_w],
                        w_in[:, 3 * fox_w:3 * fox_w + n_fox])
    sq, sk, sv = jnp.split(w_in[:, 3 * fox_w + n_fox:], 3, axis=1)
    gate_rows = 16
    w_nt = jnp.concatenate([fq, fv, sq, sv], axis=1).astype(BF16)
    w_t = jnp.concatenate([fk, fv, sk, sv, jnp.pad(flog, ((0, 0), (0, gate_rows - n_fox)))],
                          axis=1).T.astype(BF16)
    b_col = jnp.pad(W["b_fox_f"][0], (0, gate_rows - n_fox))[:, None]
    w_c = W["w_c_in"][0]
    kv_w = n_kv * HEAD_DIM
    q_w = w_c.shape[1] - 2 * kv_w
    w_k, w_v = w_c[:, q_w:q_w + kv_w], w_c[:, q_w + kv_w:]
    w_c_all = jnp.concatenate([w_c, _dup_heads(w_k, n_kv), _dup_heads(w_v, n_kv)], axis=1).astype(BF16)
    bf = lambda name, i: W[name][i].astype(BF16)
    out = dict(w_main=w_main, w_flog=w_flog, b_f=b_f, w_nt=w_nt, w_t=w_t, b_col=b_col,
               w_c_all=w_c_all, q_w=q_w, kv_w=kv_w,
               w_ab_out_f=W["w_ab_out"][0][:fox_w].astype(BF16),
               w_ab_out_s=W["w_ab_out"][0][fox_w:].astype(BF16),
               w_c_out=bf("w_c_out", 0), c_sinks=W["c_sinks"][0])
    for i in range(2):
        out[f"ffn{i}"] = (W["norm_ffn"][i][None, :], bf("w_ffn_gate", i), bf("w_ffn_up", i),
                          bf("w_ffn_down", i))
        out[f"pe{i}"] = (W["norm_pe"][i][None, :], bf("w_pe_gate", i), bf("w_pe_proj", i))
    out["norm_mix"] = [W["norm_mix"][i][None, :] for i in range(2)]
    out["norm_final"] = W["norm_final"][None, :]
    return out


def _pad_lanes(a):
    pad = (-a.shape[1]) % LANES
    return jnp.pad(a, ((0, 0), (0, pad))) if pad else a


def _tail(h, acts, w_outs, p, P, i):
    p_all = p.reshape(p.shape[0], h.shape[0], p.shape[-1])
    return _layer_tail(h, acts, p_all, i, w_outs, P[f"ffn{i}"], P[f"pe{i}"], P["norm_final"],
                       final_norm=(i == p.shape[0] - 1))


def _trunk_prompt(x, p, P, n_fox, n_kv):
    batch, seq, d = x.shape
    n = batch * seq
    h = x.reshape(n, d)
    fkt, fvt, skt, svt, lft, fqb, fvb, sqb, svb, fktb, sktb = _proj_ab_t(
        h, P["norm_mix"][0], P["w_nt"], P["w_t"], P["b_col"], n_fox, batch, seq)
    tq_fox = _row_tile(seq, FOX_TILE)
    ck = _cumsum_rows(lft.reshape(batch * n_fox, seq))
    ck = ck.reshape(batch * n_fox // 2, 2 * (seq // tq_fox), tq_fox)
    fo = _prompt_attention(fqb, fktb, fvb, ck, batch, seq, tq_fox)
    so = _prompt_attention(sqb, sktb, svb, None, batch, seq, _row_tile(seq, SB_TILE))
    h = _tail(h, [fo, so], [P["w_ab_out_f"], P["w_ab_out_s"]], p, P, 0)
    tables = _rope_tables(jnp.arange(seq))
    k, v, qb, kd, vd = _proj_c(h, P["norm_mix"][1], P["w_c_all"], tables, seq, P["q_w"], P["kv_w"],
                               window_only=True)
    o = _swa_prompt(P["c_sinks"], qb, kd, vd, batch, seq, n_kv)
    y = _tail(h, [o], [P["w_c_out"]], p, P, 1)
    heads = lambda a: jnp.transpose(a.reshape(batch, n_fox, HEAD_DIM, seq), (0, 3, 1, 2))[None]
    kw = k.reshape(1, batch, WINDOW, n_kv, HEAD_DIM)
    vw = v.reshape(1, batch, WINDOW, n_kv, HEAD_DIM)
    return y.reshape(batch, seq, d), (heads(fkt), heads(fvt), jnp.transpose(lft, (0, 2, 1))[None],
                                      heads(skt), heads(svt), kw, vw)


def _trunk_sample(x, p, P, caches, n_fox, n_kv):
    c_fk, c_fv, c_fl, c_sk, c_sv, c_wk, c_wv = caches
    batch, t, d = x.shape
    past = c_fk.shape[2]
    n = batch * t
    h = x.reshape(n, d)
    width = n_fox * HEAD_DIM
    fk, fv, sk, sv, logf, fqb, fkb, fvb, sqb, skb, svb = _proj_ab(
        h, P["norm_mix"][0], P["w_main"], P["w_flog"], P["b_f"], n_fox)
    lf_all = jnp.concatenate([jnp.swapaxes(c_fl[0], 1, 2),
                              jnp.swapaxes(logf.reshape(batch, t, n_fox), 1, 2)], axis=2)
    lf_rows = _pad_lanes(lf_all.reshape(batch * n_fox, past + t))
    c_rows = _cumsum_rows(lf_rows)
    tkc = _row_tile(past, 1024)
    ck_cache = c_rows[:, :past].reshape(batch * n_fox // 2, 2 * (past // tkc), tkc)
    ck_new = c_rows[:, past:past + t].reshape(batch * n_fox // 2, 2, t)
    cfk, cfv, csk, csv = [jnp.transpose(c[0], (0, 2, 3, 1)).reshape(batch, width, past)
                          for c in (c_fk, c_fv, c_sk, c_sv)]
    fo, so = _ab_sample_attention(fqb, fkb, fvb, cfk, cfv, ck_cache, ck_new,
                                  sqb, skb, svb, csk, csv, batch, t, past, tkc, _row_tile(past, 256))
    h = _tail(h, [fo, so], [P["w_ab_out_f"], P["w_ab_out_s"]], p, P, 0)
    tm = _row_tile(n, 512)
    pos = past + jnp.arange(t)
    tables = _rope_tables(jnp.tile(pos, max(tm // t, 1)))
    k, v, qb, kd, vd = _proj_c(h, P["norm_mix"][1], P["w_c_all"], tables, tables[0].shape[0],
                               P["q_w"], P["kv_w"], window_only=False)
    kd_cache = _dup_heads(c_wk[0].reshape(batch * WINDOW, n_kv * HEAD_DIM), n_kv).astype(BF16)
    vd_cache = _dup_heads(c_wv[0].reshape(batch * WINDOW, n_kv * HEAD_DIM), n_kv).astype(BF16)
    o = _swa_sample(P["c_sinks"], qb, kd_cache, kd, vd_cache, vd, batch, t, n_kv)
    y = _tail(h, [o], [P["w_c_out"]], p, P, 1)
    heads = lambda a, nh: a.reshape(1, batch, t, nh, HEAD_DIM)
    roll = lambda c, new: jnp.concatenate([c[0], new.reshape(batch, t, n_kv, HEAD_DIM)], axis=1)[None, :, -WINDOW:]
    return y.reshape(batch, t, d), (heads(fk, n_fox), heads(fv, n_fox),
                                    logf.reshape(1, batch, t, n_fox),
                                    heads(sk, n_fox), heads(sv, n_fox), roll(c_wk, k), roll(c_wv, v))


def kernel(x_prompt, x_sample, cache_fox_k, cache_fox_v, cache_fox_logf, cache_sb_k, cache_sb_v, cache_swa_k, cache_swa_v, p_prompt, p_sample, norm_mix, w_ab_in, b_fox_f, w_ab_out, w_c_in, c_sinks, w_c_out, norm_ffn, w_ffn_gate, w_ffn_up, w_ffn_down, norm_pe, w_pe_gate, w_pe_proj, norm_final):
    W = dict(norm_mix=norm_mix, w_ab_in=w_ab_in, b_fox_f=b_fox_f, w_ab_out=w_ab_out, w_c_in=w_c_in,
             c_sinks=c_sinks, w_c_out=w_c_out, norm_ffn=norm_ffn, w_ffn_gate=w_ffn_gate,
             w_ffn_up=w_ffn_up, w_ffn_down=w_ffn_down, norm_pe=norm_pe, w_pe_gate=w_pe_gate,
             w_pe_proj=w_pe_proj, norm_final=norm_final)
    n_fox = cache_fox_k.shape[3]
    n_sb = cache_sb_k.shape[3]
    n_kv = cache_swa_k.shape[3]
    assert n_fox == n_sb and n_fox % 2 == 0 and n_kv % 1 == 0
    assert cache_swa_k.shape[2] == WINDOW and x_sample.shape[1] == CHUNK
    P = _prep_weights(W, n_fox, n_sb, n_kv)
    caches = (cache_fox_k, cache_fox_v, cache_fox_logf, cache_sb_k, cache_sb_v, cache_swa_k, cache_swa_v)
    y_p, st_p = _trunk_prompt(x_prompt, p_prompt, P, n_fox, n_kv)
    y_s, st_s = _trunk_sample(x_sample, p_sample, P, caches, n_fox, n_kv)
    return (y_p, y_s, *st_p, *st_s)
```

```python
import functools

import jax
import jax.numpy as jnp
from jax import lax
from jax.experimental import pallas as pl
from jax.experimental.pallas import tpu as pltpu

F32 = jnp.float32
BF16 = jnp.bfloat16

HEAD_DIM = 64
CHUNK = 64
WINDOW = 128
ROPE_DIM = 16
ROPE_THETA = 500000.0
RMS_EPS = 1e-6
SCALE = HEAD_DIM ** -0.5
SKIP_LOGIT = -110.0
REACH_SLACK = 1.01
LOG2_E = 1.4426950408889634
LN_2 = 0.6931471805599453

LANES = 128
V7X_VMEM_BYTES = 64 * 1024 * 1024
VMEM_LIMIT = V7X_VMEM_BYTES - 8 * 1024 * 1024


def _params(*sem):
    return pltpu.CompilerParams(dimension_semantics=sem, vmem_limit_bytes=VMEM_LIMIT)


def _row_tile(n, pref):
    t = pref
    while n % t:
        t //= 2
    return t


def _rms(x, g):
    ms = jnp.mean(x * x, axis=-1, keepdims=True)
    return x * lax.rsqrt(ms + RMS_EPS) * g


def _dot(a, b):
    return jnp.dot(a, b, preferred_element_type=F32)


def _dot_t(a, b):
    return lax.dot_general(a, b, (((1,), (1,)), ((), ())), preferred_element_type=F32)


def _softplus(z):
    return jnp.maximum(z, 0.0) + jnp.log(1.0 + jnp.exp(-jnp.abs(z)))


def _split_heads(q):
    first = lax.broadcasted_iota(jnp.int32, q.shape, 1) < HEAD_DIM
    zero = jnp.zeros_like(q)
    return jnp.where(first, q, zero), jnp.where(first, zero, q)


def _merge_heads(a, b):
    first = lax.broadcasted_iota(jnp.int32, a.shape, 1) < HEAD_DIM
    return jnp.where(first, a, b)


def _proj_ab_kernel(x_ref, g_ref, w_ref, wf_ref, bf_ref,
                    fk_ref, fv_ref, sk_ref, sv_ref, lf_ref,
                    fqb_ref, fkb_ref, fvb_ref, sqb_ref, skb_ref, svb_ref, *, width):
    xn = _rms(x_ref[...], g_ref[...]).astype(BF16)

    def col(c):
        return _dot(xn, w_ref[:, c * width:(c + 1) * width])

    fqb_ref[...] = (col(0) * SCALE).astype(BF16)
    y = col(1)
    fk_ref[...] = y.reshape(fk_ref.shape)
    fkb_ref[...] = y.astype(BF16)
    y = col(2)
    fv_ref[...] = y.reshape(fv_ref.shape)
    fvb_ref[...] = y.astype(BF16)
    sqb_ref[...] = (col(3) * SCALE).astype(BF16)
    y = col(4)
    sk_ref[...] = y.reshape(sk_ref.shape)
    skb_ref[...] = y.astype(BF16)
    y = col(5)
    sv_ref[...] = y.reshape(sv_ref.shape)
    svb_ref[...] = y.astype(BF16)
    u = _dot(xn, wf_ref[...]) + bf_ref[...]
    lf = -_softplus(-u)
    lf_ref[...] = lf[:, :lf_ref.shape[1]]


def _proj_ab(x, g, w_main, w_f, b_f, n_heads):
    n, d = x.shape
    width = n_heads * HEAD_DIM
    tm = _row_tile(n, 512)
    row = lambda i: (i, 0)
    const = lambda i: (0, 0)
    f32_out = jax.ShapeDtypeStruct((n, n_heads, HEAD_DIM), F32)
    hblk = pl.BlockSpec((tm, n_heads, HEAD_DIM), lambda i: (i, 0, 0))
    bf_out = jax.ShapeDtypeStruct((n, width), BF16)
    blk = pl.BlockSpec((tm, width), row)
    return pl.pallas_call(
        functools.partial(_proj_ab_kernel, width=width),
        grid=(n // tm,),
        in_specs=[pl.BlockSpec((tm, d), row), pl.BlockSpec((1, d), const),
                  pl.BlockSpec(w_main.shape, const), pl.BlockSpec(w_f.shape, const),
                  pl.BlockSpec(b_f.shape, const)],
        out_specs=[hblk, hblk, hblk, hblk, pl.BlockSpec((tm, n_heads), row)] + [blk] * 6,
        out_shape=[f32_out] * 4 + [jax.ShapeDtypeStruct((n, n_heads), F32)] + [bf_out] * 6,
        compiler_params=_params("parallel"),
        name="proj_ab",
    )(x, g, w_main, w_f, b_f)


def _proj_ab_t_kernel(x_ref, g_ref, w_ref, wt_ref, bf_ref,
                      fkt_ref, fvt_ref, skt_ref, svt_ref, lft_ref,
                      fqb_ref, fvb_ref, sqb_ref, svb_ref, fktb_ref, sktb_ref, *, width, n_heads):
    xn = _rms(x_ref[...], g_ref[...]).astype(BF16)
    for c, (ref, scale) in enumerate(((fqb_ref, SCALE), (fvb_ref, 1.0), (sqb_ref, SCALE), (svb_ref, 1.0))):
        ref[...] = (_dot(xn, w_ref[:, c * width:(c + 1) * width]) * scale).astype(BF16)
    for c, (ref, bf_ref_) in enumerate(((fkt_ref, fktb_ref), (fvt_ref, None), (skt_ref, sktb_ref),
                                        (svt_ref, None))):
        yt = _dot_t(wt_ref[c * width:(c + 1) * width, :], xn)
        ref[...] = yt
        if bf_ref_ is not None:
            bf_ref_[...] = yt.astype(BF16)
    ut = _dot_t(wt_ref[4 * width:, :], xn) + bf_ref[...]
    lft_ref[...] = (-_softplus(-ut))[:n_heads]


def _proj_ab_t(x, g, w_nt, w_t, b_col, n_heads, batch, seq):
    n, d = x.shape
    width = n_heads * HEAD_DIM
    tm = _row_tile(seq, 512)
    per = seq // tm
    row = lambda i: (i, 0)
    const = lambda i: (0, 0)
    t_blk = pl.BlockSpec((None, width, tm), lambda i: (i // per, 0, i % per))
    blk = pl.BlockSpec((tm, width), row)
    t32 = jax.ShapeDtypeStruct((batch, width, seq), F32)
    tbf = jax.ShapeDtypeStruct((batch, width, seq), BF16)
    bf_out = jax.ShapeDtypeStruct((n, width), BF16)
    return pl.pallas_call(
        functools.partial(_proj_ab_t_kernel, width=width, n_heads=n_heads),
        grid=(n // tm,),
        in_specs=[pl.BlockSpec((tm, d), row), pl.BlockSpec((1, d), const),
                  pl.BlockSpec(w_nt.shape, const), pl.BlockSpec(w_t.shape, const),
                  pl.BlockSpec(b_col.shape, const)],
        out_specs=[t_blk] * 4 + [pl.BlockSpec((None, n_heads, tm), lambda i: (i // per, 0, i % per))]
        + [blk] * 4 + [t_blk] * 2,
        out_shape=[t32] * 4 + [jax.ShapeDtypeStruct((batch, n_heads, seq), F32)] + [bf_out] * 4 + [tbf] * 2,
        compiler_params=_params("parallel"),
        name="proj_ab_t",
    )(x, g, w_nt, w_t, b_col)


def _cumsum_kernel(x_ref, o_ref, *, n_chunks):
    r = lax.broadcasted_iota(jnp.int32, (LANES, LANES), 0)
    c = lax.broadcasted_iota(jnp.int32, (LANES, LANES), 1)
    upper = (r <= c).astype(BF16)

    def body(t, carry):
        off = pl.multiple_of(t * LANES, LANES)
        x = x_ref[:, pl.ds(off, LANES)]
        x1 = x.astype(BF16)
        r1 = x - x1.astype(F32)
        x2 = r1.astype(BF16)
        x3 = (r1 - x2.astype(F32)).astype(BF16)
        y = _dot(x1, upper) + _dot(x2, upper) + _dot(x3, upper) + carry
        o_ref[:, pl.ds(off, LANES)] = y
        return y[:, LANES - 1:LANES]

    lax.fori_loop(0, n_chunks, body, jnp.zeros((x_ref.shape[0], 1), F32))


def _cumsum_rows(x):
    r, l = x.shape
    tr = _row_tile(r, 16)
    return pl.pallas_call(
        functools.partial(_cumsum_kernel, n_chunks=l // LANES),
        grid=(r // tr,),
        in_specs=[pl.BlockSpec((tr, l), lambda i: (i, 0))],
        out_specs=pl.BlockSpec((tr, l), lambda i: (i, 0)),
        out_shape=jax.ShapeDtypeStruct((r, l), F32),
        compiler_params=_params("parallel"),
        name="cumsum_logf",
    )(x)


def _qk(qh, kj, transposed):
    return _dot(qh, kj) if transposed else _dot_t(qh, kj)


def _pv(p, vj, transposed):
    return _dot_t(p, vj) if transposed else _dot(p, vj)


def _fox_update(state, qh, kj, vj, ck_row, valid, transposed=False):
    m, l, acc = state
    s = _qk(qh, kj, transposed) - ck_row
    if valid is not None:
        s = jnp.where(valid, s, -jnp.inf)
    m_new = jnp.maximum(m, jnp.max(s, axis=-1, keepdims=True))
    alpha = jnp.exp(m - m_new)
    p = jnp.exp(s - m_new)
    l = alpha * l + jnp.sum(p, axis=-1, keepdims=True)
    acc = alpha * acc + _pv(p.astype(BF16), vj, transposed)
    return m_new, l, acc


def _fox_init(tq):
    return (jnp.full((tq, 1), -jnp.inf, F32), jnp.zeros((tq, 1), F32), jnp.zeros((tq, LANES), F32))


def _sb_update(state, qh, kj, vj, tri, valid, transposed=False):
    carry, acc = state
    z = _qk(qh, kj, transposed)
    log_1m = -_softplus(z)
    if valid is not None:
        log_1m = jnp.where(valid, log_1m, 0.0)
    hi = log_1m.astype(BF16)
    lo = (log_1m - hi.astype(F32)).astype(BF16)
    between = _dot(hi, tri) + _dot(lo, tri)
    a = jnp.exp(z + log_1m + between + carry)
    if valid is not None:
        a = jnp.where(valid, a, 0.0)
    acc = acc + _pv(a.astype(BF16), vj, transposed)
    carry = carry + jnp.sum(log_1m, axis=-1, keepdims=True)
    return carry, acc


def _sb_init(tq):
    return (jnp.zeros((tq, 1), F32), jnp.zeros((tq, LANES), F32))


def _strict_lower(t):
    r = lax.broadcasted_iota(jnp.int32, (t, t), 0)
    c = lax.broadcasted_iota(jnp.int32, (t, t), 1)
    return (r > c).astype(BF16)


def _newest_first(j0, states, step, may_matter):
    def cond(c):
        return jnp.logical_and(c[0] >= 0, c[1])

    def body(c):
        j, _, st = c
        st = step(j, st)
        return j - 1, may_matter(j - 1, st), st

    return lax.while_loop(cond, body, (j0, may_matter(j0, states), states))[2]


def _fox_reach(q_half, kmax):
    qf = q_half.astype(F32)
    return jnp.sqrt(jnp.sum(qf * qf, axis=-1, keepdims=True)) * kmax * REACH_SLACK


def _fox_may_matter(j, states, reach, ck_of_block):
    jc = jnp.maximum(j, 0)
    worst = None
    for h in range(2):
        ck_last = ck_of_block(h, jc)[:, -1:]
        gap = reach[h] - ck_last - states[h][0]
        worst = gap if worst is None else jnp.maximum(worst, gap)
    return jnp.max(worst) >= SKIP_LOGIT


def _sb_may_matter(states):
    return jnp.max(jnp.maximum(states[0][0], states[1][0])) >= SKIP_LOGIT


def _max_key_norms(load_keys, n_keys, chunk):
    def body(t, best):
        kf = load_keys(pl.multiple_of(t * chunk, chunk), chunk).astype(BF16).astype(F32)
        sq = kf * kf
        n0 = jnp.sum(sq[:HEAD_DIM], axis=0, keepdims=True)
        n1 = jnp.sum(sq[HEAD_DIM:], axis=0, keepdims=True)
        return jnp.maximum(best[0], n0), jnp.maximum(best[1], n1)

    zero = jnp.zeros((1, chunk), F32)
    best = lax.fori_loop(0, n_keys // chunk, body, (zero, zero))
    return [jnp.sqrt(jnp.max(b, axis=1, keepdims=True)) for b in best]


STRIP = 16
FOX_TILE = 512
SB_TILE = 256
SB_STEP_PAIRS = 2


def _strips(tq, tk, diag):
    for r in range(tq // STRIP):
        hi = (r + 1) * STRIP
        ncol = min(tk, -(-hi // LANES) * LANES) if diag else tk
        yield r * STRIP, slice(r * STRIP, hi), ncol


def _group_mask(row0, g, strict):
    lo = g * LANES
    if lo + LANES - 1 <= row0 - (1 if strict else 0):
        return None
    rr = row0 + lax.broadcasted_iota(jnp.int32, (STRIP, LANES), 0)
    cc = lo + lax.broadcasted_iota(jnp.int32, (STRIP, LANES), 1)
    return cc < rr if strict else cc <= rr


def _row_bcast(x):
    return jnp.broadcast_to(x, (x.shape[0], LANES))


HEADS = (0, 1)


def _fox_block(qs, kj, vj, ck_rows, diag, scr):
    s_scr, p_scr, m_scr, l_scr, a_scr, acc_scr = scr
    tq, tk = s_scr.shape[1:]
    for h in HEADS:
        s_scr[h] = _dot(qs[h], kj)

    def logits(h, row0, rows, ncol):
        out = []
        for g in range(ncol // LANES):
            cols = slice(g * LANES, (g + 1) * LANES)
            sg = s_scr[h, rows, cols] - ck_rows[h][:, cols]
            mask = _group_mask(row0, g, strict=False) if diag else None
            if mask is not None:
                sg = jnp.where(mask, sg, -jnp.inf)
            s_scr[h, rows, cols] = sg
            out.append(sg)
        return out

    def finished_logits(h, rows, ncol):
        return [s_scr[h, rows, g * LANES:(g + 1) * LANES] for g in range(ncol // LANES)]

    for h in HEADS:
        for row0, rows, ncol in _strips(tq, tk, diag):
            part = functools.reduce(jnp.maximum, logits(h, row0, rows, ncol))
            m_new = _row_bcast(jnp.max(part, axis=-1, keepdims=True))
            if not diag:
                m_old = m_scr[h, rows, :]
                m_new = jnp.maximum(m_old, m_new)
                a_scr[h, rows, :] = jnp.exp(m_old - m_new)
            m_scr[h, rows, :] = m_new
    for h in HEADS:
        for row0, rows, ncol in _strips(tq, tk, diag):
            m_new = m_scr[h, rows, :]
            pg = [jnp.exp(sg - m_new) for sg in finished_logits(h, rows, ncol)]
            lsum = functools.reduce(jnp.add, pg)
            if diag:
                l_scr[h, rows, :] = lsum
            else:
                alpha = a_scr[h, rows, :]
                l_scr[h, rows, :] = alpha * l_scr[h, rows, :] + lsum
                acc_scr[h, rows, :] = alpha * acc_scr[h, rows, :]
            pg = [p.astype(BF16) for p in pg]
            pg += [jnp.zeros((STRIP, LANES), BF16)] * ((tk - ncol) // LANES)
            p_scr[h, rows, :] = jnp.concatenate(pg, axis=1)
    for h in HEADS:
        pv = _dot(p_scr[h], vj)
        acc_scr[h] = pv if diag else acc_scr[h] + pv


def _log_1m_sigmoid(z):
    t = jnp.log2(1.0 + jnp.exp2(jnp.abs(z) * (-LOG2_E)))
    return t * (-LN_2) - jnp.maximum(z, 0.0)


def _sb_block(qs, kjs, vjs, tri, diag, scr):
    z_scr, hi_scr, lo_scr, b_scr, p_scr, c_scr, d_scr, acc_scr = scr
    tq, tk = z_scr.shape[1:]
    zero = jnp.zeros((STRIP, LANES), BF16)
    heads = range(len(qs))
    for h in heads:
        z_scr[h] = _dot(qs[h], kjs[h // 2])
    for h in heads:
        for row0, rows, ncol in _strips(tq, tk, diag):
            his, los, dsum = [], [], None
            for g in range(ncol // LANES):
                cols = slice(g * LANES, (g + 1) * LANES)
                z = z_scr[h, rows, cols]
                log_1m = _log_1m_sigmoid(z)
                mask = _group_mask(row0, g, strict=True) if diag else None
                if mask is not None:
                    log_1m = jnp.where(mask, log_1m, 0.0)
                hi = log_1m.astype(BF16)
                his.append(hi)
                los.append((log_1m - hi.astype(F32)).astype(BF16))
                z_scr[h, rows, cols] = z + log_1m
                dsum = log_1m if dsum is None else dsum + log_1m
            pad = [zero] * ((tk - ncol) // LANES)
            hi_scr[h, rows, :] = jnp.concatenate(his + pad, axis=1)
            lo_scr[h, rows, :] = jnp.concatenate(los + pad, axis=1)
            d_scr[h, rows, :] = dsum
    for h in heads:
        b_scr[h] = _dot(hi_scr[h], tri) + _dot(lo_scr[h], tri)
    for h in heads:
        for row0, rows, ncol in _strips(tq, tk, diag):
            carry = None if diag else c_scr[h, rows, :]
            pg = []
            for g in range(ncol // LANES):
                cols = slice(g * LANES, (g + 1) * LANES)
                e = z_scr[h, rows, cols] + b_scr[h, rows, cols]
                a = jnp.exp(e if carry is None else e + carry)
                mask = _group_mask(row0, g, strict=True) if diag else None
                pg.append((a if mask is None else jnp.where(mask, a, 0.0)).astype(BF16))
            p_scr[h, rows, :] = jnp.concatenate(pg + [zero] * ((tk - ncol) // LANES), axis=1)
    for h in heads:
        pv = _dot(p_scr[h], vjs[h // 2])
        acc_scr[h] = pv if diag else acc_scr[h] + pv
        d = _row_bcast(jnp.sum(d_scr[h], axis=-1, keepdims=True))
        c_scr[h] = d if diag else c_scr[h] + d


def _fox_prompt_kernel(q_ref, k_ref, v_ref, ck_ref, o_ref, kmax_ref, *scr, tq, nk):
    i = pl.program_id(2)

    @pl.when(i == 0)
    def _():
        norms = _max_key_norms(lambda off, n: k_ref[:, pl.ds(off, n)], nk * tq, tq)
        for h in range(2):
            kmax_ref[h] = jnp.broadcast_to(norms[h], kmax_ref.shape[1:])

    qs = _split_heads(q_ref[...])
    reach = [_fox_reach(qs[h], kmax_ref[h, 0:1, 0:1]) for h in range(2)]
    ck_of_block = lambda h, j: ck_ref[pl.ds(h * nk + j, 1), :]
    m_scr = scr[2]

    def step(j, carry, diag=False):
        off = pl.multiple_of(j * tq, tq)
        kj = k_ref[:, pl.ds(off, tq)]
        vj = v_ref[pl.ds(off, tq), :]
        _fox_block(qs, kj, vj, [ck_of_block(h, j) for h in HEADS], diag, scr)
        return carry

    def may_matter(j, carry):
        return _fox_may_matter(j, [(m_scr[h],) for h in range(2)], reach, ck_of_block)

    step(i, 0, diag=True)
    _newest_first(i - 1, 0, step, may_matter)
    l_scr, acc_scr = scr[3], scr[5]
    outs = [acc_scr[h] / jnp.sum(l_scr[h], axis=-1, keepdims=True) for h in range(2)]
    o_ref[...] = _merge_heads(outs[0], outs[1]).astype(o_ref.dtype)


def _sb_prompt_kernel(q_ref, k_ref, v_ref, o_ref, *scr, tq, pairs):
    i = pl.program_id(2)
    lanes = lambda pr: slice(pr * LANES, (pr + 1) * LANES)
    qs = [qh for pr in range(pairs) for qh in _split_heads(q_ref[:, lanes(pr)])]
    tri = _strict_lower(tq)
    c_scr, acc_scr = scr[5], scr[7]

    def step(j, carry, diag=False):
        off = pl.multiple_of(j * tq, tq)
        kjs = [k_ref[lanes(pr), pl.ds(off, tq)] for pr in range(pairs)]
        vjs = [v_ref[pl.ds(off, tq), lanes(pr)] for pr in range(pairs)]
        _sb_block(qs, kjs, vjs, tri, diag, scr)
        return carry

    def may_matter(j, carry):
        worst = functools.reduce(jnp.maximum, [c_scr[h] for h in range(2 * pairs)])
        return jnp.max(worst) >= SKIP_LOGIT

    step(i, 0, diag=True)
    _newest_first(i - 1, 0, step, may_matter)
    for pr in range(pairs):
        o_ref[:, lanes(pr)] = _merge_heads(acc_scr[2 * pr], acc_scr[2 * pr + 1]).astype(o_ref.dtype)


def _prompt_attention(q, k, v, ck, batch, seq, tq):
    n, width = q.shape
    pairs = width // LANES
    nq = seq // tq
    step_pairs = SB_STEP_PAIRS if ck is None and pairs % SB_STEP_PAIRS == 0 else 1
    wide = step_pairs * LANES
    q_spec = pl.BlockSpec((tq, wide), lambda b, p, i: (b * nq + i, p))
    k_spec = pl.BlockSpec((None, wide, seq), lambda b, p, i: (b, p, 0))
    v_spec = pl.BlockSpec((seq, wide), lambda b, p, i: (b, p))
    heads = 2 * step_pairs
    tile = lambda dt: pltpu.VMEM((heads, tq, tq), dt)
    acc = pltpu.VMEM((heads, tq, LANES), F32)
    column = acc
    if ck is None:
        kernel = functools.partial(_sb_prompt_kernel, tq=tq, pairs=step_pairs)
        in_specs, args, name = [q_spec, k_spec, v_spec], (q, k, v), "sb_prompt"
        scratch = [tile(F32), tile(BF16), tile(BF16), tile(F32), tile(BF16), column, column, acc]
    else:
        kernel = functools.partial(_fox_prompt_kernel, tq=tq, nk=nq)
        ck_spec = pl.BlockSpec((None, 2 * nq, tq), lambda b, p, i: (b * pairs + p, 0, 0))
        in_specs, args, name = [q_spec, k_spec, v_spec, ck_spec], (q, k, v, ck), "fox_prompt"
        scratch = [pltpu.VMEM((2, 8, LANES), F32), tile(F32), tile(BF16), column, column, column, acc]
    return pl.pallas_call(
        kernel,
        grid=(batch, pairs // step_pairs, nq),
        in_specs=in_specs,
        out_specs=q_spec,
        out_shape=jax.ShapeDtypeStruct((n, width), BF16),
        scratch_shapes=scratch,
        compiler_params=_params("parallel", "parallel", "arbitrary"),
        name=name,
    )(*args)


def _ab_sample_kernel(fq_ref, fk_ref, fv_ref, cfk_ref, cfv_ref, ckc_ref, ckn_ref,
                      sq_ref, sk_ref, sv_ref, csk_ref, csv_ref, fo_ref, so_ref, *, t, tkc, nkc, tks, nks):
    row = lax.broadcasted_iota(jnp.int32, (t, t), 0)
    col = lax.broadcasted_iota(jnp.int32, (t, t), 1)

    qs = _split_heads(fq_ref[...])
    kmax = _max_key_norms(lambda off, n: cfk_ref[:, pl.ds(off, n)], nkc * tkc, tks)
    reach = [_fox_reach(qs[h], kmax[h]) for h in range(2)]
    ck_of_block = lambda h, j: ckc_ref[pl.ds(h * nkc + j, 1), :]

    def fox_cache(j, states):
        off = pl.multiple_of(j * tkc, tkc)
        kj = cfk_ref[:, pl.ds(off, tkc)].astype(BF16)
        vj = cfv_ref[:, pl.ds(off, tkc)].astype(BF16)
        return tuple(
            _fox_update(states[h], qs[h], kj, vj, ck_of_block(h, j), None, transposed=True)
            for h in range(2))

    states = tuple(
        _fox_update(_fox_init(t), qs[h], fk_ref[...], fv_ref[...], ckn_ref[h:h + 1, :], col <= row)
        for h in range(2))
    states = _newest_first(nkc - 1, states, fox_cache,
                           lambda j, st: _fox_may_matter(j, st, reach, ck_of_block))
    outs = [acc / l for (_, l, acc) in states]
    fo_ref[...] = _merge_heads(outs[0], outs[1]).astype(fo_ref.dtype)

    qs2 = _split_heads(sq_ref[...])
    tri_new = _strict_lower(t)
    tri_cache = _strict_lower(tks)
    sstates = tuple(
        _sb_update(_sb_init(t), qs2[h], sk_ref[...], sv_ref[...], tri_new, col < row)
        for h in range(2))

    def sb_cache(j, st):
        off = pl.multiple_of(j * tks, tks)
        kj = csk_ref[:, pl.ds(off, tks)].astype(BF16)
        vj = csv_ref[:, pl.ds(off, tks)].astype(BF16)
        return tuple(_sb_update(st[h], qs2[h], kj, vj, tri_cache, None, transposed=True)
                     for h in range(2))

    sstates = _newest_first(nks - 1, sstates, sb_cache, lambda j, st: _sb_may_matter(st))
    so_ref[...] = _merge_heads(sstates[0][1], sstates[1][1]).astype(so_ref.dtype)


def _ab_sample_attention(fq, fk, fv, cfk, cfv, ck_cache, ck_new, sq, sk, sv, csk, csv,
                         batch, t, past, tkc, tks):
    n, width = fq.shape
    pairs = width // LANES
    nkc = past // tkc
    new_spec = pl.BlockSpec((t, LANES), lambda b, p: (b, p))
    cache_spec = pl.BlockSpec((None, LANES, past), lambda b, p: (b, p, 0))
    ckc_spec = pl.BlockSpec((None, 2 * nkc, tkc), lambda b, p: (b * pairs + p, 0, 0))
    ckn_spec = pl.BlockSpec((None, 2, t), lambda b, p: (b * pairs + p, 0, 0))
    out = jax.ShapeDtypeStruct((n, width), BF16)
    return pl.pallas_call(
        functools.partial(_ab_sample_kernel, t=t, tkc=tkc, nkc=nkc, tks=tks, nks=past // tks),
        grid=(batch, pairs),
        in_specs=[new_spec, new_spec, new_spec, cache_spec, cache_spec, ckc_spec, ckn_spec,
                  new_spec, new_spec, new_spec, cache_spec, cache_spec],
        out_specs=[new_spec, new_spec],
        out_shape=[out, out],
        compiler_params=_params("parallel", "parallel"),
        name="ab_sample",
    )(fq, fk, fv, cfk, cfv, ck_cache, ck_new, sq, sk, sv, csk, csv)


FF_CHUNK = 256


def _tail_kernel(*refs, n_in, final_norm):
    h_ref, p_ref = refs[0], refs[1 + n_in]
    a_refs, wo_refs = refs[1:1 + n_in], refs[2 + n_in:2 + 2 * n_in]
    g1_ref, wg_ref, wu_ref, wd_ref, g2_ref, wpg_ref, wpp_ref, gf_ref, o_ref, act_ref = refs[2 + 2 * n_in:]
    h1 = h_ref[...]
    for a_ref, w_ref in zip(a_refs, wo_refs):
        h1 = h1 + _dot(a_ref[...], w_ref[...])
    o_ref[...] = h1
    xn = _rms(h1, g1_ref[...]).astype(BF16)
    for c in range(wg_ref.shape[1] // FF_CHUNK):
        cols = slice(c * FF_CHUNK, (c + 1) * FF_CHUNK)
        gate = _dot(xn, wg_ref[:, cols])
        up = _dot(xn, wu_ref[:, cols])
        act_ref[:, cols] = (gate * jax.nn.sigmoid(gate) * up).astype(BF16)
    h2 = o_ref[...] + _dot(act_ref[...], wd_ref[...])
    gate = jax.nn.sigmoid(_dot(_rms(h2, g2_ref[...]).astype(BF16), wpg_ref[...]))
    h3 = h2 + _dot(p_ref[...].astype(BF16), wpp_ref[...]) * gate
    o_ref[...] = _rms(h3, gf_ref[...]) if final_norm else h3


def _layer_tail(h, acts, p_all, layer, w_outs, ffn, pe, g_final, final_norm):
    n, d = h.shape
    g1, w_gate, w_up, w_down = ffn
    g2, w_pg, w_pp = pe
    assert w_gate.shape[1] % FF_CHUNK == 0
    tm = _row_tile(n, 512)
    row = lambda i: (i, 0)
    const = lambda i: (0, 0)
    resident = lambda w: pl.BlockSpec(w.shape, const, pipeline_mode=pl.Buffered(1))
    vec = pl.BlockSpec((1, d), const)
    in_specs = [pl.BlockSpec((tm, d), row)]
    in_specs += [pl.BlockSpec((tm, a.shape[1]), row) for a in acts]
    in_specs += [pl.BlockSpec((None, tm, p_all.shape[2]), lambda i: (layer, i, 0))]
    in_specs += [resident(w) for w in w_outs]
    in_specs += [vec, resident(w_gate), resident(w_up), resident(w_down), vec, resident(w_pg),
                 resident(w_pp), vec]
    return pl.pallas_call(
        functools.partial(_tail_kernel, n_in=len(acts), final_norm=final_norm),
        grid=(n // tm,),
        in_specs=in_specs,
        out_specs=pl.BlockSpec((tm, d), row),
        out_shape=jax.ShapeDtypeStruct((n, d), F32),
        scratch_shapes=[pltpu.VMEM((tm, w_gate.shape[1]), BF16)],
        compiler_params=_params("parallel"),
        name="tail_final" if final_norm else "tail",
    )(h, *acts, p_all, *w_outs, g1, w_gate, w_up, w_down, g2, w_pg, w_pp, g_final)


def _rope(y, cos, sin_lo, sin_hi):
    out = []
    for c in range(y.shape[1] // LANES):
        yc = y[:, c * LANES:(c + 1) * LANES]
        out.append(yc * cos + pltpu.roll(yc, 8, 1) * sin_hi + pltpu.roll(yc, LANES - 8, 1) * sin_lo)
    return out[0] if len(out) == 1 else jnp.concatenate(out, axis=1)


def _proj_c_kernel(x_ref, g_ref, w_ref, cos_ref, slo_ref, shi_ref,
                   k_ref, v_ref, qb_ref, kd_ref, vd_ref, *, q_w, kv_w, window_only, nper):
    xn = _rms(x_ref[...], g_ref[...]).astype(BF16)
    tabs = (cos_ref[...], slo_ref[...], shi_ref[...])
    qb_ref[...] = (_rope(_dot(xn, w_ref[:, :q_w]), *tabs) * SCALE).astype(BF16)
    o = q_w + 2 * kv_w
    kd_ref[...] = _rope(_dot(xn, w_ref[:, o:o + 2 * kv_w]), *tabs).astype(BF16)
    o += 2 * kv_w
    vd_ref[...] = _dot(xn, w_ref[:, o:o + 2 * kv_w]).astype(BF16)

    def f32_state():
        k = _rope(_dot(xn, w_ref[:, q_w:q_w + kv_w]), *tabs)
        v = _dot(xn, w_ref[:, q_w + kv_w:q_w + 2 * kv_w])
        k_ref[...] = k[-k_ref.shape[0]:]
        v_ref[...] = v[-v_ref.shape[0]:]

    if window_only:
        pl.when(pl.program_id(0) % nper == nper - 1)(f32_state)
    else:
        f32_state()


def _proj_c(x, g, w_all, tables, period, q_w, kv_w, window_only):
    n, d = x.shape
    tm = _row_tile(period, 512)
    nper = period // tm
    row = lambda i: (i, 0)
    const = lambda i: (0, 0)
    tab = pl.BlockSpec((tm, LANES), lambda i: (i % nper, 0))
    if window_only:
        assert tm >= WINDOW
        state_spec = pl.BlockSpec((None, WINDOW, kv_w), lambda i: (i // nper, 0, 0))
        state_shape = jax.ShapeDtypeStruct((n // period, WINDOW, kv_w), F32)
    else:
        state_spec = pl.BlockSpec((tm, kv_w), row)
        state_shape = jax.ShapeDtypeStruct((n, kv_w), F32)
    return pl.pallas_call(
        functools.partial(_proj_c_kernel, q_w=q_w, kv_w=kv_w, window_only=window_only, nper=nper),
        grid=(n // tm,),
        in_specs=[pl.BlockSpec((tm, d), row), pl.BlockSpec((1, d), const),
                  pl.BlockSpec(w_all.shape, const), tab, tab, tab],
        out_specs=[state_spec, state_spec,
                   pl.BlockSpec((tm, q_w), row), pl.BlockSpec((tm, 2 * kv_w), row),
                   pl.BlockSpec((tm, 2 * kv_w), row)],
        out_shape=[state_shape, state_shape,
                   jax.ShapeDtypeStruct((n, q_w), BF16), jax.ShapeDtypeStruct((n, 2 * kv_w), BF16),
                   jax.ShapeDtypeStruct((n, 2 * kv_w), BF16)],
        compiler_params=_params("arbitrary"),
        name="proj_c",
    )(x, g, w_all, *tables)


def _rope_tables(pos):
    half = ROPE_DIM // 2
    inv_freq = ROPE_THETA ** (-2.0 * jnp.arange(half, dtype=F32) / ROPE_DIM)
    ang = pos.astype(F32)[:, None] * inv_freq[None, :]
    cos, sin = jnp.cos(ang), jnp.sin(ang)
    n = pos.shape[0]
    rest = HEAD_DIM - ROPE_DIM
    cos_h = jnp.concatenate([cos, cos, jnp.ones((n, rest), F32)], axis=1)
    lo_h = jnp.concatenate([-sin, jnp.zeros((n, half + rest), F32)], axis=1)
    hi_h = jnp.concatenate([jnp.zeros((n, half), F32), sin, jnp.zeros((n, rest), F32)], axis=1)
    two = lambda a: jnp.concatenate([a, a], axis=1)
    return two(cos_h), two(lo_h), two(hi_h)


def _swa_kernel(sink_ref, q_ref, kprev_ref, kcur_ref, vprev_ref, vcur_ref, o_ref, kb_ref, vb_ref,
                *, tq, n_kv, group, first_pos_of_tile):
    band = WINDOW + CHUNK
    kb_ref[0:WINDOW, :] = kprev_ref[...]
    kb_ref[WINDOW:, :] = kcur_ref[...]
    vb_ref[0:WINDOW, :] = vprev_ref[...]
    vb_ref[WINDOW:, :] = vcur_ref[...]
    rows = group * CHUNK
    ridx = lax.broadcasted_iota(jnp.int32, (rows, 1), 0) // CHUNK
    col = lax.broadcasted_iota(jnp.int32, (rows, band), 1)
    tile_pos = None if first_pos_of_tile is None else first_pos_of_tile(pl.program_id(1))

    def chunk(cc, _):
        base = pl.multiple_of(cc * CHUNK, CHUNK)
        kv_heads = range(n_kv)
        scores = []
        for kvh in kv_heads:
            lhs = []
            for pr in range(group // 2):
                lane0 = (kvh * (group // 2) + pr) * LANES
                lhs.extend(_split_heads(q_ref[pl.ds(base, CHUNK), lane0:lane0 + LANES]))
            s = _dot_t(jnp.concatenate(lhs, axis=0), kb_ref[pl.ds(base, band), kvh * LANES:(kvh + 1) * LANES])
            if tile_pos is not None:
                s = jnp.where(tile_pos - WINDOW + cc * CHUNK + col >= 0, s, -jnp.inf)
            scores.append(s)
        weights, dens = [], []
        for kvh in kv_heads:
            sink = jnp.zeros((rows, 1), F32)
            for gi in range(group):
                sink = jnp.where(ridx == gi, sink_ref[kvh * group + gi], sink)
            m = jnp.maximum(jnp.max(scores[kvh], axis=-1, keepdims=True), sink)
            e = jnp.exp(scores[kvh] - m)
            dens.append(jnp.sum(e, axis=-1, keepdims=True) + jnp.exp(sink - m))
            weights.append(e.astype(BF16))
        for kvh in kv_heads:
            o = _dot(weights[kvh], vb_ref[pl.ds(base, band), kvh * LANES:(kvh + 1) * LANES]) / dens[kvh]
            for pr in range(group // 2):
                lane0 = (kvh * (group // 2) + pr) * LANES
                pair = _merge_heads(o[(2 * pr) * CHUNK:(2 * pr + 1) * CHUNK],
                                    o[(2 * pr + 1) * CHUNK:(2 * pr + 2) * CHUNK])
                o_ref[pl.ds(base, CHUNK), lane0:lane0 + LANES] = pair.astype(o_ref.dtype)
        return 0

    lax.fori_loop(0, tq // CHUNK, chunk, 0)


def _swa_prompt(sinks, q, kd, vd, batch, seq, n_kv):
    n, q_w = q.shape
    group = q_w // HEAD_DIM // n_kv
    tq = _row_tile(seq, 512)
    nt = seq // tq
    per_win = tq // WINDOW
    cur = lambda b, i: (b * nt + i, 0)
    prev = lambda b, i: (jnp.maximum(b * nt * per_win + i * per_win - 1, b * nt * per_win), 0)
    kv_w = kd.shape[1]
    return pl.pallas_call(
        functools.partial(_swa_kernel, tq=tq, n_kv=n_kv, group=group,
                          first_pos_of_tile=lambda i: i * tq),
        grid=(batch, nt),
        in_specs=[pl.BlockSpec(memory_space=pltpu.SMEM),
                  pl.BlockSpec((tq, q_w), cur),
                  pl.BlockSpec((WINDOW, kv_w), prev), pl.BlockSpec((tq, kv_w), cur),
                  pl.BlockSpec((WINDOW, kv_w), prev), pl.BlockSpec((tq, kv_w), cur)],
        out_specs=pl.BlockSpec((tq, q_w), cur),
        out_shape=jax.ShapeDtypeStruct((n, q_w), BF16),
        scratch_shapes=[pltpu.VMEM((WINDOW + tq, kv_w), BF16), pltpu.VMEM((WINDOW + tq, kv_w), BF16)],
        compiler_params=_params("parallel", "parallel"),
        name="swa_prompt",
    )(sinks, q, kd, kd, vd, vd)


def _swa_sample(sinks, q, kd_cache, kd_new, vd_cache, vd_new, batch, t, n_kv):
    n, q_w = q.shape
    group = q_w // HEAD_DIM // n_kv
    kv_w = kd_new.shape[1]
    blk = lambda b, i: (b, 0)
    return pl.pallas_call(
        functools.partial(_swa_kernel, tq=t, n_kv=n_kv, group=group, first_pos_of_tile=None),
        grid=(batch, 1),
        in_specs=[pl.BlockSpec(memory_space=pltpu.SMEM),
                  pl.BlockSpec((t, q_w), blk),
                  pl.BlockSpec((WINDOW, kv_w), blk), pl.BlockSpec((t, kv_w), blk),
                  pl.BlockSpec((WINDOW, kv_w), blk), pl.BlockSpec((t, kv_w), blk)],
        out_specs=pl.BlockSpec((t, q_w), blk),
        out_shape=jax.ShapeDtypeStruct((n, q_w), BF16),
        scratch_shapes=[pltpu.VMEM((WINDOW + t, kv_w), BF16), pltpu.VMEM((WINDOW + t, kv_w), BF16)],
        compiler_params=_params("parallel", "parallel"),
        name="swa_sample",
    )(sinks, q, kd_cache, kd_new, vd_cache, vd_new)


def _dup_heads(a, n_heads):
    lead = a.shape[:-1]
    a = a.reshape(*lead, n_heads, 1, HEAD_DIM)
    return jnp.broadcast_to(a, (*lead, n_heads, 2, HEAD_DIM)).reshape(*lead, n_heads * LANES)


def _prep_weights(W, n_fox, n_sb, n_kv):
    fox_w = n_fox * HEAD_DIM
    w_in = W["w_ab_in"][0]
    w_main = jnp.concatenate([w_in[:, :3 * fox_w], w_in[:, 3 * fox_w + n_fox:]], axis=1).astype(BF16)
    w_flog = jnp.pad(w_in[:, 3 * fox_w:3 * fox_w + n_fox], ((0, 0), (0, LANES - n_fox))).astype(BF16)
    b_f = jnp.pad(W["b_fox_f"][0][None, :], ((0, 0), (0, LANES - n_fox)))
    fq, fk, fv, flog = (w_in[:, :fox_w], w_in[:, fox_w:2 * fox_w], w_in[:, 2 * fox_w:3 * fox_w],
                        w_in[:, 3 * fox_w:3 * fox_w + n_fox])
    sq, sk, sv = jnp.split(w_in[:, 3 * fox_w + n_fox:], 3, axis=1)
    gate_rows = 16
    w_nt = jnp.concatenate([fq, fv, sq, sv], axis=1).astype(BF16)
    w_t = jnp.concatenate([fk, fv, sk, sv, jnp.pad(flog, ((0, 0), (0, gate_rows - n_fox)))],
                          axis=1).T.astype(BF16)
    b_col = jnp.pad(W["b_fox_f"][0], (0, gate_rows - n_fox))[:, None]
    w_c = W["w_c_in"][0]
    kv_w = n_kv * HEAD_DIM
    q_w = w_c.shape[1] - 2 * kv_w
    w_k, w_v = w_c[:, q_w:q_w + kv_w], w_c[:, q_w + kv_w:]
    w_c_all = jnp.concatenate([w_c, _dup_heads(w_k, n_kv), _dup_heads(w_v, n_kv)], axis=1).astype(BF16)
    bf = lambda name, i: W[name][i].astype(BF16)
    out = dict(w_main=w_main, w_flog=w_flog, b_f=b_f, w_nt=w_nt, w_t=w_t, b_col=b_col,
               w_c_all=w_c_all, q_w=q_w, kv_w=kv_w,
               w_ab_out_f=W["w_ab_out"][0][:fox_w].astype(BF16),
               w_ab_out_s=W["w_ab_out"][0][fox_w:].astype(BF16),
               w_c_out=bf("w_c_out", 0), c_sinks=W["c_sinks"][0])
    for i in range(2):
        out[f"ffn{i}"] = (W["norm_ffn"][i][None, :], bf("w_ffn_gate", i), bf("w_ffn_up", i),
                          bf("w_ffn_down", i))
        out[f"pe{i}"] = (W["norm_pe"][i][None, :], bf("w_pe_gate", i), bf("w_pe_proj", i))
    out["norm_mix"] = [W["norm_mix"][i][None, :] for i in range(2)]
    out["norm_final"] = W["norm_final"][None, :]
    return out


def _pad_lanes(a):
    pad = (-a.shape[1]) % LANES
    return jnp.pad(a, ((0, 0), (0, pad))) if pad else a


def _tail(h, acts, w_outs, p, P, i):
    p_all = p.reshape(p.shape[0], h.shape[0], p.shape[-1])
    return _layer_tail(h, acts, p_all, i, w_outs, P[f"ffn{i}"], P[f"pe{i}"], P["norm_final"],
                       final_norm=(i == p.shape[0] - 1))


def _trunk_prompt(x, p, P, n_fox, n_kv):
    batch, seq, d = x.shape
    n = batch * seq
    h = x.reshape(n, d)
    fkt, fvt, skt, svt, lft, fqb, fvb, sqb, svb, fktb, sktb = _proj_ab_t(
        h, P["norm_mix"][0], P["w_nt"], P["w_t"], P["b_col"], n_fox, batch, seq)
    tq_fox = _row_tile(seq, FOX_TILE)
    ck = _cumsum_rows(lft.reshape(batch * n_fox, seq))
    ck = ck.reshape(batch * n_fox // 2, 2 * (seq // tq_fox), tq_fox)
    fo = _prompt_attention(fqb, fktb, fvb, ck, batch, seq, tq_fox)
    so = _prompt_attention(sqb, sktb, svb, None, batch, seq, _row_tile(seq, SB_TILE))
    h = _tail(h, [fo, so], [P["w_ab_out_f"], P["w_ab_out_s"]], p, P, 0)
    tables = _rope_tables(jnp.arange(seq))
    k, v, qb, kd, vd = _proj_c(h, P["norm_mix"][1], P["w_c_all"], tables, seq, P["q_w"], P["kv_w"],
                               window_only=True)
    o = _swa_prompt(P["c_sinks"], qb, kd, vd, batch, seq, n_kv)
    y = _tail(h, [o], [P["w_c_out"]], p, P, 1)
    heads = lambda a: jnp.transpose(a.reshape(batch, n_fox, HEAD_DIM, seq), (0, 3, 1, 2))[None]
    kw = k.reshape(1, batch, WINDOW, n_kv, HEAD_DIM)
    vw = v.reshape(1, batch, WINDOW, n_kv, HEAD_DIM)
    return y.reshape(batch, seq, d), (heads(fkt), heads(fvt), jnp.transpose(lft, (0, 2, 1))[None],
                                      heads(skt), heads(svt), kw, vw)


def _trunk_sample(x, p, P, caches, n_fox, n_kv):
    c_fk, c_fv, c_fl, c_sk, c_sv, c_wk, c_wv = caches
    batch, t, d = x.shape
    past = c_fk.shape[2]
    n = batch * t
    h = x.reshape(n, d)
    width = n_fox * HEAD_DIM
    fk, fv, sk, sv, logf, fqb, fkb, fvb, sqb, skb, svb = _proj_ab(
        h, P["norm_mix"][0], P["w_main"], P["w_flog"], P["b_f"], n_fox)
    lf_all = jnp.concatenate([jnp.swapaxes(c_fl[0], 1, 2),
                              jnp.swapaxes(logf.reshape(batch, t, n_fox), 1, 2)], axis=2)
    lf_rows = _pad_lanes(lf_all.reshape(batch * n_fox, past + t))
    c_rows = _cumsum_rows(lf_rows)
    tkc = _row_tile(past, 1024)
    ck_cache = c_rows[:, :past].reshape(batch * n_fox // 2, 2 * (past // tkc), tkc)
    ck_new = c_rows[:, past:past + t].reshape(batch * n_fox // 2, 2, t)
    cfk, cfv, csk, csv = [jnp.transpose(c[0], (0, 2, 3, 1)).reshape(batch, width, past)
                          for c in (c_fk, c_fv, c_sk, c_sv)]
    fo, so = _ab_sample_attention(fqb, fkb, fvb, cfk, cfv, ck_cache, ck_new,
                                  sqb, skb, svb, csk, csv, batch, t, past, tkc, _row_tile(past, 256))
    h = _tail(h, [fo, so], [P["w_ab_out_f"], P["w_ab_out_s"]], p, P, 0)
    tm = _row_tile(n, 512)
    pos = past + jnp.arange(t)
    tables = _rope_tables(jnp.tile(pos, max(tm // t, 1)))
    k, v, qb, kd, vd = _proj_c(h, P["norm_mix"][1], P["w_c_all"], tables, tables[0].shape[0],
                               P["q_w"], P["kv_w"], window_only=False)
    kd_cache = _dup_heads(c_wk[0].reshape(batch * WINDOW, n_kv * HEAD_DIM), n_kv).astype(BF16)
    vd_cache = _dup_heads(c_wv[0].reshape(batch * WINDOW, n_kv * HEAD_DIM), n_kv).astype(BF16)
    o = _swa_sample(P["c_sinks"], qb, kd_cache, kd, vd_cache, vd, batch, t, n_kv)
    y = _tail(h, [o], [P["w_c_out"]], p, P, 1)
    heads = lambda a, nh: a.reshape(1, batch, t, nh, HEAD_DIM)
    roll = lambda c, new: jnp.concatenate([c[0], new.reshape(batch, t, n_kv, HEAD_DIM)], axis=1)[None, :, -WINDOW:]
    return y.reshape(batch, t, d), (heads(fk, n_fox), heads(fv, n_fox),
                                    logf.reshape(1, batch, t, n_fox),
                                    heads(sk, n_fox), heads(sv, n_fox), roll(c_wk, k), roll(c_wv, v))


def kernel(x_prompt, x_sample, cache_fox_k, cache_fox_v, cache_fox_logf, cache_sb_k, cache_sb_v, cache_swa_k, cache_swa_v, p_prompt, p_sample, norm_mix, w_ab_in, b_fox_f, w_ab_out, w_c_in, c_sinks, w_c_out, norm_ffn, w_ffn_gate, w_ffn_up, w_ffn_down, norm_pe, w_pe_gate, w_pe_proj, norm_final):
    W = dict(norm_mix=norm_mix, w_ab_in=w_ab_in, b_fox_f=b_fox_f, w_ab_out=w_ab_out, w_c_in=w_c_in,
             c_sinks=c_sinks, w_c_out=w_c_out, norm_ffn=norm_ffn, w_ffn_gate=w_ffn_gate,
             w_ffn_up=w_ffn_up, w_ffn_down=w_ffn_down, norm_pe=norm_pe, w_pe_gate=w_pe_gate,
             w_pe_proj=w_pe_proj, norm_final=norm_final)
    n_fox = cache_fox_k.shape[3]
    n_sb = cache_sb_k.shape[3]
    n_kv = cache_swa_k.shape[3]
    assert n_fox == n_sb and n_fox % 2 == 0 and n_kv % 1 == 0
    assert cache_swa_k.shape[2] == WINDOW and x_sample.shape[1] == CHUNK
    P = _prep_weights(W, n_fox, n_sb, n_kv)
    caches = (cache_fox_k, cache_fox_v, cache_fox_logf, cache_sb_k, cache_sb_v, cache_swa_k, cache_swa_v)
    y_p, st_p = _trunk_prompt(x_prompt, p_prompt, P, n_fox, n_kv)
    y_s, st_s = _trunk_sample(x_sample, p_sample, P, caches, n_fox, n_kv)
    return (y_p, y_s, *st_p, *st_s)
```

```python
import functools

import jax
import jax.numpy as jnp
from jax import lax
from jax.experimental import pallas as pl
from jax.experimental.pallas import tpu as pltpu

F32 = jnp.float32
BF16 = jnp.bfloat16

HEAD_DIM = 64
CHUNK = 64
WINDOW = 128
ROPE_DIM = 16
ROPE_THETA = 500000.0
RMS_EPS = 1e-6
SCALE = HEAD_DIM ** -0.5
SKIP_LOGIT = -110.0
REACH_SLACK = 1.01
LOG2_E = 1.4426950408889634
LN_2 = 0.6931471805599453

LANES = 128
V7X_VMEM_BYTES = 64 * 1024 * 1024
VMEM_LIMIT = V7X_VMEM_BYTES - 8 * 1024 * 1024


def _params(*sem):
    return pltpu.CompilerParams(dimension_semantics=sem, vmem_limit_bytes=VMEM_LIMIT)


def _row_tile(n, pref):
    t = pref
    while n % t:
        t //= 2
    return t


def _rms(x, g):
    ms = jnp.mean(x * x, axis=-1, keepdims=True)
    return x * lax.rsqrt(ms + RMS_EPS) * g


def _dot(a, b):
    return jnp.dot(a, b, preferred_element_type=F32)


def _dot_t(a, b):
    return lax.dot_general(a, b, (((1,), (1,)), ((), ())), preferred_element_type=F32)


def _softplus(z):
    return jnp.maximum(z, 0.0) + jnp.log(1.0 + jnp.exp(-jnp.abs(z)))


def _split_heads(q):
    first = lax.broadcasted_iota(jnp.int32, q.shape, 1) < HEAD_DIM
    zero = jnp.zeros_like(q)
    return jnp.where(first, q, zero), jnp.where(first, zero, q)


def _merge_heads(a, b):
    first = lax.broadcasted_iota(jnp.int32, a.shape, 1) < HEAD_DIM
    return jnp.where(first, a, b)


def _proj_ab_kernel(x_ref, g_ref, w_ref, wf_ref, bf_ref,
                    fk_ref, fv_ref, sk_ref, sv_ref, lf_ref,
                    fqb_ref, fkb_ref, fvb_ref, sqb_ref, skb_ref, svb_ref, *, width):
    xn = _rms(x_ref[...], g_ref[...]).astype(BF16)

    def col(c):
        return _dot(xn, w_ref[:, c * width:(c + 1) * width])

    fqb_ref[...] = (col(0) * SCALE).astype(BF16)
    y = col(1)
    fk_ref[...] = y.reshape(fk_ref.shape)
    fkb_ref[...] = y.astype(BF16)
    y = col(2)
    fv_ref[...] = y.reshape(fv_ref.shape)
    fvb_ref[...] = y.astype(BF16)
    sqb_ref[...] = (col(3) * SCALE).astype(BF16)
    y = col(4)
    sk_ref[...] = y.reshape(sk_ref.shape)
    skb_ref[...] = y.astype(BF16)
    y = col(5)
    sv_ref[...] = y.reshape(sv_ref.shape)
    svb_ref[...] = y.astype(BF16)
    u = _dot(xn, wf_ref[...]) + bf_ref[...]
    lf = -_softplus(-u)
    lf_ref[...] = lf[:, :lf_ref.shape[1]]


def _proj_ab(x, g, w_main, w_f, b_f, n_heads):
    n, d = x.shape
    width = n_heads * HEAD_DIM
    tm = _row_tile(n, 512)
    row = lambda i: (i, 0)
    const = lambda i: (0, 0)
    f32_out = jax.ShapeDtypeStruct((n, n_heads, HEAD_DIM), F32)
    hblk = pl.BlockSpec((tm, n_heads, HEAD_DIM), lambda i: (i, 0, 0))
    bf_out = jax.ShapeDtypeStruct((n, width), BF16)
    blk = pl.BlockSpec((tm, width), row)
    return pl.pallas_call(
        functools.partial(_proj_ab_kernel, width=width),
        grid=(n // tm,),
        in_specs=[pl.BlockSpec((tm, d), row), pl.BlockSpec((1, d), const),
                  pl.BlockSpec(w_main.shape, const), pl.BlockSpec(w_f.shape, const),
                  pl.BlockSpec(b_f.shape, const)],
        out_specs=[hblk, hblk, hblk, hblk, pl.BlockSpec((tm, n_heads), row)] + [blk] * 6,
        out_shape=[f32_out] * 4 + [jax.ShapeDtypeStruct((n, n_heads), F32)] + [bf_out] * 6,
        compiler_params=_params("parallel"),
        name="proj_ab",
    )(x, g, w_main, w_f, b_f)


def _proj_ab_t_kernel(x_ref, g_ref, w_ref, wt_ref, bf_ref,
                      fkt_ref, fvt_ref, skt_ref, svt_ref, lft_ref,
                      fqb_ref, sqb_ref, fktb_ref, fvtb_ref, sktb_ref, svtb_ref, *, width, n_heads):
    xn = _rms(x_ref[...], g_ref[...]).astype(BF16)
    for c, ref in enumerate((fqb_ref, sqb_ref)):
        ref[...] = (_dot(xn, w_ref[:, c * width:(c + 1) * width]) * SCALE).astype(BF16)
    for c, (ref, bf_ref_) in enumerate(((fkt_ref, fktb_ref), (fvt_ref, fvtb_ref), (skt_ref, sktb_ref),
                                        (svt_ref, svtb_ref))):
        yt = _dot_t(wt_ref[c * width:(c + 1) * width, :], xn)
        ref[...] = yt
        bf_ref_[...] = yt.astype(BF16)
    ut = _dot_t(wt_ref[4 * width:, :], xn) + bf_ref[...]
    lft_ref[...] = (-_softplus(-ut))[:n_heads]


def _proj_ab_t(x, g, w_nt, w_t, b_col, n_heads, batch, seq):
    n, d = x.shape
    width = n_heads * HEAD_DIM
    tm = _row_tile(seq, 512)
    per = seq // tm
    row = lambda i: (i, 0)
    const = lambda i: (0, 0)
    t_blk = pl.BlockSpec((None, width, tm), lambda i: (i // per, 0, i % per))
    blk = pl.BlockSpec((tm, width), row)
    t32 = jax.ShapeDtypeStruct((batch, width, seq), F32)
    tbf = jax.ShapeDtypeStruct((batch, width, seq), BF16)
    bf_out = jax.ShapeDtypeStruct((n, width), BF16)
    return pl.pallas_call(
        functools.partial(_proj_ab_t_kernel, width=width, n_heads=n_heads),
        grid=(n // tm,),
        in_specs=[pl.BlockSpec((tm, d), row), pl.BlockSpec((1, d), const),
                  pl.BlockSpec(w_nt.shape, const), pl.BlockSpec(w_t.shape, const),
                  pl.BlockSpec(b_col.shape, const)],
        out_specs=[t_blk] * 4 + [pl.BlockSpec((None, n_heads, tm), lambda i: (i // per, 0, i % per))]
        + [blk] * 2 + [t_blk] * 4,
        out_shape=[t32] * 4 + [jax.ShapeDtypeStruct((batch, n_heads, seq), F32)] + [bf_out] * 2 + [tbf] * 4,
        compiler_params=_params("parallel"),
        name="proj_ab_t",
    )(x, g, w_nt, w_t, b_col)


def _cumsum_kernel(x_ref, o_ref, *, n_chunks):
    r = lax.broadcasted_iota(jnp.int32, (LANES, LANES), 0)
    c = lax.broadcasted_iota(jnp.int32, (LANES, LANES), 1)
    upper = (r <= c).astype(BF16)

    def local(t, _):
        off = pl.multiple_of(t * LANES, LANES)
        x = x_ref[:, pl.ds(off, LANES)]
        x1 = x.astype(BF16)
        r1 = x - x1.astype(F32)
        x2 = r1.astype(BF16)
        x3 = (r1 - x2.astype(F32)).astype(BF16)
        o_ref[:, pl.ds(off, LANES)] = _dot(x1, upper) + _dot(x2, upper) + _dot(x3, upper)
        return 0

    def add_carry(t, carry):
        off = pl.multiple_of(t * LANES, LANES)
        y = o_ref[:, pl.ds(off, LANES)] + carry
        o_ref[:, pl.ds(off, LANES)] = y
        return y[:, LANES - 1:LANES]

    unroll = 8 if n_chunks % 8 == 0 else 1
    lax.fori_loop(0, n_chunks, local, 0, unroll=unroll)
    lax.fori_loop(0, n_chunks, add_carry, jnp.zeros((x_ref.shape[0], 1), F32), unroll=unroll)


def _cumsum_rows(x):
    r, l = x.shape
    tr = _row_tile(r, 16)
    return pl.pallas_call(
        functools.partial(_cumsum_kernel, n_chunks=l // LANES),
        grid=(r // tr,),
        in_specs=[pl.BlockSpec((tr, l), lambda i: (i, 0))],
        out_specs=pl.BlockSpec((tr, l), lambda i: (i, 0)),
        out_shape=jax.ShapeDtypeStruct((r, l), F32),
        compiler_params=_params("parallel"),
        name="cumsum_logf",
    )(x)


def _qk(qh, kj, transposed):
    return _dot(qh, kj) if transposed else _dot_t(qh, kj)


def _pv(p, vj, transposed):
    return _dot_t(p, vj) if transposed else _dot(p, vj)


def _fox_update(state, qh, kj, vj, ck_row, valid, transposed=False):
    m, l, acc = state
    s = _qk(qh, kj, transposed) - ck_row
    if valid is not None:
        s = jnp.where(valid, s, -jnp.inf)
    m_new = jnp.maximum(m, jnp.max(s, axis=-1, keepdims=True))
    alpha = jnp.exp(m - m_new)
    p = jnp.exp(s - m_new)
    l = alpha * l + jnp.sum(p, axis=-1, keepdims=True)
    acc = alpha * acc + _pv(p.astype(BF16), vj, transposed)
    return m_new, l, acc


def _fox_init(tq):
    return (jnp.full((tq, 1), -jnp.inf, F32), jnp.zeros((tq, 1), F32), jnp.zeros((tq, LANES), F32))


def _sb_update(state, qh, kj, vj, tri, valid, transposed=False):
    carry, acc = state
    z = _qk(qh, kj, transposed)
    log_1m = -_softplus(z)
    if valid is not None:
        log_1m = jnp.where(valid, log_1m, 0.0)
    hi = log_1m.astype(BF16)
    lo = (log_1m - hi.astype(F32)).astype(BF16)
    between = _dot(hi, tri) + _dot(lo, tri)
    a = jnp.exp(z + log_1m + between + carry)
    if valid is not None:
        a = jnp.where(valid, a, 0.0)
    acc = acc + _pv(a.astype(BF16), vj, transposed)
    carry = carry + jnp.sum(log_1m, axis=-1, keepdims=True)
    return carry, acc


def _sb_init(tq):
    return (jnp.zeros((tq, 1), F32), jnp.zeros((tq, LANES), F32))


def _strict_lower(t):
    r = lax.broadcasted_iota(jnp.int32, (t, t), 0)
    c = lax.broadcasted_iota(jnp.int32, (t, t), 1)
    return (r > c).astype(BF16)


def _newest_first(j0, states, step, may_matter):
    def cond(c):
        return jnp.logical_and(c[0] >= 0, c[1])

    def body(c):
        j, _, st = c
        st = step(j, st)
        return j - 1, may_matter(j - 1, st), st

    return lax.while_loop(cond, body, (j0, may_matter(j0, states), states))[2]


def _fox_reach(q_half, kmax):
    qf = q_half.astype(F32)
    return jnp.sqrt(jnp.sum(qf * qf, axis=-1, keepdims=True)) * kmax * REACH_SLACK


def _fox_may_matter(j, states, reach, ck_of_block):
    jc = jnp.maximum(j, 0)
    worst = None
    for h in range(len(states)):
        ck_last = ck_of_block(h, jc)[:, -1:]
        gap = reach[h] - ck_last - states[h][0]
        worst = gap if worst is None else jnp.maximum(worst, gap)
    return jnp.max(worst) >= SKIP_LOGIT


def _sb_may_matter(states):
    return jnp.max(jnp.maximum(states[0][0], states[1][0])) >= SKIP_LOGIT


def _max_key_norms(load_keys, n_keys, chunk):
    def body(t, best):
        kf = load_keys(pl.multiple_of(t * chunk, chunk), chunk).astype(BF16).astype(F32)
        sq = kf * kf
        n0 = jnp.sum(sq[:HEAD_DIM], axis=0, keepdims=True)
        n1 = jnp.sum(sq[HEAD_DIM:], axis=0, keepdims=True)
        return jnp.maximum(best[0], n0), jnp.maximum(best[1], n1)

    zero = jnp.zeros((1, chunk), F32)
    best = lax.fori_loop(0, n_keys // chunk, body, (zero, zero))
    return [jnp.sqrt(jnp.max(b, axis=1, keepdims=True)) for b in best]


STRIP = 16
FOX_TILE = 512
SB_TILE = 256
FOX_STEP_PAIRS = 1
SB_STEP_PAIRS = 2


def _strips(tq, tk, diag):
    for r in range(tq // STRIP):
        hi = (r + 1) * STRIP
        ncol = min(tk, -(-hi // LANES) * LANES) if diag else tk
        yield r * STRIP, slice(r * STRIP, hi), ncol


def _group_mask(row0, g, strict):
    lo = g * LANES
    if lo + LANES - 1 <= row0 - (1 if strict else 0):
        return None
    rr = row0 + lax.broadcasted_iota(jnp.int32, (STRIP, LANES), 0)
    cc = lo + lax.broadcasted_iota(jnp.int32, (STRIP, LANES), 1)
    return cc < rr if strict else cc <= rr


def _row_bcast(x):
    return jnp.broadcast_to(x, (x.shape[0], LANES))


def _fox_block(qs, kjs, vjs, ck_rows, diag, scr):
    s_scr, p_scr, m_scr, l_scr, a_scr, acc_scr = scr
    tq, tk = s_scr.shape[1:]
    heads = range(len(qs))
    for h in heads:
        s_scr[h] = _dot(qs[h], kjs[h // 2])

    def logits(h, row0, rows, ncol):
        out = []
        for g in range(ncol // LANES):
            cols = slice(g * LANES, (g + 1) * LANES)
            sg = s_scr[h, rows, cols] - ck_rows[h][:, cols]
            mask = _group_mask(row0, g, strict=False) if diag else None
            if mask is not None:
                sg = jnp.where(mask, sg, -jnp.inf)
            s_scr[h, rows, cols] = sg
            out.append(sg)
        return out

    def finished_logits(h, rows, ncol):
        return [s_scr[h, rows, g * LANES:(g + 1) * LANES] for g in range(ncol // LANES)]

    for h in heads:
        for row0, rows, ncol in _strips(tq, tk, diag):
            part = functools.reduce(jnp.maximum, logits(h, row0, rows, ncol))
            m_new = _row_bcast(jnp.max(part, axis=-1, keepdims=True))
            if not diag:
                m_old = m_scr[h, rows, :]
                m_new = jnp.maximum(m_old, m_new)
                a_scr[h, rows, :] = jnp.exp(m_old - m_new)
            m_scr[h, rows, :] = m_new
    for h in heads:
        for row0, rows, ncol in _strips(tq, tk, diag):
            m_new = m_scr[h, rows, :]
            pg = [jnp.exp(sg - m_new) for sg in finished_logits(h, rows, ncol)]
            lsum = functools.reduce(jnp.add, pg)
            if diag:
                l_scr[h, rows, :] = lsum
            else:
                alpha = a_scr[h, rows, :]
                l_scr[h, rows, :] = alpha * l_scr[h, rows, :] + lsum
                acc_scr[h, rows, :] = alpha * acc_scr[h, rows, :]
            pg = [p.astype(BF16) for p in pg]
            pg += [jnp.zeros((STRIP, LANES), BF16)] * ((tk - ncol) // LANES)
            p_scr[h, rows, :] = jnp.concatenate(pg, axis=1)
    for h in heads:
        pv = _dot_t(p_scr[h], vjs[h // 2])
        acc_scr[h] = pv if diag else acc_scr[h] + pv


def _log_1m_sigmoid(z):
    t = jnp.log2(1.0 + jnp.exp2(jnp.abs(z) * (-LOG2_E)))
    return t * (-LN_2) - jnp.maximum(z, 0.0)


def _sb_block(qs, kjs, vjs, tri, diag, scr):
    z_scr, hi_scr, lo_scr, b_scr, p_scr, c_scr, d_scr, acc_scr = scr
    tq, tk = z_scr.shape[1:]
    zero = jnp.zeros((STRIP, LANES), BF16)
    heads = range(len(qs))
    for h in heads:
        z_scr[h] = _dot(qs[h], kjs[h // 2])
    for h in heads:
        for row0, rows, ncol in _strips(tq, tk, diag):
            his, los, dsum = [], [], None
            for g in range(ncol // LANES):
                cols = slice(g * LANES, (g + 1) * LANES)
                z = z_scr[h, rows, cols]
                log_1m = _log_1m_sigmoid(z)
                mask = _group_mask(row0, g, strict=True) if diag else None
                if mask is not None:
                    log_1m = jnp.where(mask, log_1m, 0.0)
                hi = log_1m.astype(BF16)
                his.append(hi)
                los.append((log_1m - hi.astype(F32)).astype(BF16))
                z_scr[h, rows, cols] = z + log_1m
                dsum = log_1m if dsum is None else dsum + log_1m
            pad = [zero] * ((tk - ncol) // LANES)
            hi_scr[h, rows, :] = jnp.concatenate(his + pad, axis=1)
            lo_scr[h, rows, :] = jnp.concatenate(los + pad, axis=1)
            d_scr[h, rows, :] = dsum
    for h in heads:
        b_scr[h] = _dot(hi_scr[h], tri) + _dot(lo_scr[h], tri)
    for h in heads:
        for row0, rows, ncol in _strips(tq, tk, diag):
            carry = None if diag else c_scr[h, rows, :]
            pg = []
            for g in range(ncol // LANES):
                cols = slice(g * LANES, (g + 1) * LANES)
                e = z_scr[h, rows, cols] + b_scr[h, rows, cols]
                a = jnp.exp(e if carry is None else e + carry)
                mask = _group_mask(row0, g, strict=True) if diag else None
                pg.append((a if mask is None else jnp.where(mask, a, 0.0)).astype(BF16))
            p_scr[h, rows, :] = jnp.concatenate(pg + [zero] * ((tk - ncol) // LANES), axis=1)
    for h in heads:
        pv = _dot_t(p_scr[h], vjs[h // 2])
        acc_scr[h] = pv if diag else acc_scr[h] + pv
        d = _row_bcast(jnp.sum(d_scr[h], axis=-1, keepdims=True))
        c_scr[h] = d if diag else c_scr[h] + d


def _fox_prompt_kernel(q_ref, k_ref, v_ref, ck_ref, o_ref, kmax_ref, *scr, tq, nk, pairs):
    i = pl.program_id(2)
    lanes = lambda pr: slice(pr * LANES, (pr + 1) * LANES)
    heads = range(2 * pairs)

    @pl.when(i == 0)
    def _():
        for pr in range(pairs):
            norms = _max_key_norms(lambda off, n: k_ref[lanes(pr), pl.ds(off, n)], nk * tq, tq)
            for hh in range(2):
                kmax_ref[2 * pr + hh] = jnp.broadcast_to(norms[hh], kmax_ref.shape[1:])

    qs = [qh for pr in range(pairs) for qh in _split_heads(q_ref[:, lanes(pr)])]
    reach = [_fox_reach(qs[h], kmax_ref[h, 0:1, 0:1]) for h in heads]
    ck_of_block = lambda h, j: ck_ref[pl.ds(h * nk + j, 1), :]
    m_scr = scr[2]

    def step(j, carry, diag=False):
        off = pl.multiple_of(j * tq, tq)
        kjs = [k_ref[lanes(pr), pl.ds(off, tq)] for pr in range(pairs)]
        vjs = [v_ref[lanes(pr), pl.ds(off, tq)] for pr in range(pairs)]
        _fox_block(qs, kjs, vjs, [ck_of_block(h, j) for h in heads], diag, scr)
        return carry

    def may_matter(j, carry):
        return _fox_may_matter(j, [(m_scr[h],) for h in heads], reach, ck_of_block)

    step(i, 0, diag=True)
    _newest_first(i - 1, 0, step, may_matter)
    l_scr, acc_scr = scr[3], scr[5]
    outs = [acc_scr[h] / jnp.sum(l_scr[h], axis=-1, keepdims=True) for h in heads]
    for pr in range(pairs):
        o_ref[:, lanes(pr)] = _merge_heads(outs[2 * pr], outs[2 * pr + 1]).astype(o_ref.dtype)


def _sb_prompt_kernel(q_ref, k_ref, v_ref, o_ref, *scr, tq, pairs):
    i = pl.program_id(2)
    lanes = lambda pr: slice(pr * LANES, (pr + 1) * LANES)
    qs = [qh for pr in range(pairs) for qh in _split_heads(q_ref[:, lanes(pr)])]
    tri = _strict_lower(tq)
    c_scr, acc_scr = scr[5], scr[7]

    def step(j, carry, diag=False):
        off = pl.multiple_of(j * tq, tq)
        kjs = [k_ref[lanes(pr), pl.ds(off, tq)] for pr in range(pairs)]
        vjs = [v_ref[lanes(pr), pl.ds(off, tq)] for pr in range(pairs)]
        _sb_block(qs, kjs, vjs, tri, diag, scr)
        return carry

    def may_matter(j, carry):
        worst = functools.reduce(jnp.maximum, [c_scr[h] for h in range(2 * pairs)])
        return jnp.max(worst) >= SKIP_LOGIT

    step(i, 0, diag=True)
    _newest_first(i - 1, 0, step, may_matter)
    for pr in range(pairs):
        o_ref[:, lanes(pr)] = _merge_heads(acc_scr[2 * pr], acc_scr[2 * pr + 1]).astype(o_ref.dtype)


def _prompt_attention(q, k, v, ck, batch, seq, tq):
    n, width = q.shape
    pairs = width // LANES
    nq = seq // tq
    want = SB_STEP_PAIRS if ck is None else FOX_STEP_PAIRS
    step_pairs = want if pairs % want == 0 else 1
    wide = step_pairs * LANES
    q_spec = pl.BlockSpec((tq, wide), lambda b, p, i: (b * nq + i, p))
    k_spec = pl.BlockSpec((None, wide, seq), lambda b, p, i: (b, p, 0))
    v_spec = k_spec
    heads = 2 * step_pairs
    tile = lambda dt: pltpu.VMEM((heads, tq, tq), dt)
    acc = pltpu.VMEM((heads, tq, LANES), F32)
    column = acc
    if ck is None:
        kernel = functools.partial(_sb_prompt_kernel, tq=tq, pairs=step_pairs)
        in_specs, args, name = [q_spec, k_spec, v_spec], (q, k, v), "sb_prompt"
        scratch = [tile(F32), tile(BF16), tile(BF16), tile(F32), tile(BF16), column, column, acc]
    else:
        kernel = functools.partial(_fox_prompt_kernel, tq=tq, nk=nq, pairs=step_pairs)
        steps = pairs // step_pairs
        ck = ck.reshape(batch * steps, heads * nq, tq)
        ck_spec = pl.BlockSpec((None, heads * nq, tq), lambda b, p, i: (b * steps + p, 0, 0))
        in_specs, args, name = [q_spec, k_spec, v_spec, ck_spec], (q, k, v, ck), "fox_prompt"
        scratch = [pltpu.VMEM((heads, 8, LANES), F32), tile(F32), tile(BF16), column, column, column, acc]
    return pl.pallas_call(
        kernel,
        grid=(batch, pairs // step_pairs, nq),
        in_specs=in_specs,
        out_specs=q_spec,
        out_shape=jax.ShapeDtypeStruct((n, width), BF16),
        scratch_shapes=scratch,
        compiler_params=_params("parallel", "parallel", "arbitrary"),
        name=name,
    )(*args)


def _ab_sample_kernel(fq_ref, fk_ref, fv_ref, cfk_ref, cfv_ref, ckc_ref, ckn_ref,
                      sq_ref, sk_ref, sv_ref, csk_ref, csv_ref, fo_ref, so_ref, *, t, tkc, nkc, tks, nks):
    row = lax.broadcasted_iota(jnp.int32, (t, t), 0)
    col = lax.broadcasted_iota(jnp.int32, (t, t), 1)

    qs = _split_heads(fq_ref[...])
    kmax = _max_key_norms(lambda off, n: cfk_ref[:, pl.ds(off, n)], nkc * tkc, tks)
    reach = [_fox_reach(qs[h], kmax[h]) for h in range(2)]
    ck_of_block = lambda h, j: ckc_ref[pl.ds(h * nkc + j, 1), :]

    def fox_cache(j, states):
        off = pl.multiple_of(j * tkc, tkc)
        kj = cfk_ref[:, pl.ds(off, tkc)].astype(BF16)
        vj = cfv_ref[:, pl.ds(off, tkc)].astype(BF16)
        return tuple(
            _fox_update(states[h], qs[h], kj, vj, ck_of_block(h, j), None, transposed=True)
            for h in range(2))

    states = tuple(
        _fox_update(_fox_init(t), qs[h], fk_ref[...], fv_ref[...], ckn_ref[h:h + 1, :], col <= row)
        for h in range(2))
    states = _newest_first(nkc - 1, states, fox_cache,
                           lambda j, st: _fox_may_matter(j, st, reach, ck_of_block))
    outs = [acc / l for (_, l, acc) in states]
    fo_ref[...] = _merge_heads(outs[0], outs[1]).astype(fo_ref.dtype)

    qs2 = _split_heads(sq_ref[...])
    tri_new = _strict_lower(t)
    tri_cache = _strict_lower(tks)
    sstates = tuple(
        _sb_update(_sb_init(t), qs2[h], sk_ref[...], sv_ref[...], tri_new, col < row)
        for h in range(2))

    def sb_cache(j, st):
        off = pl.multiple_of(j * tks, tks)
        kj = csk_ref[:, pl.ds(off, tks)].astype(BF16)
        vj = csv_ref[:, pl.ds(off, tks)].astype(BF16)
        return tuple(_sb_update(st[h], qs2[h], kj, vj, tri_cache, None, transposed=True)
                     for h in range(2))

    sstates = _newest_first(nks - 1, sstates, sb_cache, lambda j, st: _sb_may_matter(st))
    so_ref[...] = _merge_heads(sstates[0][1], sstates[1][1]).astype(so_ref.dtype)


def _ab_sample_attention(fq, fk, fv, cfk, cfv, ck_cache, ck_new, sq, sk, sv, csk, csv,
                         batch, t, past, tkc, tks):
    n, width = fq.shape
    pairs = width // LANES
    nkc = past // tkc
    new_spec = pl.BlockSpec((t, LANES), lambda b, p: (b, p))
    cache_spec = pl.BlockSpec((None, LANES, past), lambda b, p: (b, p, 0))
    ckc_spec = pl.BlockSpec((None, 2 * nkc, tkc), lambda b, p: (b * pairs + p, 0, 0))
    ckn_spec = pl.BlockSpec((None, 2, t), lambda b, p: (b * pairs + p, 0, 0))
    out = jax.ShapeDtypeStruct((n, width), BF16)
    return pl.pallas_call(
        functools.partial(_ab_sample_kernel, t=t, tkc=tkc, nkc=nkc, tks=tks, nks=past // tks),
        grid=(batch, pairs),
        in_specs=[new_spec, new_spec, new_spec, cache_spec, cache_spec, ckc_spec, ckn_spec,
                  new_spec, new_spec, new_spec, cache_spec, cache_spec],
        out_specs=[new_spec, new_spec],
        out_shape=[out, out],
        compiler_params=_params("parallel", "parallel"),
        name="ab_sample",
    )(fq, fk, fv, cfk, cfv, ck_cache, ck_new, sq, sk, sv, csk, csv)


FF_CHUNK = 256


def _tail_kernel(*refs, n_in, final_norm):
    h_ref, p_ref = refs[0], refs[1 + n_in]
    a_refs, wo_refs = refs[1:1 + n_in], refs[2 + n_in:2 + 2 * n_in]
    g1_ref, wg_ref, wu_ref, wd_ref, g2_ref, wpg_ref, wpp_ref, gf_ref, o_ref, act_ref = refs[2 + 2 * n_in:]
    h1 = h_ref[...]
    for a_ref, w_ref in zip(a_refs, wo_refs):
        h1 = h1 + _dot(a_ref[...], w_ref[...])
    o_ref[...] = h1
    xn = _rms(h1, g1_ref[...]).astype(BF16)
    for c in range(wg_ref.shape[1] // FF_CHUNK):
        cols = slice(c * FF_CHUNK, (c + 1) * FF_CHUNK)
        gate = _dot(xn, wg_ref[:, cols])
        up = _dot(xn, wu_ref[:, cols])
        act_ref[:, cols] = (gate * jax.nn.sigmoid(gate) * up).astype(BF16)
    h2 = o_ref[...] + _dot(act_ref[...], wd_ref[...])
    gate = jax.nn.sigmoid(_dot(_rms(h2, g2_ref[...]).astype(BF16), wpg_ref[...]))
    h3 = h2 + _dot(p_ref[...].astype(BF16), wpp_ref[...]) * gate
    o_ref[...] = _rms(h3, gf_ref[...]) if final_norm else h3


def _layer_tail(h, acts, p_all, layer, w_outs, ffn, pe, g_final, final_norm):
    n, d = h.shape
    g1, w_gate, w_up, w_down = ffn
    g2, w_pg, w_pp = pe
    assert w_gate.shape[1] % FF_CHUNK == 0
    tm = _row_tile(n, 512)
    row = lambda i: (i, 0)
    const = lambda i: (0, 0)
    resident = lambda w: pl.BlockSpec(w.shape, const, pipeline_mode=pl.Buffered(1))
    vec = pl.BlockSpec((1, d), const)
    in_specs = [pl.BlockSpec((tm, d), row)]
    in_specs += [pl.BlockSpec((tm, a.shape[1]), row) for a in acts]
    in_specs += [pl.BlockSpec((None, tm, p_all.shape[2]), lambda i: (layer, i, 0))]
    in_specs += [resident(w) for w in w_outs]
    in_specs += [vec, resident(w_gate), resident(w_up), resident(w_down), vec, resident(w_pg),
                 resident(w_pp), vec]
    return pl.pallas_call(
        functools.partial(_tail_kernel, n_in=len(acts), final_norm=final_norm),
        grid=(n // tm,),
        in_specs=in_specs,
        out_specs=pl.BlockSpec((tm, d), row),
        out_shape=jax.ShapeDtypeStruct((n, d), F32),
        scratch_shapes=[pltpu.VMEM((tm, w_gate.shape[1]), BF16)],
        compiler_params=_params("parallel"),
        name="tail_final" if final_norm else "tail",
    )(h, *acts, p_all, *w_outs, g1, w_gate, w_up, w_down, g2, w_pg, w_pp, g_final)


def _rope(y, cos, sin_lo, sin_hi):
    out = []
    for c in range(y.shape[1] // LANES):
        yc = y[:, c * LANES:(c + 1) * LANES]
        out.append(yc * cos + pltpu.roll(yc, 8, 1) * sin_hi + pltpu.roll(yc, LANES - 8, 1) * sin_lo)
    return out[0] if len(out) == 1 else jnp.concatenate(out, axis=1)


def _proj_c_kernel(x_ref, g_ref, w_ref, cos_ref, slo_ref, shi_ref,
                   k_ref, v_ref, qb_ref, kd_ref, vd_ref, *, q_w, kv_w, window_only, nper):
    xn = _rms(x_ref[...], g_ref[...]).astype(BF16)
    tabs = (cos_ref[...], slo_ref[...], shi_ref[...])
    qb_ref[...] = (_rope(_dot(xn, w_ref[:, :q_w]), *tabs) * SCALE).astype(BF16)
    o = q_w + 2 * kv_w
    kd_ref[...] = _rope(_dot(xn, w_ref[:, o:o + 2 * kv_w]), *tabs).astype(BF16)
    o += 2 * kv_w
    vd_ref[...] = _dot(xn, w_ref[:, o:o + 2 * kv_w]).astype(BF16)

    def f32_state():
        k = _rope(_dot(xn, w_ref[:, q_w:q_w + kv_w]), *tabs)
        v = _dot(xn, w_ref[:, q_w + kv_w:q_w + 2 * kv_w])
        k_ref[...] = k[-k_ref.shape[0]:]
        v_ref[...] = v[-v_ref.shape[0]:]

    if window_only:
        pl.when(pl.program_id(0) % nper == nper - 1)(f32_state)
    else:
        f32_state()


def _proj_c(x, g, w_all, tables, period, q_w, kv_w, window_only):
    n, d = x.shape
    tm = _row_tile(period, 512)
    nper = period // tm
    row = lambda i: (i, 0)
    const = lambda i: (0, 0)
    tab = pl.BlockSpec((tm, LANES), lambda i: (i % nper, 0))
    if window_only:
        assert tm >= WINDOW
        state_spec = pl.BlockSpec((None, WINDOW, kv_w), lambda i: (i // nper, 0, 0))
        state_shape = jax.ShapeDtypeStruct((n // period, WINDOW, kv_w), F32)
    else:
        state_spec = pl.BlockSpec((tm, kv_w), row)
        state_shape = jax.ShapeDtypeStruct((n, kv_w), F32)
    return pl.pallas_call(
        functools.partial(_proj_c_kernel, q_w=q_w, kv_w=kv_w, window_only=window_only, nper=nper),
        grid=(n // tm,),
        in_specs=[pl.BlockSpec((tm, d), row), pl.BlockSpec((1, d), const),
                  pl.BlockSpec(w_all.shape, const), tab, tab, tab],
        out_specs=[state_spec, state_spec,
                   pl.BlockSpec((tm, q_w), row), pl.BlockSpec((tm, 2 * kv_w), row),
                   pl.BlockSpec((tm, 2 * kv_w), row)],
        out_shape=[state_shape, state_shape,
                   jax.ShapeDtypeStruct((n, q_w), BF16), jax.ShapeDtypeStruct((n, 2 * kv_w), BF16),
                   jax.ShapeDtypeStruct((n, 2 * kv_w), BF16)],
        compiler_params=_params("arbitrary"),
        name="proj_c",
    )(x, g, w_all, *tables)


def _rope_tables(pos):
    half = ROPE_DIM // 2
    inv_freq = ROPE_THETA ** (-2.0 * jnp.arange(half, dtype=F32) / ROPE_DIM)
    ang = pos.astype(F32)[:, None] * inv_freq[None, :]
    cos, sin = jnp.cos(ang), jnp.sin(ang)
    n = pos.shape[0]
    rest = HEAD_DIM - ROPE_DIM
    cos_h = jnp.concatenate([cos, cos, jnp.ones((n, rest), F32)], axis=1)
    lo_h = jnp.concatenate([-sin, jnp.zeros((n, half + rest), F32)], axis=1)
    hi_h = jnp.concatenate([jnp.zeros((n, half), F32), sin, jnp.zeros((n, rest), F32)], axis=1)
    two = lambda a: jnp.concatenate([a, a], axis=1)
    return two(cos_h), two(lo_h), two(hi_h)


def _swa_kernel(sink_ref, q_ref, kprev_ref, kcur_ref, vprev_ref, vcur_ref, o_ref, kb_ref, vb_ref,
                *, tq, n_kv, group, first_pos_of_tile):
    band = WINDOW + CHUNK
    kb_ref[0:WINDOW, :] = kprev_ref[...]
    kb_ref[WINDOW:, :] = kcur_ref[...]
    vb_ref[0:WINDOW, :] = vprev_ref[...]
    vb_ref[WINDOW:, :] = vcur_ref[...]
    rows = group * CHUNK
    ridx = lax.broadcasted_iota(jnp.int32, (rows, 1), 0) // CHUNK
    col = lax.broadcasted_iota(jnp.int32, (rows, band), 1)
    tile_pos = None if first_pos_of_tile is None else first_pos_of_tile(pl.program_id(1))

    def chunk(cc, _):
        base = pl.multiple_of(cc * CHUNK, CHUNK)
        kv_heads = range(n_kv)
        scores = []
        for kvh in kv_heads:
            lhs = []
            for pr in range(group // 2):
                lane0 = (kvh * (group // 2) + pr) * LANES
                lhs.extend(_split_heads(q_ref[pl.ds(base, CHUNK), lane0:lane0 + LANES]))
            s = _dot_t(jnp.concatenate(lhs, axis=0), kb_ref[pl.ds(base, band), kvh * LANES:(kvh + 1) * LANES])
            if tile_pos is not None:
                s = jnp.where(tile_pos - WINDOW + cc * CHUNK + col >= 0, s, -jnp.inf)
            scores.append(s)
        weights, dens = [], []
        for kvh in kv_heads:
            sink = jnp.zeros((rows, 1), F32)
            for gi in range(group):
                sink = jnp.where(ridx == gi, sink_ref[kvh * group + gi], sink)
            m = jnp.maximum(jnp.max(scores[kvh], axis=-1, keepdims=True), sink)
            e = jnp.exp(scores[kvh] - m)
            dens.append(jnp.sum(e, axis=-1, keepdims=True) + jnp.exp(sink - m))
            weights.append(e.astype(BF16))
        for kvh in kv_heads:
            o = _dot(weights[kvh], vb_ref[pl.ds(base, band), kvh * LANES:(kvh + 1) * LANES]) / dens[kvh]
            for pr in range(group // 2):
                lane0 = (kvh * (group // 2) + pr) * LANES
                pair = _merge_heads(o[(2 * pr) * CHUNK:(2 * pr + 1) * CHUNK],
                                    o[(2 * pr + 1) * CHUNK:(2 * pr + 2) * CHUNK])
                o_ref[pl.ds(base, CHUNK), lane0:lane0 + LANES] = pair.astype(o_ref.dtype)
        return 0

    lax.fori_loop(0, tq // CHUNK, chunk, 0)


def _swa_prompt(sinks, q, kd, vd, batch, seq, n_kv):
    n, q_w = q.shape
    group = q_w // HEAD_DIM // n_kv
    tq = _row_tile(seq, 512)
    nt = seq // tq
    per_win = tq // WINDOW
    cur = lambda b, i: (b * nt + i, 0)
    prev = lambda b, i: (jnp.maximum(b * nt * per_win + i * per_win - 1, b * nt * per_win), 0)
    kv_w = kd.shape[1]
    return pl.pallas_call(
        functools.partial(_swa_kernel, tq=tq, n_kv=n_kv, group=group,
                          first_pos_of_tile=lambda i: i * tq),
        grid=(batch, nt),
        in_specs=[pl.BlockSpec(memory_space=pltpu.SMEM),
                  pl.BlockSpec((tq, q_w), cur),
                  pl.BlockSpec((WINDOW, kv_w), prev), pl.BlockSpec((tq, kv_w), cur),
                  pl.BlockSpec((WINDOW, kv_w), prev), pl.BlockSpec((tq, kv_w), cur)],
        out_specs=pl.BlockSpec((tq, q_w), cur),
        out_shape=jax.ShapeDtypeStruct((n, q_w), BF16),
        scratch_shapes=[pltpu.VMEM((WINDOW + tq, kv_w), BF16), pltpu.VMEM((WINDOW + tq, kv_w), BF16)],
        compiler_params=_params("parallel", "parallel"),
        name="swa_prompt",
    )(sinks, q, kd, kd, vd, vd)


def _swa_sample(sinks, q, kd_cache, kd_new, vd_cache, vd_new, batch, t, n_kv):
    n, q_w = q.shape
    group = q_w // HEAD_DIM // n_kv
    kv_w = kd_new.shape[1]
    blk = lambda b, i: (b, 0)
    return pl.pallas_call(
        functools.partial(_swa_kernel, tq=t, n_kv=n_kv, group=group, first_pos_of_tile=None),
        grid=(batch, 1),
        in_specs=[pl.BlockSpec(memory_space=pltpu.SMEM),
                  pl.BlockSpec((t, q_w), blk),
                  pl.BlockSpec((WINDOW, kv_w), blk), pl.BlockSpec((t, kv_w), blk),
                  pl.BlockSpec((WINDOW, kv_w), blk), pl.BlockSpec((t, kv_w), blk)],
        out_specs=pl.BlockSpec((t, q_w), blk),
        out_shape=jax.ShapeDtypeStruct((n, q_w), BF16),
        scratch_shapes=[pltpu.VMEM((WINDOW + t, kv_w), BF16), pltpu.VMEM((WINDOW + t, kv_w), BF16)],
        compiler_params=_params("parallel", "parallel"),
        name="swa_sample",
    )(sinks, q, kd_cache, kd_new, vd_cache, vd_new)


def _dup_heads(a, n_heads):
    lead = a.shape[:-1]
    a = a.reshape(*lead, n_heads, 1, HEAD_DIM)
    return jnp.broadcast_to(a, (*lead, n_heads, 2, HEAD_DIM)).reshape(*lead, n_heads * LANES)


def _prep_weights(W, n_fox, n_sb, n_kv):
    fox_w = n_fox * HEAD_DIM
    w_in = W["w_ab_in"][0]
    w_main = jnp.concatenate([w_in[:, :3 * fox_w], w_in[:, 3 * fox_w + n_fox:]], axis=1).astype(BF16)
    w_flog = jnp.pad(w_in[:, 3 * fox_w:3 * fox_w + n_fox], ((0, 0), (0, LANES - n_fox))).astype(BF16)
    b_f = jnp.pad(W["b_fox_f"][0][None, :], ((0, 0), (0, LANES - n_fox)))
    fq, fk, fv, flog = (w_in[:, :fox_w], w_in[:, fox_w:2 * fox_w], w_in[:, 2 * fox_w:3 * fox_w],
                        w_in[:, 3 * fox_w:3 * fox_w + n_fox])
    sq, sk, sv = jnp.split(w_in[:, 3 * fox_w + n_fox:], 3, axis=1)
    gate_rows = 16
    w_nt = jnp.concatenate([fq, sq], axis=1).astype(BF16)
    w_t = jnp.concatenate([fk, fv, sk, sv, jnp.pad(flog, ((0, 0), (0, gate_rows - n_fox)))],
                          axis=1).T.astype(BF16)
    b_col = jnp.pad(W["b_fox_f"][0], (0, gate_rows - n_fox))[:, None]
    w_c = W["w_c_in"][0]
    kv_w = n_kv * HEAD_DIM
    q_w = w_c.shape[1] - 2 * kv_w
    w_k, w_v = w_c[:, q_w:q_w + kv_w], w_c[:, q_w + kv_w:]
    w_c_all = jnp.concatenate([w_c, _dup_heads(w_k, n_kv), _dup_heads(w_v, n_kv)], axis=1).astype(BF16)
    bf = lambda name, i: W[name][i].astype(BF16)
    out = dict(w_main=w_main, w_flog=w_flog, b_f=b_f, w_nt=w_nt, w_t=w_t, b_col=b_col,
               w_c_all=w_c_all, q_w=q_w, kv_w=kv_w,
               w_ab_out_f=W["w_ab_out"][0][:fox_w].astype(BF16),
               w_ab_out_s=W["w_ab_out"][0][fox_w:].astype(BF16),
               w_c_out=bf("w_c_out", 0), c_sinks=W["c_sinks"][0])
    for i in range(2):
        out[f"ffn{i}"] = (W["norm_ffn"][i][None, :], bf("w_ffn_gate", i), bf("w_ffn_up", i),
                          bf("w_ffn_down", i))
        out[f"pe{i}"] = (W["norm_pe"][i][None, :], bf("w_pe_gate", i), bf("w_pe_proj", i))
    out["norm_mix"] = [W["norm_mix"][i][None, :] for i in range(2)]
    out["norm_final"] = W["norm_final"][None, :]
    return out


def _pad_lanes(a):
    pad = (-a.shape[1]) % LANES
    return jnp.pad(a, ((0, 0), (0, pad))) if pad else a


def _tail(h, acts, w_outs, p, P, i):
    p_all = p.reshape(p.shape[0], h.shape[0], p.shape[-1])
    return _layer_tail(h, acts, p_all, i, w_outs, P[f"ffn{i}"], P[f"pe{i}"], P["norm_final"],
                       final_norm=(i == p.shape[0] - 1))


def _trunk_prompt(x, p, P, n_fox, n_kv):
    batch, seq, d = x.shape
    n = batch * seq
    h = x.reshape(n, d)
    fkt, fvt, skt, svt, lft, fqb, sqb, fktb, fvtb, sktb, svtb = _proj_ab_t(
        h, P["norm_mix"][0], P["w_nt"], P["w_t"], P["b_col"], n_fox, batch, seq)
    tq_fox = _row_tile(seq, FOX_TILE)
    ck = _cumsum_rows(lft.reshape(batch * n_fox, seq))
    fo = _prompt_attention(fqb, fktb, fvtb, ck, batch, seq, tq_fox)
    so = _prompt_attention(sqb, sktb, svtb, None, batch, seq, _row_tile(seq, SB_TILE))
    h = _tail(h, [fo, so], [P["w_ab_out_f"], P["w_ab_out_s"]], p, P, 0)
    tables = _rope_tables(jnp.arange(seq))
    k, v, qb, kd, vd = _proj_c(h, P["norm_mix"][1], P["w_c_all"], tables, seq, P["q_w"], P["kv_w"],
                               window_only=True)
    o = _swa_prompt(P["c_sinks"], qb, kd, vd, batch, seq, n_kv)
    y = _tail(h, [o], [P["w_c_out"]], p, P, 1)
    heads = lambda a: jnp.transpose(a.reshape(batch, n_fox, HEAD_DIM, seq), (0, 3, 1, 2))[None]
    kw = k.reshape(1, batch, WINDOW, n_kv, HEAD_DIM)
    vw = v.reshape(1, batch, WINDOW, n_kv, HEAD_DIM)
    return y.reshape(batch, seq, d), (heads(fkt), heads(fvt), jnp.transpose(lft, (0, 2, 1))[None],
                                      heads(skt), heads(svt), kw, vw)


def _trunk_sample(x, p, P, caches, n_fox, n_kv):
    c_fk, c_fv, c_fl, c_sk, c_sv, c_wk, c_wv = caches
    batch, t, d = x.shape
    past = c_fk.shape[2]
    n = batch * t
    h = x.reshape(n, d)
    width = n_fox * HEAD_DIM
    fk, fv, sk, sv, logf, fqb, fkb, fvb, sqb, skb, svb = _proj_ab(
        h, P["norm_mix"][0], P["w_main"], P["w_flog"], P["b_f"], n_fox)
    lf_all = jnp.concatenate([jnp.swapaxes(c_fl[0], 1, 2),
                              jnp.swapaxes(logf.reshape(batch, t, n_fox), 1, 2)], axis=2)
    lf_rows = _pad_lanes(lf_all.reshape(batch * n_fox, past + t))
    c_rows = _cumsum_rows(lf_rows)
    tkc = _row_tile(past, 2048)
    ck_cache = c_rows[:, :past].reshape(batch * n_fox // 2, 2 * (past // tkc), tkc)
    ck_new = c_rows[:, past:past + t].reshape(batch * n_fox // 2, 2, t)
    cfk, cfv, csk, csv = [jnp.transpose(c[0], (0, 2, 3, 1)).reshape(batch, width, past)
                          for c in (c_fk, c_fv, c_sk, c_sv)]
    fo, so = _ab_sample_attention(fqb, fkb, fvb, cfk, cfv, ck_cache, ck_new,
                                  sqb, skb, svb, csk, csv, batch, t, past, tkc, _row_tile(past, 256))
    h = _tail(h, [fo, so], [P["w_ab_out_f"], P["w_ab_out_s"]], p, P, 0)
    tm = _row_tile(n, 512)
    pos = past + jnp.arange(t)
    tables = _rope_tables(jnp.tile(pos, max(tm // t, 1)))
    k, v, qb, kd, vd = _proj_c(h, P["norm_mix"][1], P["w_c_all"], tables, tables[0].shape[0],
                               P["q_w"], P["kv_w"], window_only=False)
    kd_cache = _dup_heads(c_wk[0].reshape(batch * WINDOW, n_kv * HEAD_DIM), n_kv).astype(BF16)
    vd_cache = _dup_heads(c_wv[0].reshape(batch * WINDOW, n_kv * HEAD_DIM), n_kv).astype(BF16)
    o = _swa_sample(P["c_sinks"], qb, kd_cache, kd, vd_cache, vd, batch, t, n_kv)
    y = _tail(h, [o], [P["w_c_out"]], p, P, 1)
    heads = lambda a, nh: a.reshape(1, batch, t, nh, HEAD_DIM)
    roll = lambda c, new: jnp.concatenate([c[0], new.reshape(batch, t, n_kv, HEAD_DIM)], axis=1)[None, :, -WINDOW:]
    return y.reshape(batch, t, d), (heads(fk, n_fox), heads(fv, n_fox),
                                    logf.reshape(1, batch, t, n_fox),
                                    heads(sk, n_fox), heads(sv, n_fox), roll(c_wk, k), roll(c_wv, v))


def kernel(x_prompt, x_sample, cache_fox_k, cache_fox_v, cache_fox_logf, cache_sb_k, cache_sb_v, cache_swa_k, cache_swa_v, p_prompt, p_sample, norm_mix, w_ab_in, b_fox_f, w_ab_out, w_c_in, c_sinks, w_c_out, norm_ffn, w_ffn_gate, w_ffn_up, w_ffn_down, norm_pe, w_pe_gate, w_pe_proj, norm_final):
    W = dict(norm_mix=norm_mix, w_ab_in=w_ab_in, b_fox_f=b_fox_f, w_ab_out=w_ab_out, w_c_in=w_c_in,
             c_sinks=c_sinks, w_c_out=w_c_out, norm_ffn=norm_ffn, w_ffn_gate=w_ffn_gate,
             w_ffn_up=w_ffn_up, w_ffn_down=w_ffn_down, norm_pe=norm_pe, w_pe_gate=w_pe_gate,
             w_pe_proj=w_pe_proj, norm_final=norm_final)
    n_fox = cache_fox_k.shape[3]
    n_sb = cache_sb_k.shape[3]
    n_kv = cache_swa_k.shape[3]
    assert n_fox == n_sb and n_fox % 2 == 0 and n_kv % 1 == 0
    assert cache_swa_k.shape[2] == WINDOW and x_sample.shape[1] == CHUNK
    P = _prep_weights(W, n_fox, n_sb, n_kv)
    caches = (cache_fox_k, cache_fox_v, cache_fox_logf, cache_sb_k, cache_sb_v, cache_swa_k, cache_swa_v)
    y_p, st_p = _trunk_prompt(x_prompt, p_prompt, P, n_fox, n_kv)
    y_s, st_s = _trunk_sample(x_sample, p_sample, P, caches, n_fox, n_kv)
    return (y_p, y_s, *st_p, *st_s)
```

```python
import functools

import jax
import jax.numpy as jnp
from jax import lax
from jax.experimental import pallas as pl
from jax.experimental.pallas import tpu as pltpu

F32 = jnp.float32
BF16 = jnp.bfloat16

HEAD_DIM = 64
CHUNK = 64
WINDOW = 128
ROPE_DIM = 16
ROPE_THETA = 500000.0
RMS_EPS = 1e-6
SCALE = HEAD_DIM ** -0.5
SKIP_LOGIT = -110.0
REACH_SLACK = 1.01
LOG2_E = 1.4426950408889634
LN_2 = 0.6931471805599453

LANES = 128
V7X_VMEM_BYTES = 64 * 1024 * 1024
VMEM_LIMIT = V7X_VMEM_BYTES - 8 * 1024 * 1024


def _params(*sem):
    return pltpu.CompilerParams(dimension_semantics=sem, vmem_limit_bytes=VMEM_LIMIT)


def _row_tile(n, pref):
    t = pref
    while n % t:
        t //= 2
    return t


def _rms(x, g):
    ms = jnp.mean(x * x, axis=-1, keepdims=True)
    return x * lax.rsqrt(ms + RMS_EPS) * g


def _dot(a, b):
    return jnp.dot(a, b, preferred_element_type=F32)


def _dot_t(a, b):
    return lax.dot_general(a, b, (((1,), (1,)), ((), ())), preferred_element_type=F32)


def _softplus(z):
    return jnp.maximum(z, 0.0) + jnp.log(1.0 + jnp.exp(-jnp.abs(z)))


def _split_heads(q):
    first = lax.broadcasted_iota(jnp.int32, q.shape, 1) < HEAD_DIM
    zero = jnp.zeros_like(q)
    return jnp.where(first, q, zero), jnp.where(first, zero, q)


def _merge_heads(a, b):
    first = lax.broadcasted_iota(jnp.int32, a.shape, 1) < HEAD_DIM
    return jnp.where(first, a, b)


def _proj_ab_kernel(x_ref, g_ref, w_ref, wf_ref, bf_ref,
                    fk_ref, fv_ref, sk_ref, sv_ref, lf_ref,
                    fqb_ref, fkb_ref, fvb_ref, sqb_ref, skb_ref, svb_ref, *, width):
    xn = _rms(x_ref[...], g_ref[...]).astype(BF16)

    def col(c):
        return _dot(xn, w_ref[:, c * width:(c + 1) * width])

    fqb_ref[...] = (col(0) * SCALE).astype(BF16)
    y = col(1)
    fk_ref[...] = y.reshape(fk_ref.shape)
    fkb_ref[...] = y.astype(BF16)
    y = col(2)
    fv_ref[...] = y.reshape(fv_ref.shape)
    fvb_ref[...] = y.astype(BF16)
    sqb_ref[...] = (col(3) * SCALE).astype(BF16)
    y = col(4)
    sk_ref[...] = y.reshape(sk_ref.shape)
    skb_ref[...] = y.astype(BF16)
    y = col(5)
    sv_ref[...] = y.reshape(sv_ref.shape)
    svb_ref[...] = y.astype(BF16)
    u = _dot(xn, wf_ref[...]) + bf_ref[...]
    lf = -_softplus(-u)
    lf_ref[...] = lf[:, :lf_ref.shape[1]]


def _proj_ab(x, g, w_main, w_f, b_f, n_heads):
    n, d = x.shape
    width = n_heads * HEAD_DIM
    tm = _row_tile(n, 512)
    row = lambda i: (i, 0)
    const = lambda i: (0, 0)
    f32_out = jax.ShapeDtypeStruct((n, n_heads, HEAD_DIM), F32)
    hblk = pl.BlockSpec((tm, n_heads, HEAD_DIM), lambda i: (i, 0, 0))
    bf_out = jax.ShapeDtypeStruct((n, width), BF16)
    blk = pl.BlockSpec((tm, width), row)
    return pl.pallas_call(
        functools.partial(_proj_ab_kernel, width=width),
        grid=(n // tm,),
        in_specs=[pl.BlockSpec((tm, d), row), pl.BlockSpec((1, d), const),
                  pl.BlockSpec(w_main.shape, const), pl.BlockSpec(w_f.shape, const),
                  pl.BlockSpec(b_f.shape, const)],
        out_specs=[hblk, hblk, hblk, hblk, pl.BlockSpec((tm, n_heads), row)] + [blk] * 6,
        out_shape=[f32_out] * 4 + [jax.ShapeDtypeStruct((n, n_heads), F32)] + [bf_out] * 6,
        compiler_params=_params("parallel"),
        name="proj_ab",
    )(x, g, w_main, w_f, b_f)


def _proj_ab_t_kernel(x_ref, g_ref, w_ref, wt_ref, bf_ref,
                      fkt_ref, fvt_ref, skt_ref, svt_ref, lft_ref,
                      fqb_ref, sqb_ref, fktb_ref, fvtb_ref, sktb_ref, svtb_ref, *, width, n_heads):
    xn = _rms(x_ref[...], g_ref[...]).astype(BF16)
    for c, ref in enumerate((fqb_ref, sqb_ref)):
        ref[...] = (_dot(xn, w_ref[:, c * width:(c + 1) * width]) * SCALE).astype(BF16)
    for c, (ref, bf_ref_) in enumerate(((fkt_ref, fktb_ref), (fvt_ref, fvtb_ref), (skt_ref, sktb_ref),
                                        (svt_ref, svtb_ref))):
        yt = _dot_t(wt_ref[c * width:(c + 1) * width, :], xn)
        ref[...] = yt
        bf_ref_[...] = yt.astype(BF16)
    ut = _dot_t(wt_ref[4 * width:, :], xn) + bf_ref[...]
    lft_ref[...] = (-_softplus(-ut))[:n_heads]


def _proj_ab_t(x, g, w_nt, w_t, b_col, n_heads, batch, seq):
    n, d = x.shape
    width = n_heads * HEAD_DIM
    tm = _row_tile(seq, 512)
    per = seq // tm
    row = lambda i: (i, 0)
    const = lambda i: (0, 0)
    t_blk = pl.BlockSpec((None, width, tm), lambda i: (i // per, 0, i % per))
    blk = pl.BlockSpec((tm, width), row)
    t32 = jax.ShapeDtypeStruct((batch, width, seq), F32)
    tbf = jax.ShapeDtypeStruct((batch, width, seq), BF16)
    bf_out = jax.ShapeDtypeStruct((n, width), BF16)
    return pl.pallas_call(
        functools.partial(_proj_ab_t_kernel, width=width, n_heads=n_heads),
        grid=(n // tm,),
        in_specs=[pl.BlockSpec((tm, d), row), pl.BlockSpec((1, d), const),
                  pl.BlockSpec(w_nt.shape, const), pl.BlockSpec(w_t.shape, const),
                  pl.BlockSpec(b_col.shape, const)],
        out_specs=[t_blk] * 4 + [pl.BlockSpec((None, n_heads, tm), lambda i: (i // per, 0, i % per))]
        + [blk] * 2 + [t_blk] * 4,
        out_shape=[t32] * 4 + [jax.ShapeDtypeStruct((batch, n_heads, seq), F32)] + [bf_out] * 2 + [tbf] * 4,
        compiler_params=_params("parallel"),
        name="proj_ab_t",
    )(x, g, w_nt, w_t, b_col)


def _cumsum_kernel(x_ref, o_ref, *, n_chunks):
    r = lax.broadcasted_iota(jnp.int32, (LANES, LANES), 0)
    c = lax.broadcasted_iota(jnp.int32, (LANES, LANES), 1)
    upper = (r <= c).astype(BF16)

    def local(t, _):
        off = pl.multiple_of(t * LANES, LANES)
        x = x_ref[:, pl.ds(off, LANES)]
        x1 = x.astype(BF16)
        r1 = x - x1.astype(F32)
        x2 = r1.astype(BF16)
        x3 = (r1 - x2.astype(F32)).astype(BF16)
        o_ref[:, pl.ds(off, LANES)] = _dot(x1, upper) + _dot(x2, upper) + _dot(x3, upper)
        return 0

    def add_carry(t, carry):
        off = pl.multiple_of(t * LANES, LANES)
        y = o_ref[:, pl.ds(off, LANES)] + carry
        o_ref[:, pl.ds(off, LANES)] = y
        return y[:, LANES - 1:LANES]

    unroll = 8 if n_chunks % 8 == 0 else (n_chunks if n_chunks <= 32 else 1)
    lax.fori_loop(0, n_chunks, local, 0, unroll=unroll)
    lax.fori_loop(0, n_chunks, add_carry, jnp.zeros((x_ref.shape[0], 1), F32), unroll=unroll)


def _cumsum_rows(x):
    r, l = x.shape
    tr = _row_tile(r, 16)
    return pl.pallas_call(
        functools.partial(_cumsum_kernel, n_chunks=l // LANES),
        grid=(r // tr,),
        in_specs=[pl.BlockSpec((tr, l), lambda i: (i, 0))],
        out_specs=pl.BlockSpec((tr, l), lambda i: (i, 0)),
        out_shape=jax.ShapeDtypeStruct((r, l), F32),
        compiler_params=_params("parallel"),
        name="cumsum_logf",
    )(x)


def _qk(qh, kj, transposed):
    return _dot(qh, kj) if transposed else _dot_t(qh, kj)


def _pv(p, vj, transposed):
    return _dot_t(p, vj) if transposed else _dot(p, vj)


def _fox_update(state, qh, kj, vj, ck_row, valid, transposed=False):
    m, l, acc = state
    s = _qk(qh, kj, transposed) - ck_row
    if valid is not None:
        s = jnp.where(valid, s, -jnp.inf)
    m_new = jnp.maximum(m, jnp.max(s, axis=-1, keepdims=True))
    alpha = jnp.exp(m - m_new)
    p = jnp.exp(s - m_new)
    l = alpha * l + jnp.sum(p, axis=-1, keepdims=True)
    acc = alpha * acc + _pv(p.astype(BF16), vj, transposed)
    return m_new, l, acc


def _fox_init(tq):
    return (jnp.full((tq, 1), -jnp.inf, F32), jnp.zeros((tq, 1), F32), jnp.zeros((tq, LANES), F32))


def _sb_update(state, qh, kj, vj, tri, valid, transposed=False):
    carry, acc = state
    z = _qk(qh, kj, transposed)
    log_1m = -_softplus(z)
    if valid is not None:
        log_1m = jnp.where(valid, log_1m, 0.0)
    hi = log_1m.astype(BF16)
    lo = (log_1m - hi.astype(F32)).astype(BF16)
    between = _dot(hi, tri) + _dot(lo, tri)
    a = jnp.exp(z + log_1m + between + carry)
    if valid is not None:
        a = jnp.where(valid, a, 0.0)
    acc = acc + _pv(a.astype(BF16), vj, transposed)
    carry = carry + jnp.sum(log_1m, axis=-1, keepdims=True)
    return carry, acc


def _sb_init(tq):
    return (jnp.zeros((tq, 1), F32), jnp.zeros((tq, LANES), F32))


def _strict_lower(t):
    r = lax.broadcasted_iota(jnp.int32, (t, t), 0)
    c = lax.broadcasted_iota(jnp.int32, (t, t), 1)
    return (r > c).astype(BF16)


def _newest_first(j0, states, step, may_matter):
    def cond(c):
        return jnp.logical_and(c[0] >= 0, c[1])

    def body(c):
        j, _, st = c
        st = step(j, st)
        return j - 1, may_matter(j - 1, st), st

    return lax.while_loop(cond, body, (j0, may_matter(j0, states), states))[2]


def _fox_reach(q_half, kmax):
    qf = q_half.astype(F32)
    return jnp.sqrt(jnp.sum(qf * qf, axis=-1, keepdims=True)) * kmax * REACH_SLACK


def _fox_may_matter(j, states, reach, ck_of_block):
    jc = jnp.maximum(j, 0)
    worst = None
    for h in range(len(states)):
        ck_last = ck_of_block(h, jc)[:, -1:]
        gap = reach[h] - ck_last - states[h][0]
        worst = gap if worst is None else jnp.maximum(worst, gap)
    return jnp.max(worst) >= SKIP_LOGIT


def _sb_may_matter(states):
    return jnp.max(jnp.maximum(states[0][0], states[1][0])) >= SKIP_LOGIT


def _max_key_norms(load_keys, n_keys, chunk):
    def body(t, best):
        kf = load_keys(pl.multiple_of(t * chunk, chunk), chunk).astype(BF16).astype(F32)
        sq = kf * kf
        n0 = jnp.sum(sq[:HEAD_DIM], axis=0, keepdims=True)
        n1 = jnp.sum(sq[HEAD_DIM:], axis=0, keepdims=True)
        return jnp.maximum(best[0], n0), jnp.maximum(best[1], n1)

    zero = jnp.zeros((1, chunk), F32)
    best = lax.fori_loop(0, n_keys // chunk, body, (zero, zero))
    return [jnp.sqrt(jnp.max(b, axis=1, keepdims=True)) for b in best]


STRIP = 16
FOX_TILE = 512
SB_TILE = 256
FOX_STEP_PAIRS = 1
SB_STEP_PAIRS = 2


def _strips(tq, tk, diag):
    for r in range(tq // STRIP):
        hi = (r + 1) * STRIP
        ncol = min(tk, -(-hi // LANES) * LANES) if diag else tk
        yield r * STRIP, slice(r * STRIP, hi), ncol


def _group_mask(row0, g, strict):
    lo = g * LANES
    if lo + LANES - 1 <= row0 - (1 if strict else 0):
        return None
    rr = row0 + lax.broadcasted_iota(jnp.int32, (STRIP, LANES), 0)
    cc = lo + lax.broadcasted_iota(jnp.int32, (STRIP, LANES), 1)
    return cc < rr if strict else cc <= rr


def _row_bcast(x):
    return jnp.broadcast_to(x, (x.shape[0], LANES))


def _fox_block(qs, kjs, vjs, ck_rows, diag, scr):
    s_scr, p_scr, m_scr, l_scr, a_scr, acc_scr = scr
    tq, tk = s_scr.shape[1:]
    heads = range(len(qs))
    for h in heads:
        s_scr[h] = _dot(qs[h], kjs[h // 2])

    def logits(h, row0, rows, ncol):
        out = []
        for g in range(ncol // LANES):
            cols = slice(g * LANES, (g + 1) * LANES)
            sg = s_scr[h, rows, cols] - ck_rows[h][:, cols]
            mask = _group_mask(row0, g, strict=False) if diag else None
            if mask is not None:
                sg = jnp.where(mask, sg, -jnp.inf)
            s_scr[h, rows, cols] = sg
            out.append(sg)
        return out

    def finished_logits(h, rows, ncol):
        return [s_scr[h, rows, g * LANES:(g + 1) * LANES] for g in range(ncol // LANES)]

    for h in heads:
        for row0, rows, ncol in _strips(tq, tk, diag):
            part = functools.reduce(jnp.maximum, logits(h, row0, rows, ncol))
            m_new = _row_bcast(jnp.max(part, axis=-1, keepdims=True))
            if not diag:
                m_old = m_scr[h, rows, :]
                m_new = jnp.maximum(m_old, m_new)
                a_scr[h, rows, :] = jnp.exp(m_old - m_new)
            m_scr[h, rows, :] = m_new
    for h in heads:
        for row0, rows, ncol in _strips(tq, tk, diag):
            m_new = m_scr[h, rows, :]
            pg = [jnp.exp(sg - m_new) for sg in finished_logits(h, rows, ncol)]
            lsum = functools.reduce(jnp.add, pg)
            if diag:
                l_scr[h, rows, :] = lsum
            else:
                alpha = a_scr[h, rows, :]
                l_scr[h, rows, :] = alpha * l_scr[h, rows, :] + lsum
                acc_scr[h, rows, :] = alpha * acc_scr[h, rows, :]
            pg = [p.astype(BF16) for p in pg]
            pg += [jnp.zeros((STRIP, LANES), BF16)] * ((tk - ncol) // LANES)
            p_scr[h, rows, :] = jnp.concatenate(pg, axis=1)
    for h in heads:
        pv = _dot_t(p_scr[h], vjs[h // 2])
        acc_scr[h] = pv if diag else acc_scr[h] + pv


def _log_1m_sigmoid(z):
    t = jnp.log2(1.0 + jnp.exp2(jnp.abs(z) * (-LOG2_E)))
    return t * (-LN_2) - jnp.maximum(z, 0.0)


def _sb_block(qs, kjs, vjs, tri, diag, scr):
    z_scr, hi_scr, lo_scr, b_scr, p_scr, c_scr, d_scr, acc_scr = scr
    tq, tk = z_scr.shape[1:]
    zero = jnp.zeros((STRIP, LANES), BF16)
    heads = range(len(qs))
    for h in heads:
        z_scr[h] = _dot(qs[h], kjs[h // 2])
    for h in heads:
        for row0, rows, ncol in _strips(tq, tk, diag):
            his, los, dsum = [], [], None
            for g in range(ncol // LANES):
                cols = slice(g * LANES, (g + 1) * LANES)
                z = z_scr[h, rows, cols]
                log_1m = _log_1m_sigmoid(z)
                mask = _group_mask(row0, g, strict=True) if diag else None
                if mask is not None:
                    log_1m = jnp.where(mask, log_1m, 0.0)
                hi = log_1m.astype(BF16)
                his.append(hi)
                los.append((log_1m - hi.astype(F32)).astype(BF16))
                z_scr[h, rows, cols] = z + log_1m
                dsum = log_1m if dsum is None else dsum + log_1m
            pad = [zero] * ((tk - ncol) // LANES)
            hi_scr[h, rows, :] = jnp.concatenate(his + pad, axis=1)
            lo_scr[h, rows, :] = jnp.concatenate(los + pad, axis=1)
            d_scr[h, rows, :] = dsum
    for h in heads:
        b_scr[h] = _dot(hi_scr[h], tri) + _dot(lo_scr[h], tri)
    for h in heads:
        for row0, rows, ncol in _strips(tq, tk, diag):
            carry = None if diag else c_scr[h, rows, :]
            pg = []
            for g in range(ncol // LANES):
                cols = slice(g * LANES, (g + 1) * LANES)
                e = z_scr[h, rows, cols] + b_scr[h, rows, cols]
                a = jnp.exp(e if carry is None else e + carry)
                mask = _group_mask(row0, g, strict=True) if diag else None
                pg.append((a if mask is None else jnp.where(mask, a, 0.0)).astype(BF16))
            p_scr[h, rows, :] = jnp.concatenate(pg + [zero] * ((tk - ncol) // LANES), axis=1)
    for h in heads:
        pv = _dot_t(p_scr[h], vjs[h // 2])
        acc_scr[h] = pv if diag else acc_scr[h] + pv
        d = _row_bcast(jnp.sum(d_scr[h], axis=-1, keepdims=True))
        c_scr[h] = d if diag else c_scr[h] + d


def _fox_prompt_kernel(q_ref, k_ref, v_ref, ck_ref, o_ref, kmax_ref, *scr, tq, nk, pairs):
    i = pl.program_id(2)
    lanes = lambda pr: slice(pr * LANES, (pr + 1) * LANES)
    heads = range(2 * pairs)

    @pl.when(i == 0)
    def _():
        for pr in range(pairs):
            norms = _max_key_norms(lambda off, n: k_ref[lanes(pr), pl.ds(off, n)], nk * tq, tq)
            for hh in range(2):
                kmax_ref[2 * pr + hh] = jnp.broadcast_to(norms[hh], kmax_ref.shape[1:])

    qs = [qh for pr in range(pairs) for qh in _split_heads(q_ref[:, lanes(pr)])]
    reach = [_fox_reach(qs[h], kmax_ref[h, 0:1, 0:1]) for h in heads]
    ck_of_block = lambda h, j: ck_ref[pl.ds(h * nk + j, 1), :]
    m_scr = scr[2]

    def step(j, carry, diag=False):
        off = pl.multiple_of(j * tq, tq)
        kjs = [k_ref[lanes(pr), pl.ds(off, tq)] for pr in range(pairs)]
        vjs = [v_ref[lanes(pr), pl.ds(off, tq)] for pr in range(pairs)]
        _fox_block(qs, kjs, vjs, [ck_of_block(h, j) for h in heads], diag, scr)
        return carry

    def may_matter(j, carry):
        return _fox_may_matter(j, [(m_scr[h],) for h in heads], reach, ck_of_block)

    step(i, 0, diag=True)
    _newest_first(i - 1, 0, step, may_matter)
    l_scr, acc_scr = scr[3], scr[5]
    outs = [acc_scr[h] / jnp.sum(l_scr[h], axis=-1, keepdims=True) for h in heads]
    for pr in range(pairs):
        o_ref[:, lanes(pr)] = _merge_heads(outs[2 * pr], outs[2 * pr + 1]).astype(o_ref.dtype)


def _sb_prompt_kernel(q_ref, k_ref, v_ref, o_ref, *scr, tq, pairs):
    i = pl.program_id(2)
    lanes = lambda pr: slice(pr * LANES, (pr + 1) * LANES)
    qs = [qh for pr in range(pairs) for qh in _split_heads(q_ref[:, lanes(pr)])]
    tri = _strict_lower(tq)
    c_scr, acc_scr = scr[5], scr[7]

    def step(j, carry, diag=False):
        off = pl.multiple_of(j * tq, tq)
        kjs = [k_ref[lanes(pr), pl.ds(off, tq)] for pr in range(pairs)]
        vjs = [v_ref[lanes(pr), pl.ds(off, tq)] for pr in range(pairs)]
        _sb_block(qs, kjs, vjs, tri, diag, scr)
        return carry

    def may_matter(j, carry):
        worst = functools.reduce(jnp.maximum, [c_scr[h] for h in range(2 * pairs)])
        return jnp.max(worst) >= SKIP_LOGIT

    step(i, 0, diag=True)
    _newest_first(i - 1, 0, step, may_matter)
    for pr in range(pairs):
        o_ref[:, lanes(pr)] = _merge_heads(acc_scr[2 * pr], acc_scr[2 * pr + 1]).astype(o_ref.dtype)


def _prompt_attention(q, k, v, ck, batch, seq, tq):
    n, width = q.shape
    pairs = width // LANES
    nq = seq // tq
    want = SB_STEP_PAIRS if ck is None else FOX_STEP_PAIRS
    step_pairs = want if pairs % want == 0 else 1
    wide = step_pairs * LANES
    q_spec = pl.BlockSpec((tq, wide), lambda b, p, i: (b * nq + i, p))
    k_spec = pl.BlockSpec((None, wide, seq), lambda b, p, i: (b, p, 0))
    v_spec = k_spec
    heads = 2 * step_pairs
    tile = lambda dt: pltpu.VMEM((heads, tq, tq), dt)
    acc = pltpu.VMEM((heads, tq, LANES), F32)
    column = acc
    if ck is None:
        kernel = functools.partial(_sb_prompt_kernel, tq=tq, pairs=step_pairs)
        in_specs, args, name = [q_spec, k_spec, v_spec], (q, k, v), "sb_prompt"
        scratch = [tile(F32), tile(BF16), tile(BF16), tile(F32), tile(BF16), column, column, acc]
    else:
        kernel = functools.partial(_fox_prompt_kernel, tq=tq, nk=nq, pairs=step_pairs)
        steps = pairs // step_pairs
        ck = ck.reshape(batch * steps, heads * nq, tq)
        ck_spec = pl.BlockSpec((None, heads * nq, tq), lambda b, p, i: (b * steps + p, 0, 0))
        in_specs, args, name = [q_spec, k_spec, v_spec, ck_spec], (q, k, v, ck), "fox_prompt"
        scratch = [pltpu.VMEM((heads, 8, LANES), F32), tile(F32), tile(BF16), column, column, column, acc]
    return pl.pallas_call(
        kernel,
        grid=(batch, pairs // step_pairs, nq),
        in_specs=in_specs,
        out_specs=q_spec,
        out_shape=jax.ShapeDtypeStruct((n, width), BF16),
        scratch_shapes=scratch,
        compiler_params=_params("parallel", "parallel", "arbitrary"),
        name=name,
    )(*args)


def _ab_sample_kernel(fq_ref, fk_ref, fv_ref, cfk_ref, cfv_ref, ckc_ref, ckn_ref,
                      sq_ref, sk_ref, sv_ref, csk_ref, csv_ref, fo_ref, so_ref, *, t, tkc, nkc, tks, nks):
    row = lax.broadcasted_iota(jnp.int32, (t, t), 0)
    col = lax.broadcasted_iota(jnp.int32, (t, t), 1)

    qs = _split_heads(fq_ref[...])
    kmax = _max_key_norms(lambda off, n: cfk_ref[:, pl.ds(off, n)], nkc * tkc, tks)
    reach = [_fox_reach(qs[h], kmax[h]) for h in range(2)]
    ck_of_block = lambda h, j: ckc_ref[pl.ds(h * nkc + j, 1), :]

    def fox_cache(j, states):
        off = pl.multiple_of(j * tkc, tkc)
        kj = cfk_ref[:, pl.ds(off, tkc)].astype(BF16)
        vj = cfv_ref[:, pl.ds(off, tkc)].astype(BF16)
        return tuple(
            _fox_update(states[h], qs[h], kj, vj, ck_of_block(h, j), None, transposed=True)
            for h in range(2))

    states = tuple(
        _fox_update(_fox_init(t), qs[h], fk_ref[...], fv_ref[...], ckn_ref[h:h + 1, :], col <= row)
        for h in range(2))
    states = _newest_first(nkc - 1, states, fox_cache,
                           lambda j, st: _fox_may_matter(j, st, reach, ck_of_block))
    outs = [acc / l for (_, l, acc) in states]
    fo_ref[...] = _merge_heads(outs[0], outs[1]).astype(fo_ref.dtype)

    qs2 = _split_heads(sq_ref[...])
    tri_new = _strict_lower(t)
    tri_cache = _strict_lower(tks)
    sstates = tuple(
        _sb_update(_sb_init(t), qs2[h], sk_ref[...], sv_ref[...], tri_new, col < row)
        for h in range(2))

    def sb_cache(j, st):
        off = pl.multiple_of(j * tks, tks)
        kj = csk_ref[:, pl.ds(off, tks)].astype(BF16)
        vj = csv_ref[:, pl.ds(off, tks)].astype(BF16)
        return tuple(_sb_update(st[h], qs2[h], kj, vj, tri_cache, None, transposed=True)
                     for h in range(2))

    sstates = _newest_first(nks - 1, sstates, sb_cache, lambda j, st: _sb_may_matter(st))
    so_ref[...] = _merge_heads(sstates[0][1], sstates[1][1]).astype(so_ref.dtype)


def _ab_sample_attention(fq, fk, fv, cfk, cfv, ck_cache, ck_new, sq, sk, sv, csk, csv,
                         batch, t, past, tkc, tks):
    n, width = fq.shape
    pairs = width // LANES
    nkc = past // tkc
    new_spec = pl.BlockSpec((t, LANES), lambda b, p: (b, p))
    cache_spec = pl.BlockSpec((None, LANES, past), lambda b, p: (b, p, 0))
    ckc_spec = pl.BlockSpec((None, 2 * nkc, tkc), lambda b, p: (b * pairs + p, 0, 0))
    ckn_spec = pl.BlockSpec((None, 2, t), lambda b, p: (b * pairs + p, 0, 0))
    out = jax.ShapeDtypeStruct((n, width), BF16)
    return pl.pallas_call(
        functools.partial(_ab_sample_kernel, t=t, tkc=tkc, nkc=nkc, tks=tks, nks=past // tks),
        grid=(batch, pairs),
        in_specs=[new_spec, new_spec, new_spec, cache_spec, cache_spec, ckc_spec, ckn_spec,
                  new_spec, new_spec, new_spec, cache_spec, cache_spec],
        out_specs=[new_spec, new_spec],
        out_shape=[out, out],
        compiler_params=_params("parallel", "parallel"),
        name="ab_sample",
    )(fq, fk, fv, cfk, cfv, ck_cache, ck_new, sq, sk, sv, csk, csv)


FF_CHUNK = 256


def _tail_kernel(*refs, n_in, final_norm):
    h_ref, p_ref = refs[0], refs[1 + n_in]
    a_refs, wo_refs = refs[1:1 + n_in], refs[2 + n_in:2 + 2 * n_in]
    g1_ref, wg_ref, wu_ref, wd_ref, g2_ref, wpg_ref, wpp_ref, gf_ref, o_ref, act_ref = refs[2 + 2 * n_in:]
    h1 = h_ref[...]
    for a_ref, w_ref in zip(a_refs, wo_refs):
        h1 = h1 + _dot(a_ref[...], w_ref[...])
    o_ref[...] = h1
    xn = _rms(h1, g1_ref[...]).astype(BF16)
    for c in range(wg_ref.shape[1] // FF_CHUNK):
        cols = slice(c * FF_CHUNK, (c + 1) * FF_CHUNK)
        gate = _dot(xn, wg_ref[:, cols])
        up = _dot(xn, wu_ref[:, cols])
        act_ref[:, cols] = (gate * jax.nn.sigmoid(gate) * up).astype(BF16)
    h2 = o_ref[...] + _dot(act_ref[...], wd_ref[...])
    gate = jax.nn.sigmoid(_dot(_rms(h2, g2_ref[...]).astype(BF16), wpg_ref[...]))
    h3 = h2 + _dot(p_ref[...].astype(BF16), wpp_ref[...]) * gate
    o_ref[...] = _rms(h3, gf_ref[...]) if final_norm else h3


def _layer_tail(h, acts, p_all, layer, w_outs, ffn, pe, g_final, final_norm):
    n, d = h.shape
    g1, w_gate, w_up, w_down = ffn
    g2, w_pg, w_pp = pe
    assert w_gate.shape[1] % FF_CHUNK == 0
    tm = _row_tile(n, 512)
    row = lambda i: (i, 0)
    const = lambda i: (0, 0)
    resident = lambda w: pl.BlockSpec(w.shape, const, pipeline_mode=pl.Buffered(1))
    vec = pl.BlockSpec((1, d), const)
    in_specs = [pl.BlockSpec((tm, d), row)]
    in_specs += [pl.BlockSpec((tm, a.shape[1]), row) for a in acts]
    in_specs += [pl.BlockSpec((None, tm, p_all.shape[2]), lambda i: (layer, i, 0))]
    in_specs += [resident(w) for w in w_outs]
    in_specs += [vec, resident(w_gate), resident(w_up), resident(w_down), vec, resident(w_pg),
                 resident(w_pp), vec]
    return pl.pallas_call(
        functools.partial(_tail_kernel, n_in=len(acts), final_norm=final_norm),
        grid=(n // tm,),
        in_specs=in_specs,
        out_specs=pl.BlockSpec((tm, d), row),
        out_shape=jax.ShapeDtypeStruct((n, d), F32),
        scratch_shapes=[pltpu.VMEM((tm, w_gate.shape[1]), BF16)],
        compiler_params=_params("parallel"),
        name="tail_final" if final_norm else "tail",
    )(h, *acts, p_all, *w_outs, g1, w_gate, w_up, w_down, g2, w_pg, w_pp, g_final)


def _rope(y, cos, sin_lo, sin_hi):
    out = []
    for c in range(y.shape[1] // LANES):
        yc = y[:, c * LANES:(c + 1) * LANES]
        out.append(yc * cos + pltpu.roll(yc, 8, 1) * sin_hi + pltpu.roll(yc, LANES - 8, 1) * sin_lo)
    return out[0] if len(out) == 1 else jnp.concatenate(out, axis=1)


def _proj_c_kernel(x_ref, g_ref, w_ref, cos_ref, slo_ref, shi_ref,
                   k_ref, v_ref, qb_ref, kd_ref, vd_ref, *, q_w, kv_w, window_only, nper):
    xn = _rms(x_ref[...], g_ref[...]).astype(BF16)
    tabs = (cos_ref[...], slo_ref[...], shi_ref[...])
    qb_ref[...] = (_rope(_dot(xn, w_ref[:, :q_w]), *tabs) * SCALE).astype(BF16)
    o = q_w + 2 * kv_w
    kd_ref[...] = _rope(_dot(xn, w_ref[:, o:o + 2 * kv_w]), *tabs).astype(BF16)
    o += 2 * kv_w
    vd_ref[...] = _dot(xn, w_ref[:, o:o + 2 * kv_w]).astype(BF16)

    def f32_state():
        k = _rope(_dot(xn, w_ref[:, q_w:q_w + kv_w]), *tabs)
        v = _dot(xn, w_ref[:, q_w + kv_w:q_w + 2 * kv_w])
        k_ref[...] = k[-k_ref.shape[0]:]
        v_ref[...] = v[-v_ref.shape[0]:]

    if window_only:
        pl.when(pl.program_id(0) % nper == nper - 1)(f32_state)
    else:
        f32_state()


def _proj_c(x, g, w_all, tables, period, q_w, kv_w, window_only):
    n, d = x.shape
    tm = _row_tile(period, 512)
    nper = period // tm
    row = lambda i: (i, 0)
    const = lambda i: (0, 0)
    tab = pl.BlockSpec((tm, LANES), lambda i: (i % nper, 0))
    if window_only:
        assert tm >= WINDOW
        state_spec = pl.BlockSpec((None, WINDOW, kv_w), lambda i: (i // nper, 0, 0))
        state_shape = jax.ShapeDtypeStruct((n // period, WINDOW, kv_w), F32)
    else:
        state_spec = pl.BlockSpec((tm, kv_w), row)
        state_shape = jax.ShapeDtypeStruct((n, kv_w), F32)
    return pl.pallas_call(
        functools.partial(_proj_c_kernel, q_w=q_w, kv_w=kv_w, window_only=window_only, nper=nper),
        grid=(n // tm,),
        in_specs=[pl.BlockSpec((tm, d), row), pl.BlockSpec((1, d), const),
                  pl.BlockSpec(w_all.shape, const), tab, tab, tab],
        out_specs=[state_spec, state_spec,
                   pl.BlockSpec((tm, q_w), row), pl.BlockSpec((tm, 2 * kv_w), row),
                   pl.BlockSpec((tm, 2 * kv_w), row)],
        out_shape=[state_shape, state_shape,
                   jax.ShapeDtypeStruct((n, q_w), BF16), jax.ShapeDtypeStruct((n, 2 * kv_w), BF16),
                   jax.ShapeDtypeStruct((n, 2 * kv_w), BF16)],
        compiler_params=_params("arbitrary"),
        name="proj_c",
    )(x, g, w_all, *tables)


def _rope_tables(pos):
    half = ROPE_DIM // 2
    inv_freq = ROPE_THETA ** (-2.0 * jnp.arange(half, dtype=F32) / ROPE_DIM)
    ang = pos.astype(F32)[:, None] * inv_freq[None, :]
    cos, sin = jnp.cos(ang), jnp.sin(ang)
    n = pos.shape[0]
    rest = HEAD_DIM - ROPE_DIM
    cos_h = jnp.concatenate([cos, cos, jnp.ones((n, rest), F32)], axis=1)
    lo_h = jnp.concatenate([-sin, jnp.zeros((n, half + rest), F32)], axis=1)
    hi_h = jnp.concatenate([jnp.zeros((n, half), F32), sin, jnp.zeros((n, rest), F32)], axis=1)
    two = lambda a: jnp.concatenate([a, a], axis=1)
    return two(cos_h), two(lo_h), two(hi_h)


SWA_UNROLL = 4


def _swa_kernel(sink_ref, q_ref, kprev_ref, kcur_ref, vprev_ref, vcur_ref, o_ref, kb_ref, vb_ref,
                *, tq, n_kv, group, first_pos_of_tile):
    band = WINDOW + CHUNK
    kb_ref[0:WINDOW, :] = kprev_ref[...]
    kb_ref[WINDOW:, :] = kcur_ref[...]
    vb_ref[0:WINDOW, :] = vprev_ref[...]
    vb_ref[WINDOW:, :] = vcur_ref[...]
    rows = group * CHUNK
    ridx = lax.broadcasted_iota(jnp.int32, (rows, 1), 0) // CHUNK
    col = lax.broadcasted_iota(jnp.int32, (rows, band), 1)
    tile_pos = None if first_pos_of_tile is None else first_pos_of_tile(pl.program_id(1))

    def chunk(cc, _):
        base = pl.multiple_of(cc * CHUNK, CHUNK)
        kv_heads = range(n_kv)
        scores = []
        for kvh in kv_heads:
            lhs = []
            for pr in range(group // 2):
                lane0 = (kvh * (group // 2) + pr) * LANES
                lhs.extend(_split_heads(q_ref[pl.ds(base, CHUNK), lane0:lane0 + LANES]))
            s = _dot_t(jnp.concatenate(lhs, axis=0), kb_ref[pl.ds(base, band), kvh * LANES:(kvh + 1) * LANES])
            if tile_pos is not None:
                s = jnp.where(tile_pos - WINDOW + cc * CHUNK + col >= 0, s, -jnp.inf)
            scores.append(s)
        weights, dens = [], []
        for kvh in kv_heads:
            sink = jnp.zeros((rows, 1), F32)
            for gi in range(group):
                sink = jnp.where(ridx == gi, sink_ref[kvh * group + gi], sink)
            m = jnp.maximum(jnp.max(scores[kvh], axis=-1, keepdims=True), sink)
            e = jnp.exp(scores[kvh] - m)
            dens.append(jnp.sum(e, axis=-1, keepdims=True) + jnp.exp(sink - m))
            weights.append(e.astype(BF16))
        for kvh in kv_heads:
            o = _dot(weights[kvh], vb_ref[pl.ds(base, band), kvh * LANES:(kvh + 1) * LANES]) / dens[kvh]
            for pr in range(group // 2):
                lane0 = (kvh * (group // 2) + pr) * LANES
                pair = _merge_heads(o[(2 * pr) * CHUNK:(2 * pr + 1) * CHUNK],
                                    o[(2 * pr + 1) * CHUNK:(2 * pr + 2) * CHUNK])
                o_ref[pl.ds(base, CHUNK), lane0:lane0 + LANES] = pair.astype(o_ref.dtype)
        return 0

    n_chunks = tq // CHUNK
    lax.fori_loop(0, n_chunks, chunk, 0, unroll=SWA_UNROLL if n_chunks % SWA_UNROLL == 0 else 1)


def _swa_prompt(sinks, q, kd, vd, batch, seq, n_kv):
    n, q_w = q.shape
    group = q_w // HEAD_DIM // n_kv
    tq = _row_tile(seq, 512)
    nt = seq // tq
    per_win = tq // WINDOW
    cur = lambda b, i: (b * nt + i, 0)
    prev = lambda b, i: (jnp.maximum(b * nt * per_win + i * per_win - 1, b * nt * per_win), 0)
    kv_w = kd.shape[1]
    return pl.pallas_call(
        functools.partial(_swa_kernel, tq=tq, n_kv=n_kv, group=group,
                          first_pos_of_tile=lambda i: i * tq),
        grid=(batch, nt),
        in_specs=[pl.BlockSpec(memory_space=pltpu.SMEM),
                  pl.BlockSpec((tq, q_w), cur),
                  pl.BlockSpec((WINDOW, kv_w), prev), pl.BlockSpec((tq, kv_w), cur),
                  pl.BlockSpec((WINDOW, kv_w), prev), pl.BlockSpec((tq, kv_w), cur)],
        out_specs=pl.BlockSpec((tq, q_w), cur),
        out_shape=jax.ShapeDtypeStruct((n, q_w), BF16),
        scratch_shapes=[pltpu.VMEM((WINDOW + tq, kv_w), BF16), pltpu.VMEM((WINDOW + tq, kv_w), BF16)],
        compiler_params=_params("parallel", "parallel"),
        name="swa_prompt",
    )(sinks, q, kd, kd, vd, vd)


def _swa_sample(sinks, q, kd_cache, kd_new, vd_cache, vd_new, batch, t, n_kv):
    n, q_w = q.shape
    group = q_w // HEAD_DIM // n_kv
    kv_w = kd_new.shape[1]
    blk = lambda b, i: (b, 0)
    return pl.pallas_call(
        functools.partial(_swa_kernel, tq=t, n_kv=n_kv, group=group, first_pos_of_tile=None),
        grid=(batch, 1),
        in_specs=[pl.BlockSpec(memory_space=pltpu.SMEM),
                  pl.BlockSpec((t, q_w), blk),
                  pl.BlockSpec((WINDOW, kv_w), blk), pl.BlockSpec((t, kv_w), blk),
                  pl.BlockSpec((WINDOW, kv_w), blk), pl.BlockSpec((t, kv_w), blk)],
        out_specs=pl.BlockSpec((t, q_w), blk),
        out_shape=jax.ShapeDtypeStruct((n, q_w), BF16),
        scratch_shapes=[pltpu.VMEM((WINDOW + t, kv_w), BF16), pltpu.VMEM((WINDOW + t, kv_w), BF16)],
        compiler_params=_params("parallel", "parallel"),
        name="swa_sample",
    )(sinks, q, kd_cache, kd_new, vd_cache, vd_new)


def _dup_heads(a, n_heads):
    lead = a.shape[:-1]
    a = a.reshape(*lead, n_heads, 1, HEAD_DIM)
    return jnp.broadcast_to(a, (*lead, n_heads, 2, HEAD_DIM)).reshape(*lead, n_heads * LANES)


def _prep_weights(W, n_fox, n_sb, n_kv):
    fox_w = n_fox * HEAD_DIM
    w_in = W["w_ab_in"][0]
    w_main = jnp.concatenate([w_in[:, :3 * fox_w], w_in[:, 3 * fox_w + n_fox:]], axis=1).astype(BF16)
    w_flog = jnp.pad(w_in[:, 3 * fox_w:3 * fox_w + n_fox], ((0, 0), (0, LANES - n_fox))).astype(BF16)
    b_f = jnp.pad(W["b_fox_f"][0][None, :], ((0, 0), (0, LANES - n_fox)))
    fq, fk, fv, flog = (w_in[:, :fox_w], w_in[:, fox_w:2 * fox_w], w_in[:, 2 * fox_w:3 * fox_w],
                        w_in[:, 3 * fox_w:3 * fox_w + n_fox])
    sq, sk, sv = jnp.split(w_in[:, 3 * fox_w + n_fox:], 3, axis=1)
    gate_rows = 16
    w_nt = jnp.concatenate([fq, sq], axis=1).astype(BF16)
    w_t = jnp.concatenate([fk, fv, sk, sv, jnp.pad(flog, ((0, 0), (0, gate_rows - n_fox)))],
                          axis=1).T.astype(BF16)
    b_col = jnp.pad(W["b_fox_f"][0], (0, gate_rows - n_fox))[:, None]
    w_c = W["w_c_in"][0]
    kv_w = n_kv * HEAD_DIM
    q_w = w_c.shape[1] - 2 * kv_w
    w_k, w_v = w_c[:, q_w:q_w + kv_w], w_c[:, q_w + kv_w:]
    w_c_all = jnp.concatenate([w_c, _dup_heads(w_k, n_kv), _dup_heads(w_v, n_kv)], axis=1).astype(BF16)
    bf = lambda name, i: W[name][i].astype(BF16)
    out = dict(w_main=w_main, w_flog=w_flog, b_f=b_f, w_nt=w_nt, w_t=w_t, b_col=b_col,
               w_c_all=w_c_all, q_w=q_w, kv_w=kv_w,
               w_ab_out_f=W["w_ab_out"][0][:fox_w].astype(BF16),
               w_ab_out_s=W["w_ab_out"][0][fox_w:].astype(BF16),
               w_c_out=bf("w_c_out", 0), c_sinks=W["c_sinks"][0])
    for i in range(2):
        out[f"ffn{i}"] = (W["norm_ffn"][i][None, :], bf("w_ffn_gate", i), bf("w_ffn_up", i),
                          bf("w_ffn_down", i))
        out[f"pe{i}"] = (W["norm_pe"][i][None, :], bf("w_pe_gate", i), bf("w_pe_proj", i))
    out["norm_mix"] = [W["norm_mix"][i][None, :] for i in range(2)]
    out["norm_final"] = W["norm_final"][None, :]
    return out


def _pad_lanes(a):
    pad = (-a.shape[1]) % LANES
    return jnp.pad(a, ((0, 0), (0, pad))) if pad else a


def _tail(h, acts, w_outs, p, P, i):
    p_all = p.reshape(p.shape[0], h.shape[0], p.shape[-1])
    return _layer_tail(h, acts, p_all, i, w_outs, P[f"ffn{i}"], P[f"pe{i}"], P["norm_final"],
                       final_norm=(i == p.shape[0] - 1))


def _trunk_prompt(x, p, P, n_fox, n_kv):
    batch, seq, d = x.shape
    n = batch * seq
    h = x.reshape(n, d)
    fkt, fvt, skt, svt, lft, fqb, sqb, fktb, fvtb, sktb, svtb = _proj_ab_t(
        h, P["norm_mix"][0], P["w_nt"], P["w_t"], P["b_col"], n_fox, batch, seq)
    tq_fox = _row_tile(seq, FOX_TILE)
    ck = _cumsum_rows(lft.reshape(batch * n_fox, seq))
    fo = _prompt_attention(fqb, fktb, fvtb, ck, batch, seq, tq_fox)
    so = _prompt_attention(sqb, sktb, svtb, None, batch, seq, _row_tile(seq, SB_TILE))
    h = _tail(h, [fo, so], [P["w_ab_out_f"], P["w_ab_out_s"]], p, P, 0)
    tables = _rope_tables(jnp.arange(seq))
    k, v, qb, kd, vd = _proj_c(h, P["norm_mix"][1], P["w_c_all"], tables, seq, P["q_w"], P["kv_w"],
                               window_only=True)
    o = _swa_prompt(P["c_sinks"], qb, kd, vd, batch, seq, n_kv)
    y = _tail(h, [o], [P["w_c_out"]], p, P, 1)
    heads = lambda a: jnp.transpose(a.reshape(batch, n_fox, HEAD_DIM, seq), (0, 3, 1, 2))[None]
    kw = k.reshape(1, batch, WINDOW, n_kv, HEAD_DIM)
    vw = v.reshape(1, batch, WINDOW, n_kv, HEAD_DIM)
    return y.reshape(batch, seq, d), (heads(fkt), heads(fvt), jnp.transpose(lft, (0, 2, 1))[None],
                                      heads(skt), heads(svt), kw, vw)


def _trunk_sample(x, p, P, caches, n_fox, n_kv):
    c_fk, c_fv, c_fl, c_sk, c_sv, c_wk, c_wv = caches
    batch, t, d = x.shape
    past = c_fk.shape[2]
    n = batch * t
    h = x.reshape(n, d)
    width = n_fox * HEAD_DIM
    fk, fv, sk, sv, logf, fqb, fkb, fvb, sqb, skb, svb = _proj_ab(
        h, P["norm_mix"][0], P["w_main"], P["w_flog"], P["b_f"], n_fox)
    lf_all = jnp.concatenate([jnp.swapaxes(c_fl[0], 1, 2),
                              jnp.swapaxes(logf.reshape(batch, t, n_fox), 1, 2)], axis=2)
    lf_rows = _pad_lanes(lf_all.reshape(batch * n_fox, past + t))
    c_rows = _cumsum_rows(lf_rows)
    tkc = _row_tile(past, 2048)
    ck_cache = c_rows[:, :past].reshape(batch * n_fox // 2, 2 * (past // tkc), tkc)
    ck_new = c_rows[:, past:past + t].reshape(batch * n_fox // 2, 2, t)
    cfk, cfv, csk, csv = [jnp.transpose(c[0], (0, 2, 3, 1)).reshape(batch, width, past)
                          for c in (c_fk, c_fv, c_sk, c_sv)]
    fo, so = _ab_sample_attention(fqb, fkb, fvb, cfk, cfv, ck_cache, ck_new,
                                  sqb, skb, svb, csk, csv, batch, t, past, tkc, _row_tile(past, 256))
    h = _tail(h, [fo, so], [P["w_ab_out_f"], P["w_ab_out_s"]], p, P, 0)
    tm = _row_tile(n, 512)
    pos = past + jnp.arange(t)
    tables = _rope_tables(jnp.tile(pos, max(tm // t, 1)))
    k, v, qb, kd, vd = _proj_c(h, P["norm_mix"][1], P["w_c_all"], tables, tables[0].shape[0],
                               P["q_w"], P["kv_w"], window_only=False)
    kd_cache = _dup_heads(c_wk[0].reshape(batch * WINDOW, n_kv * HEAD_DIM), n_kv).astype(BF16)
    vd_cache = _dup_heads(c_wv[0].reshape(batch * WINDOW, n_kv * HEAD_DIM), n_kv).astype(BF16)
    o = _swa_sample(P["c_sinks"], qb, kd_cache, kd, vd_cache, vd, batch, t, n_kv)
    y = _tail(h, [o], [P["w_c_out"]], p, P, 1)
    heads = lambda a, nh: a.reshape(1, batch, t, nh, HEAD_DIM)
    roll = lambda c, new: jnp.concatenate([c[0], new.reshape(batch, t, n_kv, HEAD_DIM)], axis=1)[None, :, -WINDOW:]
    return y.reshape(batch, t, d), (heads(fk, n_fox), heads(fv, n_fox),
                                    logf.reshape(1, batch, t, n_fox),
                                    heads(sk, n_fox), heads(sv, n_fox), roll(c_wk, k), roll(c_wv, v))


def kernel(x_prompt, x_sample, cache_fox_k, cache_fox_v, cache_fox_logf, cache_sb_k, cache_sb_v, cache_swa_k, cache_swa_v, p_prompt, p_sample, norm_mix, w_ab_in, b_fox_f, w_ab_out, w_c_in, c_sinks, w_c_out, norm_ffn, w_ffn_gate, w_ffn_up, w_ffn_down, norm_pe, w_pe_gate, w_pe_proj, norm_final):
    W = dict(norm_mix=norm_mix, w_ab_in=w_ab_in, b_fox_f=b_fox_f, w_ab_out=w_ab_out, w_c_in=w_c_in,
             c_sinks=c_sinks, w_c_out=w_c_out, norm_ffn=norm_ffn, w_ffn_gate=w_ffn_gate,
             w_ffn_up=w_ffn_up, w_ffn_down=w_ffn_down, norm_pe=norm_pe, w_pe_gate=w_pe_gate,
             w_pe_proj=w_pe_proj, norm_final=norm_final)
    n_fox = cache_fox_k.shape[3]
    n_sb = cache_sb_k.shape[3]
    n_kv = cache_swa_k.shape[3]
    assert n_fox == n_sb and n_fox % 2 == 0 and n_kv % 1 == 0
    assert cache_swa_k.shape[2] == WINDOW and x_sample.shape[1] == CHUNK
    P = _prep_weights(W, n_fox, n_sb, n_kv)
    caches = (cache_fox_k, cache_fox_v, cache_fox_logf, cache_sb_k, cache_sb_v, cache_swa_k, cache_swa_v)
    y_p, st_p = _trunk_prompt(x_prompt, p_prompt, P, n_fox, n_kv)
    y_s, st_s = _trunk_sample(x_sample, p_sample, P, caches, n_fox, n_kv)
    return (y_p, y_s, *st_p, *st_s)
```

```python
import functools

import jax
import jax.numpy as jnp
from jax import lax
from jax.experimental import pallas as pl
from jax.experimental.pallas import tpu as pltpu

F32 = jnp.float32
BF16 = jnp.bfloat16

HEAD_DIM = 64
CHUNK = 64
WINDOW = 128
ROPE_DIM = 16
ROPE_THETA = 500000.0
RMS_EPS = 1e-6
SCALE = HEAD_DIM ** -0.5
SKIP_LOGIT = -110.0
REACH_SLACK = 1.01
LOG2_E = 1.4426950408889634
LN_2 = 0.6931471805599453

LANES = 128
V7X_VMEM_BYTES = 64 * 1024 * 1024
VMEM_LIMIT = V7X_VMEM_BYTES - 8 * 1024 * 1024


def _params(*sem):
    return pltpu.CompilerParams(dimension_semantics=sem, vmem_limit_bytes=VMEM_LIMIT)


def _row_tile(n, pref):
    t = pref
    while n % t:
        t //= 2
    return t


def _rms(x, g):
    ms = jnp.mean(x * x, axis=-1, keepdims=True)
    return x * lax.rsqrt(ms + RMS_EPS) * g


def _dot(a, b):
    return jnp.dot(a, b, preferred_element_type=F32)


def _dot_t(a, b):
    return lax.dot_general(a, b, (((1,), (1,)), ((), ())), preferred_element_type=F32)


def _softplus(z):
    return jnp.maximum(z, 0.0) + jnp.log(1.0 + jnp.exp(-jnp.abs(z)))


def _split_heads(q):
    first = lax.broadcasted_iota(jnp.int32, q.shape, 1) < HEAD_DIM
    zero = jnp.zeros_like(q)
    return jnp.where(first, q, zero), jnp.where(first, zero, q)


def _merge_heads(a, b):
    first = lax.broadcasted_iota(jnp.int32, a.shape, 1) < HEAD_DIM
    return jnp.where(first, a, b)


def _proj_ab_kernel(x_ref, g_ref, w_ref, wf_ref, bf_ref,
                    fk_ref, fv_ref, sk_ref, sv_ref, lf_ref,
                    fqb_ref, fkb_ref, fvb_ref, sqb_ref, skb_ref, svb_ref, *, width):
    xn = _rms(x_ref[...], g_ref[...]).astype(BF16)

    def col(c):
        return _dot(xn, w_ref[:, c * width:(c + 1) * width])

    fqb_ref[...] = (col(0) * SCALE).astype(BF16)
    y = col(1)
    fk_ref[...] = y.reshape(fk_ref.shape)
    fkb_ref[...] = y.astype(BF16)
    y = col(2)
    fv_ref[...] = y.reshape(fv_ref.shape)
    fvb_ref[...] = y.astype(BF16)
    sqb_ref[...] = (col(3) * SCALE).astype(BF16)
    y = col(4)
    sk_ref[...] = y.reshape(sk_ref.shape)
    skb_ref[...] = y.astype(BF16)
    y = col(5)
    sv_ref[...] = y.reshape(sv_ref.shape)
    svb_ref[...] = y.astype(BF16)
    u = _dot(xn, wf_ref[...]) + bf_ref[...]
    lf = -_softplus(-u)
    lf_ref[...] = lf[:, :lf_ref.shape[1]]


def _proj_ab(x, g, w_main, w_f, b_f, n_heads):
    n, d = x.shape
    width = n_heads * HEAD_DIM
    tm = _row_tile(n, 512)
    row = lambda i: (i, 0)
    const = lambda i: (0, 0)
    f32_out = jax.ShapeDtypeStruct((n, n_heads, HEAD_DIM), F32)
    hblk = pl.BlockSpec((tm, n_heads, HEAD_DIM), lambda i: (i, 0, 0))
    bf_out = jax.ShapeDtypeStruct((n, width), BF16)
    blk = pl.BlockSpec((tm, width), row)
    return pl.pallas_call(
        functools.partial(_proj_ab_kernel, width=width),
        grid=(n // tm,),
        in_specs=[pl.BlockSpec((tm, d), row), pl.BlockSpec((1, d), const),
                  pl.BlockSpec(w_main.shape, const), pl.BlockSpec(w_f.shape, const),
                  pl.BlockSpec(b_f.shape, const)],
        out_specs=[hblk, hblk, hblk, hblk, pl.BlockSpec((tm, n_heads), row)] + [blk] * 6,
        out_shape=[f32_out] * 4 + [jax.ShapeDtypeStruct((n, n_heads), F32)] + [bf_out] * 6,
        compiler_params=_params("parallel"),
        name="proj_ab",
    )(x, g, w_main, w_f, b_f)


def _proj_ab_t_kernel(x_ref, g_ref, w_ref, wt_ref, bf_ref,
                      fkt_ref, fvt_ref, skt_ref, svt_ref, lft_ref,
                      fqb_ref, sqb_ref, fktb_ref, fvtb_ref, sktb_ref, svtb_ref, *, width, n_heads):
    xn = _rms(x_ref[...], g_ref[...]).astype(BF16)
    for c, ref in enumerate((fqb_ref, sqb_ref)):
        ref[...] = (_dot(xn, w_ref[:, c * width:(c + 1) * width]) * SCALE).astype(BF16)
    for c, (ref, bf_ref_) in enumerate(((fkt_ref, fktb_ref), (fvt_ref, fvtb_ref), (skt_ref, sktb_ref),
                                        (svt_ref, svtb_ref))):
        yt = _dot_t(wt_ref[c * width:(c + 1) * width, :], xn)
        ref[...] = yt
        bf_ref_[...] = yt.astype(BF16)
    ut = _dot_t(wt_ref[4 * width:, :], xn) + bf_ref[...]
    lft_ref[...] = (-_softplus(-ut))[:n_heads]


def _proj_ab_t(x, g, w_nt, w_t, b_col, n_heads, batch, seq):
    n, d = x.shape
    width = n_heads * HEAD_DIM
    tm = _row_tile(seq, 512)
    per = seq // tm
    row = lambda i: (i, 0)
    const = lambda i: (0, 0)
    t_blk = pl.BlockSpec((None, width, tm), lambda i: (i // per, 0, i % per))
    blk = pl.BlockSpec((tm, width), row)
    t32 = jax.ShapeDtypeStruct((batch, width, seq), F32)
    tbf = jax.ShapeDtypeStruct((batch, width, seq), BF16)
    bf_out = jax.ShapeDtypeStruct((n, width), BF16)
    return pl.pallas_call(
        functools.partial(_proj_ab_t_kernel, width=width, n_heads=n_heads),
        grid=(n // tm,),
        in_specs=[pl.BlockSpec((tm, d), row), pl.BlockSpec((1, d), const),
                  pl.BlockSpec(w_nt.shape, const), pl.BlockSpec(w_t.shape, const),
                  pl.BlockSpec(b_col.shape, const)],
        out_specs=[t_blk] * 4 + [pl.BlockSpec((None, n_heads, tm), lambda i: (i // per, 0, i % per))]
        + [blk] * 2 + [t_blk] * 4,
        out_shape=[t32] * 4 + [jax.ShapeDtypeStruct((batch, n_heads, seq), F32)] + [bf_out] * 2 + [tbf] * 4,
        compiler_params=_params("parallel"),
        name="proj_ab_t",
    )(x, g, w_nt, w_t, b_col)


def _cumsum_kernel(x_ref, o_ref, *, n_chunks):
    r = lax.broadcasted_iota(jnp.int32, (LANES, LANES), 0)
    c = lax.broadcasted_iota(jnp.int32, (LANES, LANES), 1)
    upper = (r <= c).astype(BF16)

    def local(t, _):
        off = pl.multiple_of(t * LANES, LANES)
        x = x_ref[:, pl.ds(off, LANES)]
        x1 = x.astype(BF16)
        r1 = x - x1.astype(F32)
        x2 = r1.astype(BF16)
        x3 = (r1 - x2.astype(F32)).astype(BF16)
        o_ref[:, pl.ds(off, LANES)] = _dot(x1, upper) + _dot(x2, upper) + _dot(x3, upper)
        return 0

    def add_carry(t, carry):
        off = pl.multiple_of(t * LANES, LANES)
        y = o_ref[:, pl.ds(off, LANES)] + carry
        o_ref[:, pl.ds(off, LANES)] = y
        return y[:, LANES - 1:LANES]

    unroll = 8 if n_chunks % 8 == 0 else (n_chunks if n_chunks <= 32 else 1)
    lax.fori_loop(0, n_chunks, local, 0, unroll=unroll)
    lax.fori_loop(0, n_chunks, add_carry, jnp.zeros((x_ref.shape[0], 1), F32), unroll=unroll)


def _cumsum_rows(x):
    r, l = x.shape
    tr = _row_tile(r, 16)
    return pl.pallas_call(
        functools.partial(_cumsum_kernel, n_chunks=l // LANES),
        grid=(r // tr,),
        in_specs=[pl.BlockSpec((tr, l), lambda i: (i, 0))],
        out_specs=pl.BlockSpec((tr, l), lambda i: (i, 0)),
        out_shape=jax.ShapeDtypeStruct((r, l), F32),
        compiler_params=_params("parallel"),
        name="cumsum_logf",
    )(x)


def _qk(qh, kj, transposed):
    return _dot(qh, kj) if transposed else _dot_t(qh, kj)


def _pv(p, vj, transposed):
    return _dot_t(p, vj) if transposed else _dot(p, vj)


def _fox_update(state, qh, kj, vj, ck_row, valid, transposed=False):
    m, l, acc = state
    s = _qk(qh, kj, transposed) - ck_row
    if valid is not None:
        s = jnp.where(valid, s, -jnp.inf)
    m_new = jnp.maximum(m, jnp.max(s, axis=-1, keepdims=True))
    alpha = jnp.exp(m - m_new)
    p = jnp.exp(s - m_new)
    l = alpha * l + jnp.sum(p, axis=-1, keepdims=True)
    acc = alpha * acc + _pv(p.astype(BF16), vj, transposed)
    return m_new, l, acc


def _fox_init(tq):
    return (jnp.full((tq, 1), -jnp.inf, F32), jnp.zeros((tq, 1), F32), jnp.zeros((tq, LANES), F32))


def _sb_update(state, qh, kj, vj, tri, valid, transposed=False):
    carry, acc = state
    z = _qk(qh, kj, transposed)
    log_1m = -_softplus(z)
    if valid is not None:
        log_1m = jnp.where(valid, log_1m, 0.0)
    hi = log_1m.astype(BF16)
    lo = (log_1m - hi.astype(F32)).astype(BF16)
    between = _dot(hi, tri) + _dot(lo, tri)
    a = jnp.exp(z + log_1m + between + carry)
    if valid is not None:
        a = jnp.where(valid, a, 0.0)
    acc = acc + _pv(a.astype(BF16), vj, transposed)
    carry = carry + jnp.sum(log_1m, axis=-1, keepdims=True)
    return carry, acc


def _sb_init(tq):
    return (jnp.zeros((tq, 1), F32), jnp.zeros((tq, LANES), F32))


def _strict_lower(t):
    r = lax.broadcasted_iota(jnp.int32, (t, t), 0)
    c = lax.broadcasted_iota(jnp.int32, (t, t), 1)
    return (r > c).astype(BF16)


def _newest_first(j0, states, step, may_matter):
    def cond(c):
        return jnp.logical_and(c[0] >= 0, c[1])

    def body(c):
        j, _, st = c
        st = step(j, st)
        return j - 1, may_matter(j - 1, st), st

    return lax.while_loop(cond, body, (j0, may_matter(j0, states), states))[2]


def _fox_reach(q_half, kmax):
    qf = q_half.astype(F32)
    return jnp.sqrt(jnp.sum(qf * qf, axis=-1, keepdims=True)) * kmax * REACH_SLACK


def _fox_may_matter(j, states, reach, ck_of_block):
    jc = jnp.maximum(j, 0)
    worst = None
    for h in range(len(states)):
        ck_last = ck_of_block(h, jc)[:, -1:]
        gap = reach[h] - ck_last - states[h][0]
        worst = gap if worst is None else jnp.maximum(worst, gap)
    return jnp.max(worst) >= SKIP_LOGIT


def _sb_may_matter(states):
    return jnp.max(jnp.maximum(states[0][0], states[1][0])) >= SKIP_LOGIT


def _max_key_norms(load_keys, n_keys, chunk):
    def body(t, best):
        kf = load_keys(pl.multiple_of(t * chunk, chunk), chunk).astype(BF16).astype(F32)
        sq = kf * kf
        n0 = jnp.sum(sq[:HEAD_DIM], axis=0, keepdims=True)
        n1 = jnp.sum(sq[HEAD_DIM:], axis=0, keepdims=True)
        return jnp.maximum(best[0], n0), jnp.maximum(best[1], n1)

    zero = jnp.zeros((1, chunk), F32)
    best = lax.fori_loop(0, n_keys // chunk, body, (zero, zero))
    return [jnp.sqrt(jnp.max(b, axis=1, keepdims=True)) for b in best]


STRIP = 16
FOX_TILE = 512
SB_TILE = 256
FOX_STEP_PAIRS = 1
SB_STEP_PAIRS = 2


def _strips(tq, tk, diag):
    for r in range(tq // STRIP):
        hi = (r + 1) * STRIP
        ncol = min(tk, -(-hi // LANES) * LANES) if diag else tk
        yield r * STRIP, slice(r * STRIP, hi), ncol


def _group_mask(row0, g, strict):
    lo = g * LANES
    if lo + LANES - 1 <= row0 - (1 if strict else 0):
        return None
    rr = row0 + lax.broadcasted_iota(jnp.int32, (STRIP, LANES), 0)
    cc = lo + lax.broadcasted_iota(jnp.int32, (STRIP, LANES), 1)
    return cc < rr if strict else cc <= rr


def _row_bcast(x):
    return jnp.broadcast_to(x, (x.shape[0], LANES))


def _fox_block(qs, kjs, vjs, ck_rows, diag, scr):
    s_scr, p_scr, m_scr, l_scr, a_scr, acc_scr = scr
    tq, tk = s_scr.shape[1:]
    heads = range(len(qs))
    for h in heads:
        s_scr[h] = _dot(qs[h], kjs[h // 2])

    def logits(h, row0, rows, ncol):
        out = []
        for g in range(ncol // LANES):
            cols = slice(g * LANES, (g + 1) * LANES)
            sg = s_scr[h, rows, cols] - ck_rows[h][:, cols]
            mask = _group_mask(row0, g, strict=False) if diag else None
            if mask is not None:
                sg = jnp.where(mask, sg, -jnp.inf)
            s_scr[h, rows, cols] = sg
            out.append(sg)
        return out

    def finished_logits(h, rows, ncol):
        return [s_scr[h, rows, g * LANES:(g + 1) * LANES] for g in range(ncol // LANES)]

    for h in heads:
        for row0, rows, ncol in _strips(tq, tk, diag):
            part = functools.reduce(jnp.maximum, logits(h, row0, rows, ncol))
            m_new = _row_bcast(jnp.max(part, axis=-1, keepdims=True))
            if not diag:
                m_old = m_scr[h, rows, :]
                m_new = jnp.maximum(m_old, m_new)
                a_scr[h, rows, :] = jnp.exp(m_old - m_new)
            m_scr[h, rows, :] = m_new
    for h in heads:
        for row0, rows, ncol in _strips(tq, tk, diag):
            m_new = m_scr[h, rows, :]
            pg = [jnp.exp(sg - m_new) for sg in finished_logits(h, rows, ncol)]
            lsum = functools.reduce(jnp.add, pg)
            if diag:
                l_scr[h, rows, :] = lsum
            else:
                alpha = a_scr[h, rows, :]
                l_scr[h, rows, :] = alpha * l_scr[h, rows, :] + lsum
                acc_scr[h, rows, :] = alpha * acc_scr[h, rows, :]
            pg = [p.astype(BF16) for p in pg]
            pg += [jnp.zeros((STRIP, LANES), BF16)] * ((tk - ncol) // LANES)
            p_scr[h, rows, :] = jnp.concatenate(pg, axis=1)
    for h in heads:
        pv = _dot_t(p_scr[h], vjs[h // 2])
        acc_scr[h] = pv if diag else acc_scr[h] + pv


def _log_1m_sigmoid(z):
    t = jnp.log2(1.0 + jnp.exp2(jnp.abs(z) * (-LOG2_E)))
    return t * (-LN_2) - jnp.maximum(z, 0.0)


def _sb_block(qs, kjs, vjs, tri, diag, scr):
    z_scr, hi_scr, lo_scr, b_scr, p_scr, c_scr, d_scr, acc_scr = scr
    tq, tk = z_scr.shape[1:]
    zero = jnp.zeros((STRIP, LANES), BF16)
    heads = range(len(qs))
    for h in heads:
        z_scr[h] = _dot(qs[h], kjs[h // 2])
    for h in heads:
        for row0, rows, ncol in _strips(tq, tk, diag):
            his, los, dsum = [], [], None
            for g in range(ncol // LANES):
                cols = slice(g * LANES, (g + 1) * LANES)
                z = z_scr[h, rows, cols]
                log_1m = _log_1m_sigmoid(z)
                mask = _group_mask(row0, g, strict=True) if diag else None
                if mask is not None:
                    log_1m = jnp.where(mask, log_1m, 0.0)
                hi = log_1m.astype(BF16)
                his.append(hi)
                los.append((log_1m - hi.astype(F32)).astype(BF16))
                z_scr[h, rows, cols] = z + log_1m
                dsum = log_1m if dsum is None else dsum + log_1m
            pad = [zero] * ((tk - ncol) // LANES)
            hi_scr[h, rows, :] = jnp.concatenate(his + pad, axis=1)
            lo_scr[h, rows, :] = jnp.concatenate(los + pad, axis=1)
            d_scr[h, rows, :] = dsum
    for h in heads:
        b_scr[h] = _dot(hi_scr[h], tri) + _dot(lo_scr[h], tri)
    for h in heads:
        for row0, rows, ncol in _strips(tq, tk, diag):
            carry = None if diag else c_scr[h, rows, :]
            pg = []
            for g in range(ncol // LANES):
                cols = slice(g * LANES, (g + 1) * LANES)
                e = z_scr[h, rows, cols] + b_scr[h, rows, cols]
                a = jnp.exp(e if carry is None else e + carry)
                mask = _group_mask(row0, g, strict=True) if diag else None
                pg.append((a if mask is None else jnp.where(mask, a, 0.0)).astype(BF16))
            p_scr[h, rows, :] = jnp.concatenate(pg + [zero] * ((tk - ncol) // LANES), axis=1)
    for h in heads:
        pv = _dot_t(p_scr[h], vjs[h // 2])
        acc_scr[h] = pv if diag else acc_scr[h] + pv
        d = _row_bcast(jnp.sum(d_scr[h], axis=-1, keepdims=True))
        c_scr[h] = d if diag else c_scr[h] + d


def _fox_prompt_kernel(q_ref, k_ref, v_ref, ck_ref, o_ref, kmax_ref, *scr, tq, nk, pairs):
    i = pl.program_id(2)
    lanes = lambda pr: slice(pr * LANES, (pr + 1) * LANES)
    heads = range(2 * pairs)

    @pl.when(i == 0)
    def _():
        for pr in range(pairs):
            norms = _max_key_norms(lambda off, n: k_ref[lanes(pr), pl.ds(off, n)], nk * tq, tq)
            for hh in range(2):
                kmax_ref[2 * pr + hh] = jnp.broadcast_to(norms[hh], kmax_ref.shape[1:])

    qs = [qh for pr in range(pairs) for qh in _split_heads(q_ref[:, lanes(pr)])]
    reach = [_fox_reach(qs[h], kmax_ref[h, 0:1, 0:1]) for h in heads]
    ck_of_block = lambda h, j: ck_ref[pl.ds(h * nk + j, 1), :]
    m_scr = scr[2]

    def step(j, carry, diag=False):
        off = pl.multiple_of(j * tq, tq)
        kjs = [k_ref[lanes(pr), pl.ds(off, tq)] for pr in range(pairs)]
        vjs = [v_ref[lanes(pr), pl.ds(off, tq)] for pr in range(pairs)]
        _fox_block(qs, kjs, vjs, [ck_of_block(h, j) for h in heads], diag, scr)
        return carry

    def may_matter(j, carry):
        return _fox_may_matter(j, [(m_scr[h],) for h in heads], reach, ck_of_block)

    step(i, 0, diag=True)
    _newest_first(i - 1, 0, step, may_matter)
    l_scr, acc_scr = scr[3], scr[5]
    outs = [acc_scr[h] / jnp.sum(l_scr[h], axis=-1, keepdims=True) for h in heads]
    for pr in range(pairs):
        o_ref[:, lanes(pr)] = _merge_heads(outs[2 * pr], outs[2 * pr + 1]).astype(o_ref.dtype)


def _sb_prompt_kernel(q_ref, k_ref, v_ref, o_ref, *scr, tq, pairs):
    i = pl.program_id(2)
    lanes = lambda pr: slice(pr * LANES, (pr + 1) * LANES)
    qs = [qh for pr in range(pairs) for qh in _split_heads(q_ref[:, lanes(pr)])]
    tri = _strict_lower(tq)
    c_scr, acc_scr = scr[5], scr[7]

    def step(j, carry, diag=False):
        off = pl.multiple_of(j * tq, tq)
        kjs = [k_ref[lanes(pr), pl.ds(off, tq)] for pr in range(pairs)]
        vjs = [v_ref[lanes(pr), pl.ds(off, tq)] for pr in range(pairs)]
        _sb_block(qs, kjs, vjs, tri, diag, scr)
        return carry

    def may_matter(j, carry):
        worst = functools.reduce(jnp.maximum, [c_scr[h] for h in range(2 * pairs)])
        return jnp.max(worst) >= SKIP_LOGIT

    step(i, 0, diag=True)
    _newest_first(i - 1, 0, step, may_matter)
    for pr in range(pairs):
        o_ref[:, lanes(pr)] = _merge_heads(acc_scr[2 * pr], acc_scr[2 * pr + 1]).astype(o_ref.dtype)


def _prompt_attention(q, k, v, ck, batch, seq, tq):
    n, width = q.shape
    pairs = width // LANES
    nq = seq // tq
    want = SB_STEP_PAIRS if ck is None else FOX_STEP_PAIRS
    step_pairs = want if pairs % want == 0 else 1
    wide = step_pairs * LANES
    q_spec = pl.BlockSpec((tq, wide), lambda b, p, i: (b * nq + i, p))
    k_spec = pl.BlockSpec((None, wide, seq), lambda b, p, i: (b, p, 0))
    v_spec = k_spec
    heads = 2 * step_pairs
    tile = lambda dt: pltpu.VMEM((heads, tq, tq), dt)
    acc = pltpu.VMEM((heads, tq, LANES), F32)
    column = acc
    if ck is None:
        kernel = functools.partial(_sb_prompt_kernel, tq=tq, pairs=step_pairs)
        in_specs, args, name = [q_spec, k_spec, v_spec], (q, k, v), "sb_prompt"
        scratch = [tile(F32), tile(BF16), tile(BF16), tile(F32), tile(BF16), column, column, acc]
    else:
        kernel = functools.partial(_fox_prompt_kernel, tq=tq, nk=nq, pairs=step_pairs)
        steps = pairs // step_pairs
        ck = ck.reshape(batch * steps, heads * nq, tq)
        ck_spec = pl.BlockSpec((None, heads * nq, tq), lambda b, p, i: (b * steps + p, 0, 0))
        in_specs, args, name = [q_spec, k_spec, v_spec, ck_spec], (q, k, v, ck), "fox_prompt"
        scratch = [pltpu.VMEM((heads, 8, LANES), F32), tile(F32), tile(BF16), column, column, column, acc]
    return pl.pallas_call(
        kernel,
        grid=(batch, pairs // step_pairs, nq),
        in_specs=in_specs,
        out_specs=q_spec,
        out_shape=jax.ShapeDtypeStruct((n, width), BF16),
        scratch_shapes=scratch,
        compiler_params=_params("parallel", "parallel", "arbitrary"),
        name=name,
    )(*args)


def _ab_sample_kernel(fq_ref, fk_ref, fv_ref, cfk_ref, cfv_ref, ckc_ref, ckn_ref,
                      sq_ref, sk_ref, sv_ref, csk_ref, csv_ref, fo_ref, so_ref, *, t, tkc, nkc, tks, nks):
    row = lax.broadcasted_iota(jnp.int32, (t, t), 0)
    col = lax.broadcasted_iota(jnp.int32, (t, t), 1)

    qs = _split_heads(fq_ref[...])
    kmax = _max_key_norms(lambda off, n: cfk_ref[:, pl.ds(off, n)], nkc * tkc, tks)
    reach = [_fox_reach(qs[h], kmax[h]) for h in range(2)]
    ck_of_block = lambda h, j: ckc_ref[pl.ds(h * nkc + j, 1), :]

    def fox_cache(j, states):
        off = pl.multiple_of(j * tkc, tkc)
        kj = cfk_ref[:, pl.ds(off, tkc)].astype(BF16)
        vj = cfv_ref[:, pl.ds(off, tkc)].astype(BF16)
        return tuple(
            _fox_update(states[h], qs[h], kj, vj, ck_of_block(h, j), None, transposed=True)
            for h in range(2))

    states = tuple(
        _fox_update(_fox_init(t), qs[h], fk_ref[...], fv_ref[...], ckn_ref[h:h + 1, :], col <= row)
        for h in range(2))
    states = _newest_first(nkc - 1, states, fox_cache,
                           lambda j, st: _fox_may_matter(j, st, reach, ck_of_block))
    outs = [acc / l for (_, l, acc) in states]
    fo_ref[...] = _merge_heads(outs[0], outs[1]).astype(fo_ref.dtype)

    qs2 = _split_heads(sq_ref[...])
    tri_new = _strict_lower(t)
    tri_cache = _strict_lower(tks)
    sstates = tuple(
        _sb_update(_sb_init(t), qs2[h], sk_ref[...], sv_ref[...], tri_new, col < row)
        for h in range(2))

    def sb_cache(j, st):
        off = pl.multiple_of(j * tks, tks)
        kj = csk_ref[:, pl.ds(off, tks)].astype(BF16)
        vj = csv_ref[:, pl.ds(off, tks)].astype(BF16)
        return tuple(_sb_update(st[h], qs2[h], kj, vj, tri_cache, None, transposed=True)
                     for h in range(2))

    sstates = _newest_first(nks - 1, sstates, sb_cache, lambda j, st: _sb_may_matter(st))
    so_ref[...] = _merge_heads(sstates[0][1], sstates[1][1]).astype(so_ref.dtype)


def _ab_sample_attention(fq, fk, fv, cfk, cfv, ck_cache, ck_new, sq, sk, sv, csk, csv,
                         batch, t, past, tkc, tks):
    n, width = fq.shape
    pairs = width // LANES
    nkc = past // tkc
    new_spec = pl.BlockSpec((t, LANES), lambda b, p: (b, p))
    cache_spec = pl.BlockSpec((None, LANES, past), lambda b, p: (b, p, 0))
    ckc_spec = pl.BlockSpec((None, 2 * nkc, tkc), lambda b, p: (b * pairs + p, 0, 0))
    ckn_spec = pl.BlockSpec((None, 2, t), lambda b, p: (b * pairs + p, 0, 0))
    out = jax.ShapeDtypeStruct((n, width), BF16)
    return pl.pallas_call(
        functools.partial(_ab_sample_kernel, t=t, tkc=tkc, nkc=nkc, tks=tks, nks=past // tks),
        grid=(batch, pairs),
        in_specs=[new_spec, new_spec, new_spec, cache_spec, cache_spec, ckc_spec, ckn_spec,
                  new_spec, new_spec, new_spec, cache_spec, cache_spec],
        out_specs=[new_spec, new_spec],
        out_shape=[out, out],
        compiler_params=_params("parallel", "parallel"),
        name="ab_sample",
    )(fq, fk, fv, cfk, cfv, ck_cache, ck_new, sq, sk, sv, csk, csv)


FF_CHUNK = 256


def _tail_kernel(*refs, n_in, final_norm):
    h_ref, p_ref = refs[0], refs[1 + n_in]
    a_refs, wo_refs = refs[1:1 + n_in], refs[2 + n_in:2 + 2 * n_in]
    g1_ref, wg_ref, wu_ref, wd_ref, g2_ref, wpg_ref, wpp_ref, gf_ref, o_ref, act_ref = refs[2 + 2 * n_in:]
    h1 = h_ref[...]
    for a_ref, w_ref in zip(a_refs, wo_refs):
        h1 = h1 + _dot(a_ref[...], w_ref[...])
    o_ref[...] = h1
    xn = _rms(h1, g1_ref[...]).astype(BF16)
    for c in range(wg_ref.shape[1] // FF_CHUNK):
        cols = slice(c * FF_CHUNK, (c + 1) * FF_CHUNK)
        gate = _dot(xn, wg_ref[:, cols])
        up = _dot(xn, wu_ref[:, cols])
        act_ref[:, cols] = (gate * jax.nn.sigmoid(gate) * up).astype(BF16)
    h2 = o_ref[...] + _dot(act_ref[...], wd_ref[...])
    gate = jax.nn.sigmoid(_dot(_rms(h2, g2_ref[...]).astype(BF16), wpg_ref[...]))
    h3 = h2 + _dot(p_ref[...].astype(BF16), wpp_ref[...]) * gate
    o_ref[...] = _rms(h3, gf_ref[...]) if final_norm else h3


def _layer_tail(h, acts, p_all, layer, w_outs, ffn, pe, g_final, final_norm):
    n, d = h.shape
    g1, w_gate, w_up, w_down = ffn
    g2, w_pg, w_pp = pe
    assert w_gate.shape[1] % FF_CHUNK == 0
    tm = _row_tile(n, 512)
    row = lambda i: (i, 0)
    const = lambda i: (0, 0)
    resident = lambda w: pl.BlockSpec(w.shape, const, pipeline_mode=pl.Buffered(1))
    vec = pl.BlockSpec((1, d), const)
    in_specs = [pl.BlockSpec((tm, d), row)]
    in_specs += [pl.BlockSpec((tm, a.shape[1]), row) for a in acts]
    in_specs += [pl.BlockSpec((None, tm, p_all.shape[2]), lambda i: (layer, i, 0))]
    in_specs += [resident(w) for w in w_outs]
    in_specs += [vec, resident(w_gate), resident(w_up), resident(w_down), vec, resident(w_pg),
                 resident(w_pp), vec]
    return pl.pallas_call(
        functools.partial(_tail_kernel, n_in=len(acts), final_norm=final_norm),
        grid=(n // tm,),
        in_specs=in_specs,
        out_specs=pl.BlockSpec((tm, d), row),
        out_shape=jax.ShapeDtypeStruct((n, d), F32),
        scratch_shapes=[pltpu.VMEM((tm, w_gate.shape[1]), BF16)],
        compiler_params=_params("parallel"),
        name="tail_final" if final_norm else "tail",
    )(h, *acts, p_all, *w_outs, g1, w_gate, w_up, w_down, g2, w_pg, w_pp, g_final)


def _rope(y, cos, sin_lo, sin_hi):
    out = []
    for c in range(y.shape[1] // LANES):
        yc = y[:, c * LANES:(c + 1) * LANES]
        out.append(yc * cos + pltpu.roll(yc, 8, 1) * sin_hi + pltpu.roll(yc, LANES - 8, 1) * sin_lo)
    return out[0] if len(out) == 1 else jnp.concatenate(out, axis=1)


def _repeat_heads(y):
    out = []
    for c in range(y.shape[1] // LANES):
        pair = y[:, c * LANES:(c + 1) * LANES]
        swapped = pltpu.roll(pair, HEAD_DIM, 1)
        out += [_merge_heads(pair, swapped), _merge_heads(swapped, pair)]
    return jnp.concatenate(out, axis=1)


def _proj_c_kernel(x_ref, g_ref, w_ref, cos_ref, slo_ref, shi_ref,
                   k_ref, v_ref, qb_ref, kd_ref, vd_ref, *, q_w, kv_w, window_only, nper):
    xn = _rms(x_ref[...], g_ref[...]).astype(BF16)
    tabs = (cos_ref[...], slo_ref[...], shi_ref[...])
    qb_ref[...] = (_rope(_dot(xn, w_ref[:, :q_w]), *tabs) * SCALE).astype(BF16)
    k = _rope(_dot(xn, w_ref[:, q_w:q_w + kv_w]), *tabs)
    v = _dot(xn, w_ref[:, q_w + kv_w:q_w + 2 * kv_w])
    kd_ref[...] = _repeat_heads(k).astype(BF16)
    vd_ref[...] = _repeat_heads(v).astype(BF16)

    def f32_state():
        k_ref[...] = k[-k_ref.shape[0]:]
        v_ref[...] = v[-v_ref.shape[0]:]

    if window_only:
        pl.when(pl.program_id(0) % nper == nper - 1)(f32_state)
    else:
        f32_state()


def _proj_c(x, g, w_all, tables, period, q_w, kv_w, window_only):
    n, d = x.shape
    tm = _row_tile(period, 512)
    nper = period // tm
    row = lambda i: (i, 0)
    const = lambda i: (0, 0)
    tab = pl.BlockSpec((tm, LANES), lambda i: (i % nper, 0))
    if window_only:
        assert tm >= WINDOW
        state_spec = pl.BlockSpec((None, WINDOW, kv_w), lambda i: (i // nper, 0, 0))
        state_shape = jax.ShapeDtypeStruct((n // period, WINDOW, kv_w), F32)
    else:
        state_spec = pl.BlockSpec((tm, kv_w), row)
        state_shape = jax.ShapeDtypeStruct((n, kv_w), F32)
    return pl.pallas_call(
        functools.partial(_proj_c_kernel, q_w=q_w, kv_w=kv_w, window_only=window_only, nper=nper),
        grid=(n // tm,),
        in_specs=[pl.BlockSpec((tm, d), row), pl.BlockSpec((1, d), const),
                  pl.BlockSpec(w_all.shape, const), tab, tab, tab],
        out_specs=[state_spec, state_spec,
                   pl.BlockSpec((tm, q_w), row), pl.BlockSpec((tm, 2 * kv_w), row),
                   pl.BlockSpec((tm, 2 * kv_w), row)],
        out_shape=[state_shape, state_shape,
                   jax.ShapeDtypeStruct((n, q_w), BF16), jax.ShapeDtypeStruct((n, 2 * kv_w), BF16),
                   jax.ShapeDtypeStruct((n, 2 * kv_w), BF16)],
        compiler_params=_params("arbitrary"),
        name="proj_c",
    )(x, g, w_all, *tables)


def _rope_tables(pos):
    half = ROPE_DIM // 2
    inv_freq = ROPE_THETA ** (-2.0 * jnp.arange(half, dtype=F32) / ROPE_DIM)
    ang = pos.astype(F32)[:, None] * inv_freq[None, :]
    cos, sin = jnp.cos(ang), jnp.sin(ang)
    n = pos.shape[0]
    rest = HEAD_DIM - ROPE_DIM
    cos_h = jnp.concatenate([cos, cos, jnp.ones((n, rest), F32)], axis=1)
    lo_h = jnp.concatenate([-sin, jnp.zeros((n, half + rest), F32)], axis=1)
    hi_h = jnp.concatenate([jnp.zeros((n, half), F32), sin, jnp.zeros((n, rest), F32)], axis=1)
    two = lambda a: jnp.concatenate([a, a], axis=1)
    return two(cos_h), two(lo_h), two(hi_h)


SWA_UNROLL = 4


def _swa_kernel(sink_ref, q_ref, kprev_ref, kcur_ref, vprev_ref, vcur_ref, o_ref, kb_ref, vb_ref,
                *, tq, n_kv, group, first_pos_of_tile):
    band = WINDOW + CHUNK
    kb_ref[0:WINDOW, :] = kprev_ref[...]
    kb_ref[WINDOW:, :] = kcur_ref[...]
    vb_ref[0:WINDOW, :] = vprev_ref[...]
    vb_ref[WINDOW:, :] = vcur_ref[...]
    rows = group * CHUNK
    ridx = lax.broadcasted_iota(jnp.int32, (rows, 1), 0) // CHUNK
    col = lax.broadcasted_iota(jnp.int32, (rows, band), 1)
    tile_pos = None if first_pos_of_tile is None else first_pos_of_tile(pl.program_id(1))

    def chunk(cc, _):
        base = pl.multiple_of(cc * CHUNK, CHUNK)
        kv_heads = range(n_kv)
        scores = []
        for kvh in kv_heads:
            lhs = []
            for pr in range(group // 2):
                lane0 = (kvh * (group // 2) + pr) * LANES
                lhs.extend(_split_heads(q_ref[pl.ds(base, CHUNK), lane0:lane0 + LANES]))
            s = _dot_t(jnp.concatenate(lhs, axis=0), kb_ref[pl.ds(base, band), kvh * LANES:(kvh + 1) * LANES])
            if tile_pos is not None:
                s = jnp.where(tile_pos - WINDOW + cc * CHUNK + col >= 0, s, -jnp.inf)
            scores.append(s)
        weights, dens = [], []
        for kvh in kv_heads:
            sink = jnp.zeros((rows, 1), F32)
            for gi in range(group):
                sink = jnp.where(ridx == gi, sink_ref[kvh * group + gi], sink)
            m = jnp.maximum(jnp.max(scores[kvh], axis=-1, keepdims=True), sink)
            e = jnp.exp(scores[kvh] - m)
            dens.append(jnp.sum(e, axis=-1, keepdims=True) + jnp.exp(sink - m))
            weights.append(e.astype(BF16))
        for kvh in kv_heads:
            o = _dot(weights[kvh], vb_ref[pl.ds(base, band), kvh * LANES:(kvh + 1) * LANES]) / dens[kvh]
            for pr in range(group // 2):
                lane0 = (kvh * (group // 2) + pr) * LANES
                pair = _merge_heads(o[(2 * pr) * CHUNK:(2 * pr + 1) * CHUNK],
                                    o[(2 * pr + 1) * CHUNK:(2 * pr + 2) * CHUNK])
                o_ref[pl.ds(base, CHUNK), lane0:lane0 + LANES] = pair.astype(o_ref.dtype)
        return 0

    n_chunks = tq // CHUNK
    lax.fori_loop(0, n_chunks, chunk, 0, unroll=SWA_UNROLL if n_chunks % SWA_UNROLL == 0 else 1)


def _swa_prompt(sinks, q, kd, vd, batch, seq, n_kv):
    n, q_w = q.shape
    group = q_w // HEAD_DIM // n_kv
    tq = _row_tile(seq, 512)
    nt = seq // tq
    per_win = tq // WINDOW
    cur = lambda b, i: (b * nt + i, 0)
    prev = lambda b, i: (jnp.maximum(b * nt * per_win + i * per_win - 1, b * nt * per_win), 0)
    kv_w = kd.shape[1]
    return pl.pallas_call(
        functools.partial(_swa_kernel, tq=tq, n_kv=n_kv, group=group,
                          first_pos_of_tile=lambda i: i * tq),
        grid=(batch, nt),
        in_specs=[pl.BlockSpec(memory_space=pltpu.SMEM),
                  pl.BlockSpec((tq, q_w), cur),
                  pl.BlockSpec((WINDOW, kv_w), prev), pl.BlockSpec((tq, kv_w), cur),
                  pl.BlockSpec((WINDOW, kv_w), prev), pl.BlockSpec((tq, kv_w), cur)],
        out_specs=pl.BlockSpec((tq, q_w), cur),
        out_shape=jax.ShapeDtypeStruct((n, q_w), BF16),
        scratch_shapes=[pltpu.VMEM((WINDOW + tq, kv_w), BF16), pltpu.VMEM((WINDOW + tq, kv_w), BF16)],
        compiler_params=_params("parallel", "parallel"),
        name="swa_prompt",
    )(sinks, q, kd, kd, vd, vd)


def _swa_sample(sinks, q, kd_cache, kd_new, vd_cache, vd_new, batch, t, n_kv):
    n, q_w = q.shape
    group = q_w // HEAD_DIM // n_kv
    kv_w = kd_new.shape[1]
    blk = lambda b, i: (b, 0)
    return pl.pallas_call(
        functools.partial(_swa_kernel, tq=t, n_kv=n_kv, group=group, first_pos_of_tile=None),
        grid=(batch, 1),
        in_specs=[pl.BlockSpec(memory_space=pltpu.SMEM),
                  pl.BlockSpec((t, q_w), blk),
                  pl.BlockSpec((WINDOW, kv_w), blk), pl.BlockSpec((t, kv_w), blk),
                  pl.BlockSpec((WINDOW, kv_w), blk), pl.BlockSpec((t, kv_w), blk)],
        out_specs=pl.BlockSpec((t, q_w), blk),
        out_shape=jax.ShapeDtypeStruct((n, q_w), BF16),
        scratch_shapes=[pltpu.VMEM((WINDOW + t, kv_w), BF16), pltpu.VMEM((WINDOW + t, kv_w), BF16)],
        compiler_params=_params("parallel", "parallel"),
        name="swa_sample",
    )(sinks, q, kd_cache, kd_new, vd_cache, vd_new)


def _dup_heads(a, n_heads):
    lead = a.shape[:-1]
    a = a.reshape(*lead, n_heads, 1, HEAD_DIM)
    return jnp.broadcast_to(a, (*lead, n_heads, 2, HEAD_DIM)).reshape(*lead, n_heads * LANES)


def _prep_weights(W, n_fox, n_sb, n_kv):
    fox_w = n_fox * HEAD_DIM
    w_in = W["w_ab_in"][0]
    w_main = jnp.concatenate([w_in[:, :3 * fox_w], w_in[:, 3 * fox_w + n_fox:]], axis=1).astype(BF16)
    w_flog = jnp.pad(w_in[:, 3 * fox_w:3 * fox_w + n_fox], ((0, 0), (0, LANES - n_fox))).astype(BF16)
    b_f = jnp.pad(W["b_fox_f"][0][None, :], ((0, 0), (0, LANES - n_fox)))
    fq, fk, fv, flog = (w_in[:, :fox_w], w_in[:, fox_w:2 * fox_w], w_in[:, 2 * fox_w:3 * fox_w],
                        w_in[:, 3 * fox_w:3 * fox_w + n_fox])
    sq, sk, sv = jnp.split(w_in[:, 3 * fox_w + n_fox:], 3, axis=1)
    gate_rows = 16
    w_nt = jnp.concatenate([fq, sq], axis=1).astype(BF16)
    w_t = jnp.concatenate([fk, fv, sk, sv, jnp.pad(flog, ((0, 0), (0, gate_rows - n_fox)))],
                          axis=1).T.astype(BF16)
    b_col = jnp.pad(W["b_fox_f"][0], (0, gate_rows - n_fox))[:, None]
    w_c = W["w_c_in"][0]
    kv_w = n_kv * HEAD_DIM
    q_w = w_c.shape[1] - 2 * kv_w
    w_c_all = w_c.astype(BF16)
    bf = lambda name, i: W[name][i].astype(BF16)
    out = dict(w_main=w_main, w_flog=w_flog, b_f=b_f, w_nt=w_nt, w_t=w_t, b_col=b_col,
               w_c_all=w_c_all, q_w=q_w, kv_w=kv_w,
               w_ab_out_f=W["w_ab_out"][0][:fox_w].astype(BF16),
               w_ab_out_s=W["w_ab_out"][0][fox_w:].astype(BF16),
               w_c_out=bf("w_c_out", 0), c_sinks=W["c_sinks"][0])
    for i in range(2):
        out[f"ffn{i}"] = (W["norm_ffn"][i][None, :], bf("w_ffn_gate", i), bf("w_ffn_up", i),
                          bf("w_ffn_down", i))
        out[f"pe{i}"] = (W["norm_pe"][i][None, :], bf("w_pe_gate", i), bf("w_pe_proj", i))
    out["norm_mix"] = [W["norm_mix"][i][None, :] for i in range(2)]
    out["norm_final"] = W["norm_final"][None, :]
    return out


def _pad_lanes(a):
    pad = (-a.shape[1]) % LANES
    return jnp.pad(a, ((0, 0), (0, pad))) if pad else a


def _tail(h, acts, w_outs, p, P, i):
    p_all = p.reshape(p.shape[0], h.shape[0], p.shape[-1])
    return _layer_tail(h, acts, p_all, i, w_outs, P[f"ffn{i}"], P[f"pe{i}"], P["norm_final"],
                       final_norm=(i == p.shape[0] - 1))


def _trunk_prompt(x, p, P, n_fox, n_kv):
    batch, seq, d = x.shape
    n = batch * seq
    h = x.reshape(n, d)
    fkt, fvt, skt, svt, lft, fqb, sqb, fktb, fvtb, sktb, svtb = _proj_ab_t(
        h, P["norm_mix"][0], P["w_nt"], P["w_t"], P["b_col"], n_fox, batch, seq)
    tq_fox = _row_tile(seq, FOX_TILE)
    ck = _cumsum_rows(lft.reshape(batch * n_fox, seq))
    fo = _prompt_attention(fqb, fktb, fvtb, ck, batch, seq, tq_fox)
    so = _prompt_attention(sqb, sktb, svtb, None, batch, seq, _row_tile(seq, SB_TILE))
    h = _tail(h, [fo, so], [P["w_ab_out_f"], P["w_ab_out_s"]], p, P, 0)
    tables = _rope_tables(jnp.arange(seq))
    k, v, qb, kd, vd = _proj_c(h, P["norm_mix"][1], P["w_c_all"], tables, seq, P["q_w"], P["kv_w"],
                               window_only=True)
    o = _swa_prompt(P["c_sinks"], qb, kd, vd, batch, seq, n_kv)
    y = _tail(h, [o], [P["w_c_out"]], p, P, 1)
    heads = lambda a: jnp.transpose(a.reshape(batch, n_fox, HEAD_DIM, seq), (0, 3, 1, 2))[None]
    kw = k.reshape(1, batch, WINDOW, n_kv, HEAD_DIM)
    vw = v.reshape(1, batch, WINDOW, n_kv, HEAD_DIM)
    return y.reshape(batch, seq, d), (heads(fkt), heads(fvt), jnp.transpose(lft, (0, 2, 1))[None],
                                      heads(skt), heads(svt), kw, vw)


def _trunk_sample(x, p, P, caches, n_fox, n_kv):
    c_fk, c_fv, c_fl, c_sk, c_sv, c_wk, c_wv = caches
    batch, t, d = x.shape
    past = c_fk.shape[2]
    n = batch * t
    h = x.reshape(n, d)
    width = n_fox * HEAD_DIM
    fk, fv, sk, sv, logf, fqb, fkb, fvb, sqb, skb, svb = _proj_ab(
        h, P["norm_mix"][0], P["w_main"], P["w_flog"], P["b_f"], n_fox)
    lf_all = jnp.concatenate([jnp.swapaxes(c_fl[0], 1, 2),
                              jnp.swapaxes(logf.reshape(batch, t, n_fox), 1, 2)], axis=2)
    lf_rows = _pad_lanes(lf_all.reshape(batch * n_fox, past + t))
    c_rows = _cumsum_rows(lf_rows)
    tkc = _row_tile(past, 2048)
    ck_cache = c_rows[:, :past].reshape(batch * n_fox // 2, 2 * (past // tkc), tkc)
    ck_new = c_rows[:, past:past + t].reshape(batch * n_fox // 2, 2, t)
    cfk, cfv, csk, csv = [jnp.transpose(c[0], (0, 2, 3, 1)).reshape(batch, width, past)
                          for c in (c_fk, c_fv, c_sk, c_sv)]
    fo, so = _ab_sample_attention(fqb, fkb, fvb, cfk, cfv, ck_cache, ck_new,
                                  sqb, skb, svb, csk, csv, batch, t, past, tkc, _row_tile(past, 256))
    h = _tail(h, [fo, so], [P["w_ab_out_f"], P["w_ab_out_s"]], p, P, 0)
    tm = _row_tile(n, 512)
    pos = past + jnp.arange(t)
    tables = _rope_tables(jnp.tile(pos, max(tm // t, 1)))
    k, v, qb, kd, vd = _proj_c(h, P["norm_mix"][1], P["w_c_all"], tables, tables[0].shape[0],
                               P["q_w"], P["kv_w"], window_only=False)
    kd_cache = _dup_heads(c_wk[0].reshape(batch * WINDOW, n_kv * HEAD_DIM), n_kv).astype(BF16)
    vd_cache = _dup_heads(c_wv[0].reshape(batch * WINDOW, n_kv * HEAD_DIM), n_kv).astype(BF16)
    o = _swa_sample(P["c_sinks"], qb, kd_cache, kd, vd_cache, vd, batch, t, n_kv)
    y = _tail(h, [o], [P["w_c_out"]], p, P, 1)
    heads = lambda a, nh: a.reshape(1, batch, t, nh, HEAD_DIM)
    roll = lambda c, new: jnp.concatenate([c[0], new.reshape(batch, t, n_kv, HEAD_DIM)], axis=1)[None, :, -WINDOW:]
    return y.reshape(batch, t, d), (heads(fk, n_fox), heads(fv, n_fox),
                                    logf.reshape(1, batch, t, n_fox),
                                    heads(sk, n_fox), heads(sv, n_fox), roll(c_wk, k), roll(c_wv, v))


def kernel(x_prompt, x_sample, cache_fox_k, cache_fox_v, cache_fox_logf, cache_sb_k, cache_sb_v, cache_swa_k, cache_swa_v, p_prompt, p_sample, norm_mix, w_ab_in, b_fox_f, w_ab_out, w_c_in, c_sinks, w_c_out, norm_ffn, w_ffn_gate, w_ffn_up, w_ffn_down, norm_pe, w_pe_gate, w_pe_proj, norm_final):
    W = dict(norm_mix=norm_mix, w_ab_in=w_ab_in, b_fox_f=b_fox_f, w_ab_out=w_ab_out, w_c_in=w_c_in,
             c_sinks=c_sinks, w_c_out=w_c_out, norm_ffn=norm_ffn, w_ffn_gate=w_ffn_gate,
             w_ffn_up=w_ffn_up, w_ffn_down=w_ffn_down, norm_pe=norm_pe, w_pe_gate=w_pe_gate,
             w_pe_proj=w_pe_proj, norm_final=norm_final)
    n_fox = cache_fox_k.shape[3]
    n_sb = cache_sb_k.shape[3]
    n_kv = cache_swa_k.shape[3]
    assert n_fox == n_sb and n_fox % 2 == 0 and n_kv % 1 == 0
    assert cache_swa_k.shape[2] == WINDOW and x_sample.shape[1] == CHUNK
    P = _prep_weights(W, n_fox, n_sb, n_kv)
    caches = (cache_fox_k, cache_fox_v, cache_fox_logf, cache_sb_k, cache_sb_v, cache_swa_k, cache_swa_v)
    y_p, st_p = _trunk_prompt(x_prompt, p_prompt, P, n_fox, n_kv)
    y_s, st_s = _trunk_sample(x_sample, p_sample, P, caches, n_fox, n_kv)
    return (y_p, y_s, *st_p, *st_s)
```
